```python
import math
import jax, jax.numpy as jnp
from jax import lax
import numpy as np

D_MODEL = 2048
BATCH = 8
SEQ = 4096
DEPTH = 1
DEC_BATCH = 4
DEC_SEQ = 4096
PAST_LEN = 128

HEAD_DIM = 64
N_HEADS = D_MODEL // 128
N_KV_HEADS = N_HEADS // 4
D_ATTN = N_HEADS * HEAD_DIM
D_KV = N_KV_HEADS * HEAD_DIM
WINDOW = 128
BLOCK = 128
NUM_BUCKETS = 32
MAX_DISTANCE = 128
SSM_HEAD_DIM = 64
SSM_HEADS = (D_MODEL // 2) // SSM_HEAD_DIM
D_SSM = SSM_HEADS * SSM_HEAD_DIM
SSM_STATE = 128
SSM_GROUPS = 2
D_BC = SSM_GROUPS * SSM_STATE
D_CONV = 3
D_CONV_CH = D_SSM + 2 * D_BC
CHUNK = 128
D_MIX = D_ATTN + D_SSM
IN_PROJ = D_ATTN + 2 * D_KV + D_SSM + D_CONV_CH + 2 * SSM_HEADS
N_EXPERTS = 16
CAPACITY_FACTOR = 2
D_FF = 2048
EPS = 1e-6
NEG_INF = -1e30

kernel_name = "hymba_swa_ssd_ec_moe_encoder"


def rmsnorm(x, w):
    x32 = x.astype(jnp.float32)
    y = x32 * lax.rsqrt(jnp.mean(x32 * x32, axis=-1, keepdims=True) + EPS)
    return (y * w.astype(jnp.float32)).astype(x.dtype)


def grouped_rmsnorm(x, w, groups):
    shp = x.shape
    x32 = x.astype(jnp.float32).reshape(shp[:-1] + (groups, shp[-1] // groups))
    y = x32 * lax.rsqrt(jnp.mean(x32 * x32, axis=-1, keepdims=True) + EPS)
    return y.reshape(shp) * w.astype(jnp.float32)


def t5_buckets(rel):
    nb = NUM_BUCKETS // 2
    ret = (rel > 0).astype(np.int32) * nb
    n = np.abs(rel)
    max_exact = nb // 2
    large = max_exact + (np.log(np.maximum(n, 1) / max_exact) / np.log(MAX_DISTANCE / max_exact)
                         * (nb - max_exact)).astype(np.int32)
    large = np.minimum(large, nb - 1)
    return ret + np.where(n < max_exact, n, large)


def windowed_gqa_attention(q, k, v, attn_sink, rel_bias):
    b, S = q.shape[0], q.shape[1]
    nb = S // BLOCK
    G = N_HEADS // N_KV_HEADS
    qb = q.reshape(b, nb, BLOCK, N_KV_HEADS, G, HEAD_DIM)

    def band(t):
        tp = jnp.pad(t, ((0, 0), (BLOCK, BLOCK), (0, 0), (0, 0))).reshape(b, nb + 2, BLOCK, N_KV_HEADS, HEAD_DIM)
        return jnp.concatenate([tp[:, :-2], tp[:, 1:-1], tp[:, 2:]], axis=2)

    kb, vb = band(k), band(v)
    rel = np.arange(3 * BLOCK)[None, :] - BLOCK - np.arange(BLOCK)[:, None]
    bias = rel_bias[t5_buckets(rel)]
    bias = jnp.transpose(bias, (2, 0, 1)).reshape(N_KV_HEADS, G, BLOCK, 3 * BLOCK).astype(jnp.float32)
    kpos = np.arange(nb)[:, None] * BLOCK - BLOCK + np.arange(3 * BLOCK)[None, :]
    mask = ((kpos >= 0) & (kpos < S))[:, None, :] & (np.abs(rel) <= WINDOW)[None]

    s = jnp.einsum('bnqkgd,bnskd->bnkgqs', qb, kb).astype(jnp.float32) * (1.0 / math.sqrt(HEAD_DIM)) + bias
    s = jnp.where(mask[None, :, None, None], s, NEG_INF)
    sink = attn_sink.astype(jnp.float32).reshape(1, 1, N_KV_HEADS, G, 1, 1)
    m = jnp.maximum(s.max(axis=-1, keepdims=True), sink)
    p = jnp.exp(s - m)
    probs = p / (p.sum(axis=-1, keepdims=True) + jnp.exp(sink - m))
    o = jnp.einsum('bnkgqs,bnskd->bnqkgd', probs.astype(v.dtype), vb)
    return o.reshape(b, S, D_ATTN)


def centred_dwconv(u, w, bias):
    S = u.shape[1]
    pad = D_CONV // 2
    up = jnp.pad(u, ((0, 0), (pad, pad), (0, 0)))
    out = bias
    for i in range(D_CONV):
        out = out + up[:, i:i + S] * w[:, i]
    return out


def ssd_chunked(xh, dt, A, Bm, Cm):
    b, S = xh.shape[0], xh.shape[1]
    nc = S // CHUNK
    J = SSM_HEADS // SSM_GROUPS
    x = (xh * dt[..., None]).reshape(b, nc, CHUNK, SSM_GROUPS, J, SSM_HEAD_DIM)
    a = (dt * A).reshape(b, nc, CHUNK, SSM_GROUPS, J)
    a_cs = jnp.cumsum(a, axis=2)
    Bc = Bm.reshape(b, nc, CHUNK, SSM_GROUPS, SSM_STATE)
    Cc = Cm.reshape(b, nc, CHUNK, SSM_GROUPS, SSM_STATE)
    tri = np.tril(np.ones((CHUNK, CHUNK), dtype=bool))[None, None, :, :, None, None]
    seg = a_cs[:, :, :, None] - a_cs[:, :, None, :]
    Lm = jnp.exp(jnp.where(tri, seg, -jnp.inf))
    CB = jnp.einsum('bclgn,bcsgn->bclsg', Cc, Bc)
    y_diag = jnp.einsum('bclsgj,bcsgjp->bclgjp', CB[..., None] * Lm, x)
    decay_to_end = jnp.exp(a_cs[:, :, -1:] - a_cs)
    states = jnp.einsum('bclgn,bclgjp->bcgjpn', Bc, x * decay_to_end[..., None])
    chunk_decay = jnp.exp(a_cs[:, :, -1])

    def step(h, inp):
        s_c, d_c = inp
        return d_c[..., None, None] * h + s_c, h

    h0 = jnp.zeros(states.shape[:1] + states.shape[2:], states.dtype)
    _, h_prev = lax.scan(step, h0, (jnp.swapaxes(states, 0, 1), jnp.swapaxes(chunk_decay, 0, 1)))
    h_prev = jnp.swapaxes(h_prev, 0, 1)
    y_off = jnp.einsum('bclgn,bcgjpn->bclgjp', Cc, h_prev) * jnp.exp(a_cs)[..., None]
    return (y_diag + y_off).reshape(b, S, SSM_HEADS, SSM_HEAD_DIM)


def bidirectional_ssd(xs, z, Bm, Cm, dt_raw, conv_w, conv_b, dt_bias_fwd, dt_bias_bwd,
                      a_log_fwd, a_log_bwd, d_skip, ssm_norm_w):
    b, S = xs.shape[0], xs.shape[1]
    xbc = jax.nn.silu(centred_dwconv(jnp.concatenate([xs, Bm, Cm], axis=-1), conv_w, conv_b))
    xc = xbc[..., :D_SSM].reshape(b, S, SSM_HEADS, SSM_HEAD_DIM)
    Bc = xbc[..., D_SSM:D_SSM + D_BC].reshape(b, S, SSM_GROUPS, SSM_STATE)
    Cc = xbc[..., D_SSM + D_BC:].reshape(b, S, SSM_GROUPS, SSM_STATE)
    dt_f = jax.nn.softplus((dt_raw[..., :SSM_HEADS] + dt_bias_fwd).astype(jnp.float32))
    dt_b = jax.nn.softplus((dt_raw[..., SSM_HEADS:] + dt_bias_bwd).astype(jnp.float32))
    A_f = -jnp.exp(a_log_fwd.astype(jnp.float32))
    A_b = -jnp.exp(a_log_bwd.astype(jnp.float32))
    y_f = ssd_chunked(xc, dt_f, A_f, Bc, Cc)
    flip = lambda t: jnp.flip(t, axis=1)
    y_b = flip(ssd_chunked(flip(xc), flip(dt_b), A_b, flip(Bc), flip(Cc)))
    y = y_f + y_b + d_skip.astype(jnp.float32)[:, None] * xc
    y = y.reshape(b, S, D_SSM) * jax.nn.silu(z.astype(jnp.float32))
    return grouped_rmsnorm(y, ssm_norm_w, SSM_GROUPS).astype(xs.dtype)


def expert_choice_moe(h, router_w, w_gate, w_up, w_down):
    b, S, D = h.shape
    T = b * S
    cap = CAPACITY_FACTOR * T // N_EXPERTS
    t = h.reshape(T, D)
    aff = jax.nn.softmax((t @ router_w).astype(jnp.float32), axis=-1)
    g, idx = lax.top_k(aff.T, cap)
    xs = t[idx]
    hid = jax.nn.silu(jnp.einsum('ecd,edf->ecf', xs, w_gate)) * jnp.einsum('ecd,edf->ecf', xs, w_up)
    out = jnp.einsum('ecf,efd->ecd', hid, w_down) * g[..., None].astype(h.dtype)
    y = jnp.zeros((T, D), h.dtype).at[idx.reshape(-1)].add(out.reshape(-1, D).astype(h.dtype))
    return y.reshape(b, S, D)


def encoder_layer(x, rel_bias, norm1_w, w_in, conv_w, conv_b, dt_bias_fwd, dt_bias_bwd, a_log_fwd,
                  a_log_bwd, d_skip, ssm_norm_w, attn_sink, attn_norm_w, w_out, norm2_w, router_w,
                  w_gate, w_up, w_down):
    b, S = x.shape[0], x.shape[1]
    hn = rmsnorm(x, norm1_w)
    proj = hn @ w_in
    o1 = D_ATTN; o2 = o1 + D_KV; o3 = o2 + D_KV; o4 = o3 + D_SSM; o5 = o4 + D_SSM; o6 = o5 + D_BC; o7 = o6 + D_BC
    q = proj[..., :o1].reshape(b, S, N_HEADS, HEAD_DIM)
    k = proj[..., o1:o2].reshape(b, S, N_KV_HEADS, HEAD_DIM)
    v = proj[..., o2:o3].reshape(b, S, N_KV_HEADS, HEAD_DIM)
    z = proj[..., o3:o4]
    xs = proj[..., o4:o5]
    Bm = proj[..., o5:o6]
    Cm = proj[..., o6:o7]
    dt_raw = proj[..., o7:]
    attn = rmsnorm(windowed_gqa_attention(q, k, v, attn_sink, rel_bias), attn_norm_w)
    ssm = bidirectional_ssd(xs, z, Bm, Cm, dt_raw, conv_w, conv_b, dt_bias_fwd, dt_bias_bwd,
                            a_log_fwd, a_log_bwd, d_skip, ssm_norm_w)
    x = x + jnp.concatenate([attn, ssm], axis=-1) @ w_out
    x = x + expert_choice_moe(rmsnorm(x, norm2_w), router_w, w_gate, w_up, w_down)
    return x


def trunk(x, rel_bias, norm1_w, w_in, conv_w, conv_b, dt_bias_fwd, dt_bias_bwd, a_log_fwd, a_log_bwd,
          d_skip, ssm_norm_w, attn_sink, attn_norm_w, w_out, norm2_w, router_w, w_gate, w_up, w_down,
          final_norm_w):
    for l in range(DEPTH):
        x = encoder_layer(x, rel_bias, norm1_w[l], w_in[l], conv_w[l], conv_b[l], dt_bias_fwd[l],
                          dt_bias_bwd[l], a_log_fwd[l], a_log_bwd[l], d_skip[l], ssm_norm_w[l],
                          attn_sink[l], attn_norm_w[l], w_out[l], norm2_w[l], router_w[l],
                          w_gate[l], w_up[l], w_down[l])
    return rmsnorm(x, final_norm_w)


def setup_inputs(seed: int = 0) -> dict:
    key = jax.random.key(seed)
    ks = jax.random.split(key, 24)
    f32 = jnp.float32
    nrm = lambda k, shp, s: jax.random.normal(k, shp, f32) * s
    gain = lambda k, shp: 1.0 + 0.01 * jax.random.normal(k, shp, f32)

    def dt_bias(k):
        dt = jnp.exp(jax.random.uniform(k, (DEPTH, SSM_HEADS), f32, math.log(1e-3), math.log(1e-1)))
        return dt + jnp.log(-jnp.expm1(-dt))

    return {
        "x_prompt": jax.random.normal(ks[0], (BATCH, SEQ, D_MODEL), f32),
        "x_sample": jax.random.normal(ks[1], (DEC_BATCH, DEC_SEQ, D_MODEL), f32),
        "rel_bias": nrm(ks[2], (NUM_BUCKETS, N_HEADS), 0.2),
        "norm1_w": gain(ks[3], (DEPTH, D_MODEL)),
        "w_in": nrm(ks[4], (DEPTH, D_MODEL, IN_PROJ), D_MODEL ** -0.5),
        "conv_w": nrm(ks[5], (DEPTH, D_CONV_CH, D_CONV), D_CONV ** -0.5),
        "conv_b": nrm(ks[6], (DEPTH, D_CONV_CH), 0.01),
        "dt_bias_fwd": dt_bias(ks[7]),
        "dt_bias_bwd": dt_bias(ks[8]),
        "a_log_fwd": jnp.log(jax.random.uniform(ks[9], (DEPTH, SSM_HEADS), f32, 1.0, 16.0)),
        "a_log_bwd": jnp.log(jax.random.uniform(ks[10], (DEPTH, SSM_HEADS), f32, 1.0, 16.0)),
        "d_skip": gain(ks[11], (DEPTH, SSM_HEADS)),
        "ssm_norm_w": gain(ks[12], (DEPTH, D_SSM)),
        "attn_sink": nrm(ks[13], (DEPTH, N_HEADS), 0.5),
        "attn_norm_w": gain(ks[14], (DEPTH, D_ATTN)),
        "w_out": nrm(ks[15], (DEPTH, D_MIX, D_MODEL), D_MIX ** -0.5),
        "norm2_w": gain(ks[16], (DEPTH, D_MODEL)),
        "router_w": nrm(ks[17], (DEPTH, D_MODEL, N_EXPERTS), D_MODEL ** -0.5),
        "w_gate": nrm(ks[18], (DEPTH, N_EXPERTS, D_MODEL, D_FF), D_MODEL ** -0.5),
        "w_up": nrm(ks[19], (DEPTH, N_EXPERTS, D_MODEL, D_FF), D_MODEL ** -0.5),
        "w_down": nrm(ks[20], (DEPTH, N_EXPERTS, D_FF, D_MODEL), D_FF ** -0.5),
        "final_norm_w": gain(ks[21], (D_MODEL,)),
    }


def reference(x_prompt, x_sample, rel_bias, norm1_w, w_in, conv_w, conv_b, dt_bias_fwd, dt_bias_bwd,
              a_log_fwd, a_log_bwd, d_skip, ssm_norm_w, attn_sink, attn_norm_w, w_out, norm2_w,
              router_w, w_gate, w_up, w_down, final_norm_w):
    y_prompt = trunk(x_prompt, rel_bias, norm1_w, w_in, conv_w, conv_b, dt_bias_fwd, dt_bias_bwd,
                     a_log_fwd, a_log_bwd, d_skip, ssm_norm_w, attn_sink, attn_norm_w, w_out, norm2_w,
                     router_w, w_gate, w_up, w_down, final_norm_w)
    y_sample = trunk(x_sample, rel_bias, norm1_w, w_in, conv_w, conv_b, dt_bias_fwd, dt_bias_bwd,
                     a_log_fwd, a_log_bwd, d_skip, ssm_norm_w, attn_sink, attn_norm_w, w_out, norm2_w,
                     router_w, w_gate, w_up, w_down, final_norm_w)
    return (y_prompt, y_sample)
```

```python
import functools
import math

import numpy as np
import jax
import jax.numpy as jnp
from jax import lax
from jax.experimental import pallas as pl
from jax.experimental.pallas import tpu as pltpu

F32 = jnp.float32
BF16 = jnp.bfloat16
I32 = jnp.int32

D_MODEL = 2048
HEAD_DIM = 64
N_HEADS = 16
N_KV_HEADS = 4
D_ATTN = 1024
D_KV = 256
WINDOW = 128
BLOCK = 128
NUM_BUCKETS = 32
MAX_DISTANCE = 128
SSM_HEAD_DIM = 64
SSM_HEADS = 16
D_SSM = 1024
SSM_STATE = 128
SSM_GROUPS = 2
HEADS_PER_GROUP = SSM_HEADS // SSM_GROUPS
D_BC = 256
CHUNK = 128
N_EXPERTS = 16
CAPACITY_FACTOR = 2
D_FF = 2048
EPS = 1e-6
NEG_INF = -1e30

LANES = 128
HALO_ROWS = 8

COL_Q, COL_Z, COL_X, COL_B, COL_K, COL_V, COL_DT = 0, 1024, 2048, 3072, 3584, 3840, 4096
PROJ_COLS = COL_DT + LANES
PROJ_TN = 1408

VMEM_LIMIT = 56 * 1024 * 1024


def _cparams(n_axes, vmem=None):
    return pltpu.CompilerParams(dimension_semantics=("arbitrary",) * n_axes,
                                vmem_limit_bytes=vmem)


def _rms(x, w):
    return x * lax.rsqrt(jnp.mean(x * x, axis=-1, keepdims=True) + EPS) * w


def _silu(x):
    return x * jax.nn.sigmoid(x)


def _inproj_kernel(x_ref, nw_ref, w_ref, o_ref, hn_ref):
    @pl.when(pl.program_id(1) == 0)
    def _():
        hn_ref[...] = _rms(x_ref[...], nw_ref[...]).astype(BF16)

    o_ref[...] = jnp.dot(hn_ref[...], w_ref[...], preferred_element_type=F32)


def _inproj(x2, norm_w, w_all, tm):
    T = x2.shape[0]
    return pl.pallas_call(
        _inproj_kernel,
        grid=(T // tm, PROJ_COLS // PROJ_TN),
        in_specs=[pl.BlockSpec((tm, D_MODEL), lambda i, j: (i, 0)),
                  pl.BlockSpec((1, D_MODEL), lambda i, j: (0, 0)),
                  pl.BlockSpec((D_MODEL, PROJ_TN), lambda i, j: (0, j))],
        out_specs=pl.BlockSpec((tm, PROJ_TN), lambda i, j: (i, j)),
        out_shape=jax.ShapeDtypeStruct((T, PROJ_COLS), F32),
        scratch_shapes=[pltpu.VMEM((tm, D_MODEL), BF16)],
        compiler_params=_cparams(2, VMEM_LIMIT),
        name="inproj",
    )(x2, norm_w, w_all)


def _attn_kernel(q_ref, kp_ref, ko_ref, kn_ref, vp_ref, vo_ref, vn_ref, bias_ref, sink_ref, nw_ref,
                 o_ref, *, nb):
    i = pl.program_id(1)
    q = q_ref[...].astype(BF16)
    k = jnp.concatenate([kp_ref[...], ko_ref[...], kn_ref[...]], axis=0).astype(BF16)
    v = jnp.concatenate([vp_ref[...], vo_ref[...], vn_ref[...]], axis=0).astype(BF16)
    qi = lax.broadcasted_iota(I32, (BLOCK, 3 * BLOCK), 0)
    kj = lax.broadcasted_iota(I32, (BLOCK, 3 * BLOCK), 1)
    rel = kj - BLOCK - qi
    kpos = i * BLOCK - BLOCK + kj
    mask = (jnp.abs(rel) <= WINDOW) & (kpos >= 0) & (kpos < nb * BLOCK)
    scale = 1.0 / math.sqrt(HEAD_DIM)
    group = N_HEADS // N_KV_HEADS
    outs = []
    for h in range(N_HEADS):
        g = h // group
        qh = q[:, h * HEAD_DIM:(h + 1) * HEAD_DIM]
        kh = k[:, g * HEAD_DIM:(g + 1) * HEAD_DIM]
        vh = v[:, g * HEAD_DIM:(g + 1) * HEAD_DIM]
        s = lax.dot_general(qh, kh, (((1,), (1,)), ((), ())), preferred_element_type=F32) * scale + bias_ref[h]
        s = jnp.where(mask, s, NEG_INF)
        sk = sink_ref[h]
        m = jnp.maximum(jnp.max(s, axis=-1, keepdims=True), sk)
        p = jnp.exp(s - m)
        den = jnp.sum(p, axis=-1, keepdims=True) + jnp.exp(sk - m)
        probs = p / den
        outs.append(jnp.dot(probs.astype(BF16), vh, preferred_element_type=F32))
    o = jnp.concatenate(outs, axis=-1)
    o_ref[...] = _rms(o, nw_ref[...]).astype(BF16)


def _attention(proj3, bias_tab, sink, norm_w):
    B, S, _ = proj3.shape
    nb = S // BLOCK
    kcol, vcol = COL_K // D_KV, COL_V // D_KV
    prev = lambda i: jnp.maximum(i - 1, 0)
    nxt = lambda i: jnp.minimum(i + 1, nb - 1)
    kv = lambda col, f: pl.BlockSpec((None, BLOCK, D_KV), lambda b, i: (b, f(i), col))
    same = lambda i: i
    return pl.pallas_call(
        functools.partial(_attn_kernel, nb=nb),
        grid=(B, nb),
        in_specs=[pl.BlockSpec((None, BLOCK, D_ATTN), lambda b, i: (b, i, COL_Q // D_ATTN)),
                  kv(kcol, prev), kv(kcol, same), kv(kcol, nxt),
                  kv(vcol, prev), kv(vcol, same), kv(vcol, nxt),
                  pl.BlockSpec((N_HEADS, BLOCK, 3 * BLOCK), lambda b, i: (0, 0, 0)),
                  pl.BlockSpec(memory_space=pltpu.SMEM),
                  pl.BlockSpec((1, D_ATTN), lambda b, i: (0, 0))],
        out_specs=pl.BlockSpec((None, BLOCK, D_ATTN), lambda b, i: (b, i, 0)),
        out_shape=jax.ShapeDtypeStruct((B, S, D_ATTN), BF16),
        compiler_params=_cparams(2),
        name="attention",
    )(proj3, proj3, proj3, proj3, proj3, proj3, proj3, bias_tab, sink, norm_w)


def _conv_silu(u, prev8, next8, w, b, has_prev, has_next):
    prev_row = jnp.where(has_prev, prev8[HALO_ROWS - 1:HALO_ROWS, :], 0.0)
    next_row = jnp.where(has_next, next8[0:1, :], 0.0)
    rid = lax.broadcasted_iota(I32, u.shape, 0)
    um = jnp.where(rid == 0, prev_row, pltpu.roll(u, 1, 0))
    up = jnp.where(rid == CHUNK - 1, next_row, pltpu.roll(u, CHUNK - 1, 0))
    y = b + um * w[0:1, :] + u * w[1:2, :] + up * w[2:3, :]
    return _silu(y)


def _ssd_kernel(*refs, reverse, nc):
    if reverse:
        (z_ref, yf_ref, x_ref, xp_ref, xn_ref, bc_ref, bcp_ref, bcn_ref, dt_ref, cw_ref, cb_ref, dtb_ref,
         alog_ref, dskip_ref, nw_ref, o_ref, h_ref) = refs
    else:
        (x_ref, xp_ref, xn_ref, bc_ref, bcp_ref, bcn_ref, dt_ref, cw_ref, cb_ref, dtb_ref,
         alog_ref, o_ref, h_ref) = refs
    step = pl.program_id(1)
    c = (nc - 1 - step) if reverse else step

    @pl.when(step == 0)
    def _():
        h_ref[...] = jnp.zeros_like(h_ref)

    has_prev, has_next = c > 0, c < nc - 1
    cw, cb = cw_ref[...], cb_ref[...]
    xc = _conv_silu(x_ref[...], xp_ref[...], xn_ref[...], cw[:, :D_SSM], cb[:, :D_SSM], has_prev, has_next)
    bcv = _conv_silu(bc_ref[...], bcp_ref[...], bcn_ref[...], cw[:, D_SSM:], cb[:, D_SSM:], has_prev, has_next)

    raw = dt_ref[...] + dtb_ref[...]
    dt = jnp.maximum(raw, 0.0) + jnp.log1p(jnp.exp(-jnp.abs(raw)))
    a = dt * (-jnp.exp(alog_ref[...]))
    li = lax.broadcasted_iota(I32, (CHUNK, CHUNK), 0)
    si = lax.broadcasted_iota(I32, (CHUNK, CHUNK), 1)
    incl = jnp.dot(jnp.where(li >= si, 1.0, 0.0).astype(F32), a, precision=lax.Precision.HIGHEST,
                   preferred_element_type=F32)
    tot = incl[CHUNK - 1:CHUNK, :]
    if reverse:
        pcs = incl - a
        dstate = jnp.exp(pcs)
        yscale = jnp.exp(tot - pcs)
        mask = si >= li
    else:
        pcs = incl
        dstate = jnp.exp(tot - pcs)
        yscale = jnp.exp(pcs)
        mask = li >= si
    pcs_t = pcs.T
    cdec = jnp.exp(tot)
    lane0 = SSM_HEADS if reverse else 0
    P = SSM_HEAD_DIM
    gw = HEADS_PER_GROUP * P

    y_groups = []
    for g in range(SSM_GROUPS):
        bg = bcv[:, g * SSM_STATE:(g + 1) * SSM_STATE]
        cg = bcv[:, D_BC + g * SSM_STATE:D_BC + (g + 1) * SSM_STATE].astype(BF16)
        cbm = lax.dot_general(cg, bg.astype(BF16), (((1,), (1,)), ((), ())), preferred_element_type=F32)
        hg = h_ref[g]
        yoff = jnp.dot(cg, hg.astype(BF16), preferred_element_type=F32)
        ys, xs, cds = [], [], []
        for jj in range(HEADS_PER_GROUP):
            j = g * HEADS_PER_GROUP + jj
            ln = lane0 + j
            col = pcs[:, ln:ln + 1]
            row = pcs_t[ln:ln + 1, :]
            seg = (row - col) if reverse else (col - row)
            lm = jnp.exp(jnp.where(mask, seg, -jnp.inf))
            wmat = (cbm * lm).astype(BF16)
            xdt = xc[:, j * P:(j + 1) * P] * dt[:, ln:ln + 1]
            yd = jnp.dot(wmat, xdt.astype(BF16), preferred_element_type=F32)
            ys.append(yd + yoff[:, jj * P:(jj + 1) * P] * yscale[:, ln:ln + 1])
            xs.append((xdt * dstate[:, ln:ln + 1]).astype(BF16))
            cds.append(jnp.broadcast_to(cdec[:, ln:ln + 1], (1, P)))
        xs = jnp.concatenate(xs, axis=1)
        snew = jnp.dot(bg.T.astype(BF16), xs, preferred_element_type=F32)
        h_ref[g] = hg * jnp.concatenate(cds, axis=1) + snew
        y_groups.append(jnp.concatenate(ys, axis=1))
    y = jnp.concatenate(y_groups, axis=1)

    if not reverse:
        o_ref[...] = y
    else:
        ytot = yf_ref[...] + y + dskip_ref[...] * xc
        yz = ytot * _silu(z_ref[...])
        halves = []
        for g in range(SSM_GROUPS):
            seg = yz[:, g * gw:(g + 1) * gw]
            halves.append(seg * lax.rsqrt(jnp.mean(seg * seg, axis=-1, keepdims=True) + EPS))
        o_ref[...] = (jnp.concatenate(halves, axis=1) * nw_ref[...]).astype(BF16)


def _ssd_pass(proj3, conv_w3, conv_b, dtb, alog, reverse, yf=None, dskip=None, norm_w=None):
    B, S, _ = proj3.shape
    nc = S // CHUNK
    hp = CHUNK // HALO_ROWS
    nh = S // HALO_ROWS
    ch = (lambda s: nc - 1 - s) if reverse else (lambda s: s)
    main = lambda w, col: pl.BlockSpec((None, CHUNK, w), lambda b, s: (b, ch(s), col // w))
    halo_p = lambda w, col: pl.BlockSpec((None, HALO_ROWS, w),
                                         lambda b, s: (b, jnp.maximum(ch(s) * hp - 1, 0), col // w))
    halo_n = lambda w, col: pl.BlockSpec((None, HALO_ROWS, w),
                                         lambda b, s: (b, jnp.minimum((ch(s) + 1) * hp, nh - 1), col // w))
    full = lambda a: pl.BlockSpec(a.shape, lambda b, s: (0,) * a.ndim)
    d_bc2 = 2 * D_BC
    in_specs = [main(D_SSM, COL_X), halo_p(D_SSM, COL_X), halo_n(D_SSM, COL_X),
                main(d_bc2, COL_B), halo_p(d_bc2, COL_B), halo_n(d_bc2, COL_B),
                main(LANES, COL_DT), full(conv_w3), full(conv_b), full(dtb), full(alog)]
    args = [proj3, proj3, proj3, proj3, proj3, proj3, proj3, conv_w3, conv_b, dtb, alog]
    if reverse:
        in_specs = [main(D_SSM, COL_Z), pl.BlockSpec((None, CHUNK, D_SSM), lambda b, s: (b, ch(s), 0))] + in_specs
        in_specs += [full(dskip), full(norm_w)]
        args = [proj3, yf] + args + [dskip, norm_w]
    return pl.pallas_call(
        functools.partial(_ssd_kernel, reverse=reverse, nc=nc),
        grid=(B, nc),
        in_specs=in_specs,
        out_specs=pl.BlockSpec((None, CHUNK, D_SSM), lambda b, s: (b, ch(s), 0)),
        out_shape=jax.ShapeDtypeStruct((B, S, D_SSM), BF16 if reverse else F32),
        scratch_shapes=[pltpu.VMEM((SSM_GROUPS, SSM_STATE, HEADS_PER_GROUP * SSM_HEAD_DIM), F32)],
        compiler_params=_cparams(2),
        name="ssd_bwd" if reverse else "ssd_fwd",
    )(*args)


def _outproj_kernel(x_ref, a_ref, s_ref, w_ref, n2_ref, rw_ref, x1_ref, h_ref, aff_ref):
    x1 = (x_ref[...]
          + jnp.dot(a_ref[...], w_ref[:D_ATTN, :], preferred_element_type=F32)
          + jnp.dot(s_ref[...], w_ref[D_ATTN:, :], preferred_element_type=F32))
    x1_ref[...] = x1
    hn = _rms(x1, n2_ref[...])
    h_ref[...] = hn.astype(BF16)
    logits = jnp.dot(hn, rw_ref[...], precision=lax.Precision.HIGHEST, preferred_element_type=F32)
    e = jnp.exp(logits - jnp.max(logits, axis=-1, keepdims=True))
    aff_ref[...] = e / jnp.sum(e, axis=-1, keepdims=True)


def _outproj(x2, attn2, ssm2, w_out, norm2_w, router_w, tm):
    T = x2.shape[0]
    row = lambda w: pl.BlockSpec((tm, w), lambda i: (i, 0))
    full = lambda a: pl.BlockSpec(a.shape, lambda i: (0,) * a.ndim)
    return pl.pallas_call(
        _outproj_kernel,
        grid=(T // tm,),
        in_specs=[row(D_MODEL), row(D_ATTN), row(D_SSM), full(w_out), full(norm2_w), full(router_w)],
        out_specs=[row(D_MODEL), row(D_MODEL), row(N_EXPERTS)],
        out_shape=[jax.ShapeDtypeStruct((T, D_MODEL), F32),
                   jax.ShapeDtypeStruct((T, D_MODEL), BF16),
                   jax.ShapeDtypeStruct((T, N_EXPERTS), F32)],
        compiler_params=_cparams(1, VMEM_LIMIT),
        name="outproj_router",
    )(x2, attn2, ssm2, w_out, norm2_w, router_w)


def _count(m):
    c = jnp.sum(jnp.where(m, 1.0, 0.0), axis=0, keepdims=True)
    return jnp.sum(c, axis=1, keepdims=True)


def _cumsum_rowmajor(m):
    R = m.shape[0]
    mb = jnp.where(m, 1.0, 0.0).astype(BF16)
    ki = lax.broadcasted_iota(I32, (LANES, LANES), 0)
    kl = lax.broadcasted_iota(I32, (LANES, LANES), 1)
    within = jnp.dot(mb, jnp.where(ki <= kl, 1.0, 0.0).astype(BF16), preferred_element_type=F32)
    ri = lax.broadcasted_iota(I32, (R, R), 0)
    rk = lax.broadcasted_iota(I32, (R, R), 1)
    above = jnp.dot(jnp.where(rk < ri, 1.0, 0.0).astype(BF16), mb, preferred_element_type=F32)
    before = jnp.dot(above.astype(BF16), jnp.ones((LANES, LANES), BF16), preferred_element_type=F32)
    return within + before


def _topk_kernel(a_ref, pos_ref, *, cap):
    bits = pltpu.bitcast(a_ref[...], I32)
    capf = jnp.float32(cap)

    def body(k, tau):
        cand = tau | lax.shift_left(jnp.int32(1), 30 - k)
        return jnp.where(_count(bits >= cand) >= capf, cand, tau)

    tau = lax.fori_loop(0, 31, body, jnp.zeros((1, 1), I32))
    gt = bits > tau
    eq = bits == tau
    need = capf - _count(gt)
    sel = gt | (eq & (_cumsum_rowmajor(eq) <= need))
    pos = _cumsum_rowmajor(sel) - 1.0
    pos_ref[...] = jnp.where(sel, pos, -1.0).astype(I32)


def _topk(aff_t3, cap):
    E, R, _ = aff_t3.shape
    assert R <= 256
    return pl.pallas_call(
        functools.partial(_topk_kernel, cap=cap),
        grid=(E,),
        in_specs=[pl.BlockSpec((None, R, LANES), lambda e: (e, 0, 0))],
        out_specs=pl.BlockSpec((None, R, LANES), lambda e: (e, 0, 0)),
        out_shape=jax.ShapeDtypeStruct((E, R, LANES), I32),
        compiler_params=_cparams(1),
        name="topk",
    )(aff_t3)


def _dispatch_kernel(pt_ref, tt_ref, flag_ref, pos_ref, h_ref, o_ref, *, tp, n_pairs, rows):
    idx = pl.program_id(0) * n_pairs + pl.program_id(1)
    flag = flag_ref[idx]

    @pl.when(flag > 0)
    def _():
        slot = lax.broadcasted_iota(I32, (tp, LANES), 0) + pt_ref[idx] * tp
        acc = None
        for r in range(rows):
            sel_t = jnp.where(slot == pos_ref[r:r + 1, :], 1.0, 0.0).astype(BF16)
            d = jnp.dot(sel_t, h_ref[r * LANES:(r + 1) * LANES, :], preferred_element_type=F32)
            acc = d if acc is None else acc + d
        acc = acc.astype(BF16)

        @pl.when(flag == 2)
        def _():
            o_ref[...] = acc

        @pl.when(flag == 1)
        def _():
            o_ref[...] = o_ref[...] + acc


def _dispatch(pt, tt, flag, pos4, h, cap, tp, n_pairs):
    E, ntt, rows, _ = pos4.shape
    tb = rows * LANES
    grid_spec = pltpu.PrefetchScalarGridSpec(
        num_scalar_prefetch=3,
        grid=(E, n_pairs),
        in_specs=[pl.BlockSpec((None, None, rows, LANES), lambda e, k, pt, tt, fl: (e, tt[e * n_pairs + k], 0, 0)),
                  pl.BlockSpec((tb, D_MODEL), lambda e, k, pt, tt, fl: (tt[e * n_pairs + k], 0))],
        out_specs=pl.BlockSpec((None, tp, D_MODEL), lambda e, k, pt, tt, fl: (e, pt[e * n_pairs + k], 0)),
    )
    return pl.pallas_call(
        functools.partial(_dispatch_kernel, tp=tp, n_pairs=n_pairs, rows=rows),
        grid_spec=grid_spec,
        out_shape=jax.ShapeDtypeStruct((E, cap, D_MODEL), BF16),
        compiler_params=_cparams(2),
        name="dispatch",
    )(pt, tt, flag, pos4, h)


def _ffn_kernel(xs_ref, wg_ref, wu_ref, wd_ref, o_ref, acc_ref):
    f = pl.program_id(2)
    xs = xs_ref[...]
    g = jnp.dot(xs, wg_ref[...].astype(BF16), preferred_element_type=F32)
    u = jnp.dot(xs, wu_ref[...].astype(BF16), preferred_element_type=F32)
    hid = (_silu(g) * u).astype(BF16)
    d = jnp.dot(hid, wd_ref[...].astype(BF16), preferred_element_type=F32)

    @pl.when(f == 0)
    def _():
        acc_ref[...] = d

    @pl.when(f > 0)
    def _():
        acc_ref[...] += d

    @pl.when(f == pl.num_programs(2) - 1)
    def _():
        o_ref[...] = acc_ref[...].astype(BF16)


def _ffn(xs, w_gate, w_up, w_down, tm, tf):
    E, cap, _ = xs.shape
    return pl.pallas_call(
        _ffn_kernel,
        grid=(E, cap // tm, D_FF // tf),
        in_specs=[pl.BlockSpec((None, tm, D_MODEL), lambda e, m, f: (e, m, 0)),
                  pl.BlockSpec((None, D_MODEL, tf), lambda e, m, f: (e, 0, f)),
                  pl.BlockSpec((None, D_MODEL, tf), lambda e, m, f: (e, 0, f)),
                  pl.BlockSpec((None, tf, D_MODEL), lambda e, m, f: (e, f, 0))],
        out_specs=pl.BlockSpec((None, tm, D_MODEL), lambda e, m, f: (e, m, 0)),
        out_shape=jax.ShapeDtypeStruct((E, cap, D_MODEL), BF16),
        scratch_shapes=[pltpu.VMEM((tm, D_MODEL), F32)],
        compiler_params=_cparams(3, VMEM_LIMIT),
        name="expert_ffn",
    )(xs, w_gate, w_up, w_down)


def _combine_kernel(ptk_ref, val_ref, x1_ref, post_ref, aff_ref, eo_ref, nw_ref, out_ref, acc_ref, *, tp):
    i, e, k = pl.program_id(0), pl.program_id(1), pl.program_id(2)
    idx = (i * N_EXPERTS + e) * 2 + k
    tb = x1_ref.shape[0]

    @pl.when((e == 0) & (k == 0))
    def _():
        acc_ref[...] = x1_ref[...]

    @pl.when(val_ref[idx] > 0)
    def _():
        lane = lax.broadcasted_iota(I32, (tb, N_EXPERTS), 1)
        pcol = jnp.sum(jnp.where(lane == e, post_ref[...], 0.0), axis=1, keepdims=True)
        gcol = jnp.sum(jnp.where(lane == e, aff_ref[...], 0.0), axis=1, keepdims=True)
        slot = (lax.broadcasted_iota(I32, (tb, tp), 1) + ptk_ref[idx] * tp).astype(F32)
        sel = jnp.where(pcol == slot, 1.0, 0.0).astype(BF16)
        d = jnp.dot(sel, eo_ref[...], preferred_element_type=F32)
        acc_ref[...] += d * gcol

    @pl.when((e == N_EXPERTS - 1) & (k == 1))
    def _():
        out_ref[...] = _rms(acc_ref[...], nw_ref[...])


def _combine(ptk, val, x1, pos_t, aff, eo, final_w, tb, tp):
    T = x1.shape[0]
    ntt = T // tb
    pair = lambda i, e, k: (i * N_EXPERTS + e) * 2 + k
    grid_spec = pltpu.PrefetchScalarGridSpec(
        num_scalar_prefetch=2,
        grid=(ntt, N_EXPERTS, 2),
        in_specs=[pl.BlockSpec((tb, D_MODEL), lambda i, e, k, pt, vl: (i, 0)),
                  pl.BlockSpec((tb, N_EXPERTS), lambda i, e, k, pt, vl: (i, 0)),
                  pl.BlockSpec((tb, N_EXPERTS), lambda i, e, k, pt, vl: (i, 0)),
                  pl.BlockSpec((None, tp, D_MODEL), lambda i, e, k, pt, vl: (e, pt[pair(i, e, k)], 0)),
                  pl.BlockSpec((1, D_MODEL), lambda i, e, k, pt, vl: (0, 0))],
        out_specs=pl.BlockSpec((tb, D_MODEL), lambda i, e, k, pt, vl: (i, 0)),
        scratch_shapes=[pltpu.VMEM((tb, D_MODEL), F32)],
    )
    return pl.pallas_call(
        functools.partial(_combine_kernel, tp=tp),
        grid_spec=grid_spec,
        out_shape=jax.ShapeDtypeStruct((T, D_MODEL), F32),
        compiler_params=_cparams(3),
        name="combine_final",
    )(ptk, val, x1, pos_t, aff, eo, final_w)


def _schedule(pos, cap, tb, tp):
    E, R, _ = pos.shape
    rows = tb // LANES
    ntt = R // rows
    npt = cap // tp
    cnt = jnp.sum((pos >= 0).reshape(E, ntt, tb), axis=-1).astype(I32)
    start = jnp.cumsum(cnt, axis=1) - cnt
    first_pt = jnp.minimum(start, cap - 1) // tp
    last_pt = jnp.where(cnt > 0, (start + cnt - 1) // tp, first_pt)
    n = last_pt - first_pt + 1
    ends = jnp.cumsum(n, axis=1)
    offs = ends - n
    n_pairs = ntt + npt
    ks = jnp.arange(n_pairs, dtype=I32)
    tile = jax.vmap(lambda en: jnp.searchsorted(en, ks, side="right"))(ends).astype(I32)
    valid = ks[None, :] < ends[:, -1:]
    tile = jnp.minimum(tile, ntt - 1)
    ptile = jnp.take_along_axis(first_pt, tile, axis=1) + ks[None, :] - jnp.take_along_axis(offs, tile, axis=1)
    ptile = jnp.where(valid, ptile, last_pt[:, -1:])
    prev_pt = jnp.concatenate([jnp.full((E, 1), -1, I32), ptile[:, :-1]], axis=1)
    flag = jnp.where(valid, jnp.where(ptile != prev_pt, 2, 1), 0).astype(I32)
    pt0 = first_pt.T
    two = (n.T == 2)
    ptk = jnp.stack([pt0, jnp.where(two, pt0 + 1, pt0)], axis=-1).astype(I32)
    val = jnp.stack([cnt.T > 0, two], axis=-1).astype(I32)
    return (ptile.reshape(-1), tile.reshape(-1), flag.reshape(-1), n_pairs, ptk.reshape(-1), val.reshape(-1))


def _t5_buckets(rel):
    nb = NUM_BUCKETS // 2
    ret = (rel > 0).astype(np.int32) * nb
    n = np.abs(rel)
    max_exact = nb // 2
    large = max_exact + (np.log(np.maximum(n, 1) / max_exact) / np.log(MAX_DISTANCE / max_exact)
                         * (nb - max_exact)).astype(np.int32)
    large = np.minimum(large, nb - 1)
    return ret + np.where(n < max_exact, n, large)


def _tile(n, pref):
    t = min(n, pref)
    assert n % t == 0
    return t


def _trunk(x, w):
    B, S, _ = x.shape
    T = B * S
    cap = CAPACITY_FACTOR * T // N_EXPERTS
    x2 = x.reshape(T, D_MODEL)
    tm = _tile(T, 512)

    proj = _inproj(x2, w["norm1_w"], w["w_in"], tm)
    proj3 = proj.reshape(B, S, PROJ_COLS)
    attn = _attention(proj3, w["bias_tab"], w["attn_sink"], w["attn_norm_w"])
    yf = _ssd_pass(proj3, w["conv_w"], w["conv_b"], w["dtb"], w["alog"], reverse=False)
    ssm = _ssd_pass(proj3, w["conv_w"], w["conv_b"], w["dtb"], w["alog"], reverse=True,
                    yf=yf, dskip=w["dskip"], norm_w=w["ssm_norm_w"])
    x1, h, aff = _outproj(x2, attn.reshape(T, D_ATTN), ssm.reshape(T, D_SSM), w["w_out"], w["norm2_w"],
                          w["router_w"], tm)

    tb = _tile(T, 256)
    tp = _tile(cap, 256)
    pos = _topk(aff.T.reshape(N_EXPERTS, T // LANES, LANES), cap)
    pt, tt, flag, n_pairs, ptk, val = _schedule(pos, cap, tb, tp)
    xs = _dispatch(pt, tt, flag, pos.reshape(N_EXPERTS, T // tb, tb // LANES, LANES), h, cap, tp, n_pairs)
    eo = _ffn(xs, w["w_gate"], w["w_up"], w["w_down"], _tile(cap, 1024), 256)
    pos_t = pos.reshape(N_EXPERTS, T).T.astype(F32)
    y = _combine(ptk, val, x1, pos_t, aff, eo, w["final_norm_w"], tb, tp)
    return y.reshape(B, S, D_MODEL)


def _prep_weights(rel_bias, norm1_w, w_in, conv_w, conv_b, dt_bias_fwd, dt_bias_bwd, a_log_fwd, a_log_bwd,
                  d_skip, ssm_norm_w, attn_sink, attn_norm_w, w_out, norm2_w, router_w, w_gate, w_up, w_down,
                  final_norm_w):
    o1 = D_ATTN; o2 = o1 + D_KV; o3 = o2 + D_KV; o4 = o3 + D_SSM; o5 = o4 + D_SSM + 2 * D_BC
    wi = w_in[0]
    w_all = jnp.concatenate([wi[:, :o1], wi[:, o3:o4], wi[:, o4:o5], wi[:, o1:o2], wi[:, o2:o3], wi[:, o5:],
                             jnp.zeros((D_MODEL, LANES - 2 * SSM_HEADS), F32)], axis=1).astype(BF16)
    rel = np.arange(3 * BLOCK)[None, :] - BLOCK - np.arange(BLOCK)[:, None]
    bias_tab = jnp.transpose(rel_bias[_t5_buckets(rel)], (2, 0, 1)).astype(F32)
    pad = jnp.zeros((LANES - 2 * SSM_HEADS,), F32)
    row = lambda a: a.reshape(1, -1).astype(F32)
    return dict(
        norm1_w=row(norm1_w[0]), w_in=w_all, bias_tab=bias_tab, attn_sink=attn_sink[0].astype(F32),
        attn_norm_w=row(attn_norm_w[0]), conv_w=conv_w[0].T.astype(F32), conv_b=row(conv_b[0]),
        dtb=row(jnp.concatenate([dt_bias_fwd[0], dt_bias_bwd[0], pad])),
        alog=row(jnp.concatenate([a_log_fwd[0], a_log_bwd[0], pad])),
        dskip=row(jnp.repeat(d_skip[0], SSM_HEAD_DIM)), ssm_norm_w=row(ssm_norm_w[0]),
        w_out=w_out[0].astype(BF16), norm2_w=row(norm2_w[0]), router_w=router_w[0].astype(F32),
        w_gate=w_gate[0], w_up=w_up[0], w_down=w_down[0], final_norm_w=row(final_norm_w))


def kernel(x_prompt, x_sample, rel_bias, norm1_w, w_in, conv_w, conv_b, dt_bias_fwd, dt_bias_bwd, a_log_fwd, a_log_bwd, d_skip, ssm_norm_w, attn_sink, attn_norm_w, w_out, norm2_w, router_w, w_gate, w_up, w_down, final_norm_w):
    assert norm1_w.shape[0] == 1
    w = _prep_weights(rel_bias, norm1_w, w_in, conv_w, conv_b, dt_bias_fwd, dt_bias_bwd, a_log_fwd, a_log_bwd,
                      d_skip, ssm_norm_w, attn_sink, attn_norm_w, w_out, norm2_w, router_w, w_gate, w_up,
                      w_down, final_norm_w)
    return (_trunk(x_prompt, w), _trunk(x_sample, w))
```

```python
import functools
import math

import numpy as np
import jax
import jax.numpy as jnp
from jax import lax
from jax.experimental import pallas as pl
from jax.experimental.pallas import tpu as pltpu

F32 = jnp.float32
BF16 = jnp.bfloat16
I32 = jnp.int32

D_MODEL = 2048
HEAD_DIM = 64
N_HEADS = 16
N_KV_HEADS = 4
D_ATTN = 1024
D_KV = 256
WINDOW = 128
BLOCK = 128
NUM_BUCKETS = 32
MAX_DISTANCE = 128
SSM_HEAD_DIM = 64
SSM_HEADS = 16
D_SSM = 1024
SSM_STATE = 128
SSM_GROUPS = 2
HEADS_PER_GROUP = SSM_HEADS // SSM_GROUPS
D_BC = 256
CHUNK = 128
N_EXPERTS = 16
CAPACITY_FACTOR = 2
D_FF = 2048
EPS = 1e-6
NEG_INF = -1e30

LANES = 128
HALO_ROWS = 8
TOKEN_ROWS = D_MODEL // LANES

COL_Q, COL_Z, COL_X, COL_B, COL_K, COL_V, COL_DT = 0, 1024, 2048, 3072, 3584, 3840, 4096
PROJ_COLS = COL_DT + LANES
PROJ_TN = 1408

VMEM_LIMIT = 56 * 1024 * 1024
FFN_VMEM_LIMIT = 60 * 1024 * 1024


def _cparams(n_axes, vmem=None):
    return pltpu.CompilerParams(dimension_semantics=("arbitrary",) * n_axes,
                                vmem_limit_bytes=vmem)


def _rms(x, w):
    return x * lax.rsqrt(jnp.mean(x * x, axis=-1, keepdims=True) + EPS) * w


def _silu(x):
    return x * jax.nn.sigmoid(x)


def _inproj_kernel(x_ref, nw_ref, w_ref, o_ref, hn_ref):
    @pl.when(pl.program_id(1) == 0)
    def _():
        hn_ref[...] = _rms(x_ref[...], nw_ref[...]).astype(BF16)

    o_ref[...] = jnp.dot(hn_ref[...], w_ref[...], preferred_element_type=F32)


def _inproj(x2, norm_w, w_all, tm):
    T = x2.shape[0]
    return pl.pallas_call(
        _inproj_kernel,
        grid=(T // tm, PROJ_COLS // PROJ_TN),
        in_specs=[pl.BlockSpec((tm, D_MODEL), lambda i, j: (i, 0)),
                  pl.BlockSpec((1, D_MODEL), lambda i, j: (0, 0)),
                  pl.BlockSpec((D_MODEL, PROJ_TN), lambda i, j: (0, j))],
        out_specs=pl.BlockSpec((tm, PROJ_TN), lambda i, j: (i, j)),
        out_shape=jax.ShapeDtypeStruct((T, PROJ_COLS), F32),
        scratch_shapes=[pltpu.VMEM((tm, D_MODEL), BF16)],
        compiler_params=_cparams(2, VMEM_LIMIT),
        name="inproj",
    )(x2, norm_w, w_all)


def _attn_kernel(q_ref, kp_ref, ko_ref, kn_ref, vp_ref, vo_ref, vn_ref, bias_ref, sink_ref, nw_ref,
                 o_ref, *, nb):
    i = pl.program_id(1)
    q = q_ref[...].astype(BF16)
    k = jnp.concatenate([kp_ref[...], ko_ref[...], kn_ref[...]], axis=0).astype(BF16)
    v = jnp.concatenate([vp_ref[...], vo_ref[...], vn_ref[...]], axis=0).astype(BF16)
    qi = lax.broadcasted_iota(I32, (BLOCK, 3 * BLOCK), 0)
    kj = lax.broadcasted_iota(I32, (BLOCK, 3 * BLOCK), 1)
    rel = kj - BLOCK - qi
    kpos = i * BLOCK - BLOCK + kj
    mask = (jnp.abs(rel) <= WINDOW) & (kpos >= 0) & (kpos < nb * BLOCK)
    scale = 1.0 / math.sqrt(HEAD_DIM)
    group = N_HEADS // N_KV_HEADS
    outs = []
    for h in range(N_HEADS):
        g = h // group
        qh = q[:, h * HEAD_DIM:(h + 1) * HEAD_DIM]
        kh = k[:, g * HEAD_DIM:(g + 1) * HEAD_DIM]
        vh = v[:, g * HEAD_DIM:(g + 1) * HEAD_DIM]
        s = lax.dot_general(qh, kh, (((1,), (1,)), ((), ())), preferred_element_type=F32) * scale + bias_ref[h]
        s = jnp.where(mask, s, NEG_INF)
        sk = sink_ref[h]
        m = jnp.maximum(jnp.max(s, axis=-1, keepdims=True), sk)
        p = jnp.exp(s - m)
        den = jnp.sum(p, axis=-1, keepdims=True) + jnp.exp(sk - m)
        probs = p / den
        outs.append(jnp.dot(probs.astype(BF16), vh, preferred_element_type=F32))
    o = jnp.concatenate(outs, axis=-1)
    o_ref[...] = _rms(o, nw_ref[...]).astype(BF16)


def _attention(proj3, bias_tab, sink, norm_w):
    B, S, _ = proj3.shape
    nb = S // BLOCK
    kcol, vcol = COL_K // D_KV, COL_V // D_KV
    prev = lambda i: jnp.maximum(i - 1, 0)
    nxt = lambda i: jnp.minimum(i + 1, nb - 1)
    kv = lambda col, f: pl.BlockSpec((None, BLOCK, D_KV), lambda b, i: (b, f(i), col))
    same = lambda i: i
    return pl.pallas_call(
        functools.partial(_attn_kernel, nb=nb),
        grid=(B, nb),
        in_specs=[pl.BlockSpec((None, BLOCK, D_ATTN), lambda b, i: (b, i, COL_Q // D_ATTN)),
                  kv(kcol, prev), kv(kcol, same), kv(kcol, nxt),
                  kv(vcol, prev), kv(vcol, same), kv(vcol, nxt),
                  pl.BlockSpec((N_HEADS, BLOCK, 3 * BLOCK), lambda b, i: (0, 0, 0)),
                  pl.BlockSpec(memory_space=pltpu.SMEM),
                  pl.BlockSpec((1, D_ATTN), lambda b, i: (0, 0))],
        out_specs=pl.BlockSpec((None, BLOCK, D_ATTN), lambda b, i: (b, i, 0)),
        out_shape=jax.ShapeDtypeStruct((B, S, D_ATTN), BF16),
        compiler_params=_cparams(2),
        name="attention",
    )(proj3, proj3, proj3, proj3, proj3, proj3, proj3, bias_tab, sink, norm_w)


def _conv_silu(u, prev8, next8, w, b, has_prev, has_next):
    prev_row = jnp.where(has_prev, prev8[HALO_ROWS - 1:HALO_ROWS, :], 0.0)
    next_row = jnp.where(has_next, next8[0:1, :], 0.0)
    rid = lax.broadcasted_iota(I32, u.shape, 0)
    um = jnp.where(rid == 0, prev_row, pltpu.roll(u, 1, 0))
    up = jnp.where(rid == CHUNK - 1, next_row, pltpu.roll(u, CHUNK - 1, 0))
    y = b + um * w[0:1, :] + u * w[1:2, :] + up * w[2:3, :]
    return _silu(y)


def _ssd_kernel(*refs, reverse, nc):
    if reverse:
        (z_ref, yf_ref, x_ref, xp_ref, xn_ref, bc_ref, bcp_ref, bcn_ref, dt_ref, cw_ref, cb_ref, dtb_ref,
         alog_ref, dskip_ref, nw_ref, o_ref, h_ref) = refs
    else:
        (x_ref, xp_ref, xn_ref, bc_ref, bcp_ref, bcn_ref, dt_ref, cw_ref, cb_ref, dtb_ref,
         alog_ref, o_ref, h_ref) = refs
    step = pl.program_id(1)
    c = (nc - 1 - step) if reverse else step

    @pl.when(step == 0)
    def _():
        h_ref[...] = jnp.zeros_like(h_ref)

    has_prev, has_next = c > 0, c < nc - 1
    cw, cb = cw_ref[...], cb_ref[...]
    xc = _conv_silu(x_ref[...], xp_ref[...], xn_ref[...], cw[:, :D_SSM], cb[:, :D_SSM], has_prev, has_next)
    bcv = _conv_silu(bc_ref[...], bcp_ref[...], bcn_ref[...], cw[:, D_SSM:], cb[:, D_SSM:], has_prev, has_next)

    raw = dt_ref[...] + dtb_ref[...]
    dt = jnp.maximum(raw, 0.0) + jnp.log1p(jnp.exp(-jnp.abs(raw)))
    a = dt * (-jnp.exp(alog_ref[...]))
    li = lax.broadcasted_iota(I32, (CHUNK, CHUNK), 0)
    si = lax.broadcasted_iota(I32, (CHUNK, CHUNK), 1)
    incl = jnp.dot(jnp.where(li >= si, 1.0, 0.0).astype(F32), a, precision=lax.Precision.HIGHEST,
                   preferred_element_type=F32)
    tot = incl[CHUNK - 1:CHUNK, :]
    if reverse:
        pcs = incl - a
        dstate = jnp.exp(pcs)
        yscale = jnp.exp(tot - pcs)
        mask = si >= li
    else:
        pcs = incl
        dstate = jnp.exp(tot - pcs)
        yscale = jnp.exp(pcs)
        mask = li >= si
    pcs_t = pcs.T
    cdec = jnp.exp(tot)
    lane0 = SSM_HEADS if reverse else 0
    P = SSM_HEAD_DIM
    gw = HEADS_PER_GROUP * P

    y_groups = []
    for g in range(SSM_GROUPS):
        bg = bcv[:, g * SSM_STATE:(g + 1) * SSM_STATE]
        cg = bcv[:, D_BC + g * SSM_STATE:D_BC + (g + 1) * SSM_STATE].astype(BF16)
        cbm = lax.dot_general(cg, bg.astype(BF16), (((1,), (1,)), ((), ())), preferred_element_type=F32)
        hg = h_ref[g]
        yoff = jnp.dot(cg, hg.astype(BF16), preferred_element_type=F32)
        ys, xs, cds = [], [], []
        for jj in range(HEADS_PER_GROUP):
            j = g * HEADS_PER_GROUP + jj
            ln = lane0 + j
            col = pcs[:, ln:ln + 1]
            row = pcs_t[ln:ln + 1, :]
            seg = (row - col) if reverse else (col - row)
            lm = jnp.exp(jnp.where(mask, seg, -jnp.inf))
            wmat = (cbm * lm).astype(BF16)
            xdt = xc[:, j * P:(j + 1) * P] * dt[:, ln:ln + 1]
            yd = jnp.dot(wmat, xdt.astype(BF16), preferred_element_type=F32)
            ys.append(yd + yoff[:, jj * P:(jj + 1) * P] * yscale[:, ln:ln + 1])
            xs.append((xdt * dstate[:, ln:ln + 1]).astype(BF16))
            cds.append(jnp.broadcast_to(cdec[:, ln:ln + 1], (1, P)))
        xs = jnp.concatenate(xs, axis=1)
        snew = jnp.dot(bg.T.astype(BF16), xs, preferred_element_type=F32)
        h_ref[g] = hg * jnp.concatenate(cds, axis=1) + snew
        y_groups.append(jnp.concatenate(ys, axis=1))
    y = jnp.concatenate(y_groups, axis=1)

    if not reverse:
        o_ref[...] = y
    else:
        ytot = yf_ref[...] + y + dskip_ref[...] * xc
        yz = ytot * _silu(z_ref[...])
        halves = []
        for g in range(SSM_GROUPS):
            seg = yz[:, g * gw:(g + 1) * gw]
            halves.append(seg * lax.rsqrt(jnp.mean(seg * seg, axis=-1, keepdims=True) + EPS))
        o_ref[...] = (jnp.concatenate(halves, axis=1) * nw_ref[...]).astype(BF16)


def _ssd_pass(proj3, conv_w3, conv_b, dtb, alog, reverse, yf=None, dskip=None, norm_w=None):
    B, S, _ = proj3.shape
    nc = S // CHUNK
    hp = CHUNK // HALO_ROWS
    nh = S // HALO_ROWS
    ch = (lambda s: nc - 1 - s) if reverse else (lambda s: s)
    main = lambda w, col: pl.BlockSpec((None, CHUNK, w), lambda b, s: (b, ch(s), col // w))
    halo_p = lambda w, col: pl.BlockSpec((None, HALO_ROWS, w),
                                         lambda b, s: (b, jnp.maximum(ch(s) * hp - 1, 0), col // w))
    halo_n = lambda w, col: pl.BlockSpec((None, HALO_ROWS, w),
                                         lambda b, s: (b, jnp.minimum((ch(s) + 1) * hp, nh - 1), col // w))
    full = lambda a: pl.BlockSpec(a.shape, lambda b, s: (0,) * a.ndim)
    d_bc2 = 2 * D_BC
    in_specs = [main(D_SSM, COL_X), halo_p(D_SSM, COL_X), halo_n(D_SSM, COL_X),
                main(d_bc2, COL_B), halo_p(d_bc2, COL_B), halo_n(d_bc2, COL_B),
                main(LANES, COL_DT), full(conv_w3), full(conv_b), full(dtb), full(alog)]
    args = [proj3, proj3, proj3, proj3, proj3, proj3, proj3, conv_w3, conv_b, dtb, alog]
    if reverse:
        in_specs = [main(D_SSM, COL_Z), pl.BlockSpec((None, CHUNK, D_SSM), lambda b, s: (b, ch(s), 0))] + in_specs
        in_specs += [full(dskip), full(norm_w)]
        args = [proj3, yf] + args + [dskip, norm_w]
    return pl.pallas_call(
        functools.partial(_ssd_kernel, reverse=reverse, nc=nc),
        grid=(B, nc),
        in_specs=in_specs,
        out_specs=pl.BlockSpec((None, CHUNK, D_SSM), lambda b, s: (b, ch(s), 0)),
        out_shape=jax.ShapeDtypeStruct((B, S, D_SSM), BF16 if reverse else F32),
        scratch_shapes=[pltpu.VMEM((SSM_GROUPS, SSM_STATE, HEADS_PER_GROUP * SSM_HEAD_DIM), F32)],
        compiler_params=_cparams(2),
        name="ssd_bwd" if reverse else "ssd_fwd",
    )(*args)


def _outproj_kernel(x_ref, a_ref, s_ref, w_ref, n2_ref, rw_ref, x1_ref, h_ref, aff_ref):
    tm = x_ref.shape[0]
    x1 = (x_ref[...]
          + jnp.dot(a_ref[...], w_ref[:D_ATTN, :], preferred_element_type=F32)
          + jnp.dot(s_ref[...], w_ref[D_ATTN:, :], preferred_element_type=F32))
    x1_ref[...] = x1
    hn = _rms(x1, n2_ref[...])
    for s in range(TOKEN_ROWS):
        h_ref[pl.ds(s, tm, stride=TOKEN_ROWS), :] = hn[:, s * LANES:(s + 1) * LANES]
    logits = jnp.dot(hn, rw_ref[...], precision=lax.Precision.HIGHEST, preferred_element_type=F32)
    e = jnp.exp(logits - jnp.max(logits, axis=-1, keepdims=True))
    aff_ref[...] = e / jnp.sum(e, axis=-1, keepdims=True)


def _outproj(x2, attn2, ssm2, w_out, norm2_w, router_w, tm):
    T = x2.shape[0]
    row = lambda w: pl.BlockSpec((tm, w), lambda i: (i, 0))
    full = lambda a: pl.BlockSpec(a.shape, lambda i: (0,) * a.ndim)
    return pl.pallas_call(
        _outproj_kernel,
        grid=(T // tm,),
        in_specs=[row(D_MODEL), row(D_ATTN), row(D_SSM), full(w_out), full(norm2_w), full(router_w)],
        out_specs=[row(D_MODEL), pl.BlockSpec((tm * TOKEN_ROWS, LANES), lambda i: (i, 0)), row(N_EXPERTS)],
        out_shape=[jax.ShapeDtypeStruct((T, D_MODEL), F32),
                   jax.ShapeDtypeStruct((T * TOKEN_ROWS, LANES), F32),
                   jax.ShapeDtypeStruct((T, N_EXPERTS), F32)],
        compiler_params=_cparams(1, VMEM_LIMIT),
        name="outproj_router",
    )(x2, attn2, ssm2, w_out, norm2_w, router_w)


def _count(m):
    c = jnp.sum(jnp.where(m, 1.0, 0.0), axis=0, keepdims=True)
    return jnp.sum(c, axis=1, keepdims=True)


def _tri(n, m, fn):
    return jnp.where(fn(lax.broadcasted_iota(I32, (n, m), 0), lax.broadcasted_iota(I32, (n, m), 1)),
                     1.0, 0.0).astype(BF16)


def _dot_u16(lhs01, x):
    hi = jnp.floor(x * (1.0 / 256.0))
    lo = x - hi * 256.0
    return (jnp.dot(lhs01, hi.astype(BF16), preferred_element_type=F32) * 256.0
            + jnp.dot(lhs01, lo.astype(BF16), preferred_element_type=F32))


def _cumsum_rowmajor(x):
    R = x.shape[0]
    within = jnp.dot(x.astype(BF16), _tri(LANES, LANES, lambda k, l: k <= l), preferred_element_type=F32)
    rowtot = jnp.broadcast_to(within[:, LANES - 1:LANES], (R, LANES))
    before = _dot_u16(_tri(R, R, lambda i, k: k < i), rowtot)
    return within + before


def _route_kernel(a_ref, idx_ref, dst_ref, g_ref, cs_ref, ce_ref, sel_s, cnt_s, cs_s, rank_s, *, cap):
    s = pl.program_id(0)
    R = a_ref.shape[0]

    @pl.when(s < N_EXPERTS)
    def _select():
        bits = pltpu.bitcast(a_ref[...], I32)
        capf = jnp.float32(cap)

        def body(k, tau):
            cand = tau | lax.shift_left(jnp.int32(1), 30 - k)
            return jnp.where(_count(bits >= cand) >= capf, cand, tau)

        tau = lax.fori_loop(0, 31, body, jnp.zeros((1, 1), I32))
        gt = bits > tau
        eq = bits == tau
        need = capf - _count(gt)
        ties = _cumsum_rowmajor(jnp.where(eq, 1.0, 0.0))
        sel = jnp.where(gt | (eq & (ties <= need)), 1.0, 0.0)
        sel_s[s] = sel

        @pl.when(s == 0)
        def _():
            cnt_s[...] = sel

        @pl.when(s > 0)
        def _():
            cnt_s[...] += sel

    @pl.when(s == N_EXPERTS)
    def _prefix():
        cnt = cnt_s[...]
        ce = _cumsum_rowmajor(cnt)
        cs_s[...] = ce - cnt
        rank_s[...] = jnp.zeros_like(rank_s)
        cs_ref[...] = ce - cnt
        ce_ref[...] = ce

    @pl.when(s >= N_EXPERTS)
    def _invert():
        e = s - N_EXPERTS
        sel = sel_s[e]
        rank = rank_s[...]
        q = cs_s[...] + rank
        rank_s[...] = rank + sel
        a = a_ref[...]
        within = jnp.dot(sel.astype(BF16), _tri(LANES, LANES, lambda k, l: k <= l), preferred_element_type=F32)
        n_b = jnp.broadcast_to(within[:, LANES - 1:LANES], (R, LANES))
        rowend = jnp.dot(_tri(R, R, lambda i, k: k <= i), n_b.astype(BF16), preferred_element_type=F32)
        slot = lax.broadcasted_iota(I32, (R, cap), 1).astype(F32)
        done = jnp.where(rowend[:, 0:1] <= slot, 1.0, 0.0)
        ones = jnp.ones((8, R), BF16)
        row_p = jnp.dot(ones, done.astype(BF16), preferred_element_type=F32)[0:1]
        start_p = jnp.dot(ones, (done * n_b[:, 0:1]).astype(BF16), preferred_element_type=F32)[0:1]
        onehot = jnp.where(lax.broadcasted_iota(I32, (R, cap), 0).astype(F32) == row_p, 1.0, 0.0).astype(BF16)
        w_t = jnp.dot(within.T.astype(BF16), onehot, preferred_element_type=F32)
        k_in_row = lax.broadcasted_iota(I32, (1, cap), 1).astype(F32) - start_p
        lane_p = jnp.sum(jnp.where(w_t <= k_in_row, 1.0, 0.0), axis=0, keepdims=True)
        pick = lax.broadcasted_iota(I32, (LANES, cap), 0).astype(F32) == lane_p

        def take(x):
            v = jnp.dot(x.T.astype(BF16), onehot, preferred_element_type=F32)
            return jnp.sum(jnp.where(pick, v, 0.0), axis=0, keepdims=True)

        q_hi = jnp.floor(q * (1.0 / 256.0))
        a_hi = a.astype(BF16).astype(F32)
        a_mid = (a - a_hi).astype(BF16).astype(F32)
        a_lo = a - a_hi - a_mid
        idx_ref[...] = (row_p * LANES + lane_p).astype(I32)
        dst_ref[...] = (take(q_hi) * 256.0 + take(q - q_hi * 256.0)).astype(I32)
        g_ref[...] = take(a_hi) + take(a_mid) + take(a_lo)


def _route(aff_t3, cap):
    E, R, _ = aff_t3.shape
    assert R <= 256 and E == N_EXPERTS
    slot_spec = pl.BlockSpec((None, 1, cap), lambda s: (jnp.maximum(s - N_EXPERTS, 0), 0, 0))
    tok_spec = pl.BlockSpec((R, LANES), lambda s: (0, 0))
    return pl.pallas_call(
        functools.partial(_route_kernel, cap=cap),
        grid=(2 * E,),
        in_specs=[pl.BlockSpec((None, R, LANES), lambda s: (s % N_EXPERTS, 0, 0))],
        out_specs=[slot_spec, slot_spec, slot_spec, tok_spec, tok_spec],
        out_shape=[jax.ShapeDtypeStruct((E, 1, cap), I32), jax.ShapeDtypeStruct((E, 1, cap), I32),
                   jax.ShapeDtypeStruct((E, 1, cap), F32),
                   jax.ShapeDtypeStruct((R, LANES), F32), jax.ShapeDtypeStruct((R, LANES), F32)],
        scratch_shapes=[pltpu.VMEM((E, R, LANES), F32), pltpu.VMEM((R, LANES), F32),
                        pltpu.VMEM((R, LANES), F32), pltpu.VMEM((R, LANES), F32)],
        compiler_params=_cparams(1, VMEM_LIMIT),
        name="route",
    )(aff_t3)


def _ffn_kernel(idx_ref, dst_ref, h_hbm, gt_ref, wg_ref, wu_ref, wd_ref, c_hbm,
                xs_stage, xsb, acc_ref, o_stage, gsem, ssem, *, tm, cap):
    e, m, f = pl.program_id(0), pl.program_id(1), pl.program_id(2)
    nf = pl.num_programs(2)
    base = e * cap + m * tm
    tr = TOKEN_ROWS

    def token_rows(t):
        return pl.ds(pl.multiple_of(t * tr, tr), tr)

    @pl.when(f == 0)
    def _gather():
        def issue(r, carry):
            pltpu.make_async_copy(h_hbm.at[token_rows(idx_ref[base + r]), :], xs_stage.at[token_rows(r), :],
                                  gsem).start()
            return carry

        lax.fori_loop(0, tm, issue, 0, unroll=8)
        pltpu.make_async_copy(h_hbm.at[pl.ds(0, tm * tr), :], xs_stage, gsem).wait()
        for s in range(tr):
            xsb[:, s * LANES:(s + 1) * LANES] = xs_stage[pl.ds(s, tm, stride=tr), :].astype(BF16)

    xs = xsb[...]
    g = jnp.dot(xs, wg_ref[...], preferred_element_type=F32)
    u = jnp.dot(xs, wu_ref[...], preferred_element_type=F32)
    d = jnp.dot((_silu(g) * u).astype(BF16), wd_ref[...], preferred_element_type=F32)

    @pl.when(f == 0)
    def _():
        acc_ref[...] = d

    @pl.when(f > 0)
    def _():
        acc_ref[...] += d

    def scatter_done():
        pltpu.make_async_copy(o_stage, c_hbm.at[pl.ds(0, tm * tr), :], ssem).wait()

    @pl.when(f == nf - 1)
    def _scatter():
        @pl.when((e > 0) | (m > 0))
        def _():
            scatter_done()

        for j in range(tm // LANES):
            rows = acc_ref[j * LANES:(j + 1) * LANES, :] * gt_ref[:, j:j + 1]
            for s in range(tr):
                o_stage[pl.ds(j * LANES * tr + s, LANES, stride=tr), :] = rows[:, s * LANES:(s + 1) * LANES]

        def issue(r, carry):
            pltpu.make_async_copy(o_stage.at[token_rows(r), :], c_hbm.at[token_rows(dst_ref[base + r]), :],
                                  ssem).start()
            return carry

        lax.fori_loop(0, tm, issue, 0, unroll=8)

        @pl.when((e == pl.num_programs(0) - 1) & (m == pl.num_programs(1) - 1))
        def _():
            scatter_done()


def _ffn(idx, dst, h_rows, g_t, w_gate, w_up, w_down, tm, tf):
    E, n_m = g_t.shape[0], g_t.shape[1]
    cap = n_m * tm
    n_contrib = E * cap
    grid_spec = pltpu.PrefetchScalarGridSpec(
        num_scalar_prefetch=2,
        grid=(E, n_m, D_FF // tf),
        in_specs=[pl.BlockSpec(memory_space=pl.ANY),
                  pl.BlockSpec((None, None, LANES, tm // LANES), lambda e, m, f, i, d: (e, m, 0, 0)),
                  pl.BlockSpec((None, D_MODEL, tf), lambda e, m, f, i, d: (e, 0, f)),
                  pl.BlockSpec((None, D_MODEL, tf), lambda e, m, f, i, d: (e, 0, f)),
                  pl.BlockSpec((None, tf, D_MODEL), lambda e, m, f, i, d: (e, f, 0))],
        out_specs=pl.BlockSpec(memory_space=pl.ANY),
        scratch_shapes=[pltpu.VMEM((tm * TOKEN_ROWS, LANES), F32), pltpu.VMEM((tm, D_MODEL), BF16),
                        pltpu.VMEM((tm, D_MODEL), F32), pltpu.VMEM((tm * TOKEN_ROWS, LANES), F32),
                        pltpu.SemaphoreType.DMA, pltpu.SemaphoreType.DMA],
    )
    return pl.pallas_call(
        functools.partial(_ffn_kernel, tm=tm, cap=cap),
        grid_spec=grid_spec,
        out_shape=jax.ShapeDtypeStruct((n_contrib * TOKEN_ROWS, LANES), F32),
        compiler_params=_cparams(3, FFN_VMEM_LIMIT),
        name="expert_ffn",
    )(idx, dst, h_rows, g_t, w_gate, w_up, w_down)


def _combine_kernel(ch_ref, tt_ref, flag_ref, x1_ref, cs_ref, ce_ref, c_ref, nw_ref, out_ref, acc_ref, *, cg, rows):
    k = pl.program_id(0)
    flag = flag_ref[k]

    @pl.when((flag & 2) > 0)
    def _():
        acc_ref[...] = x1_ref[...]

    @pl.when((flag & 1) > 0)
    def _():
        cio = (lax.broadcasted_iota(I32, (cg, LANES), 0) + ch_ref[k] * cg).astype(F32)
        parts = [jnp.where((cs_ref[r:r + 1, :] <= cio) & (cio < ce_ref[r:r + 1, :]), 1.0, 0.0) for r in range(rows)]
        a = jnp.concatenate(parts, axis=1).T.astype(BF16)
        chunk = jnp.concatenate([c_ref[pl.ds(s, cg, stride=TOKEN_ROWS), :].astype(BF16)
                                 for s in range(TOKEN_ROWS)], axis=1)
        acc_ref[...] += jnp.dot(a, chunk, preferred_element_type=F32)

    @pl.when((flag & 4) > 0)
    def _():
        out_ref[...] = _rms(acc_ref[...], nw_ref[...])


def _combine(ch, tt, flag, x1, cs3, ce3, contrib, final_w, tb, cg):
    T = x1.shape[0]
    rows = tb // LANES
    n_pairs = ch.shape[0]
    grid_spec = pltpu.PrefetchScalarGridSpec(
        num_scalar_prefetch=3,
        grid=(n_pairs,),
        in_specs=[pl.BlockSpec((tb, D_MODEL), lambda k, ch, tt, fl: (tt[k], 0)),
                  pl.BlockSpec((None, rows, LANES), lambda k, ch, tt, fl: (tt[k], 0, 0)),
                  pl.BlockSpec((None, rows, LANES), lambda k, ch, tt, fl: (tt[k], 0, 0)),
                  pl.BlockSpec((cg * TOKEN_ROWS, LANES), lambda k, ch, tt, fl: (ch[k], 0)),
                  pl.BlockSpec((1, D_MODEL), lambda k, ch, tt, fl: (0, 0))],
        out_specs=pl.BlockSpec((tb, D_MODEL), lambda k, ch, tt, fl: (tt[k], 0)),
        scratch_shapes=[pltpu.VMEM((tb, D_MODEL), F32)],
    )
    return pl.pallas_call(
        functools.partial(_combine_kernel, cg=cg, rows=rows),
        grid_spec=grid_spec,
        out_shape=jax.ShapeDtypeStruct((T, D_MODEL), F32),
        compiler_params=_cparams(1, VMEM_LIMIT),
        name="combine_final",
    )(ch, tt, flag, x1, cs3, ce3, contrib, final_w)


def _combine_schedule(cs, ce, tb, cg):
    T = cs.size
    ntt = T // tb
    total = CAPACITY_FACTOR * T
    nch = total // cg
    lo = cs.reshape(ntt, tb)[:, 0].astype(I32)
    hi = ce.reshape(ntt, tb)[:, -1].astype(I32)
    first = jnp.minimum(lo, total - 1) // cg
    last = jnp.where(hi > lo, (hi - 1) // cg, first)
    n = last - first + 1
    ends = jnp.cumsum(n)
    offs = ends - n
    n_pairs = ntt + nch
    ks = jnp.arange(n_pairs, dtype=I32)
    tile = jnp.minimum(jnp.searchsorted(ends, ks, side="right").astype(I32), ntt - 1)
    valid = ks < ends[-1]
    chunk = jnp.where(valid, first[tile] + ks - offs[tile], last[-1])
    flag = jnp.where(valid, 1 + 2 * (ks == offs[tile]) + 4 * (ks == ends[tile] - 1), 0).astype(I32)
    return chunk.astype(I32), tile, flag


def _t5_buckets(rel):
    nb = NUM_BUCKETS // 2
    ret = (rel > 0).astype(np.int32) * nb
    n = np.abs(rel)
    max_exact = nb // 2
    large = max_exact + (np.log(np.maximum(n, 1) / max_exact) / np.log(MAX_DISTANCE / max_exact)
                         * (nb - max_exact)).astype(np.int32)
    large = np.minimum(large, nb - 1)
    return ret + np.where(n < max_exact, n, large)


def _tile(n, pref):
    t = min(n, pref)
    assert n % t == 0
    return t


def _trunk(x, w):
    B, S, _ = x.shape
    T = B * S
    cap = CAPACITY_FACTOR * T // N_EXPERTS
    x2 = x.reshape(T, D_MODEL)
    tm = _tile(T, 512)

    proj = _inproj(x2, w["norm1_w"], w["w_in"], tm)
    proj3 = proj.reshape(B, S, PROJ_COLS)
    attn = _attention(proj3, w["bias_tab"], w["attn_sink"], w["attn_norm_w"])
    yf = _ssd_pass(proj3, w["conv_w"], w["conv_b"], w["dtb"], w["alog"], reverse=False)
    ssm = _ssd_pass(proj3, w["conv_w"], w["conv_b"], w["dtb"], w["alog"], reverse=True,
                    yf=yf, dskip=w["dskip"], norm_w=w["ssm_norm_w"])
    x1, h_rows, aff = _outproj(x2, attn.reshape(T, D_ATTN), ssm.reshape(T, D_SSM), w["w_out"], w["norm2_w"],
                          w["router_w"], tm)

    tb = _tile(T, 256)
    cg = 256
    idx, dst, g, cs, ce = _route(aff.T.reshape(N_EXPERTS, T // LANES, LANES), cap)
    tmf = _tile(cap, 1024)
    g_t = jnp.swapaxes(g.reshape(N_EXPERTS, cap // tmf, tmf // LANES, LANES), 2, 3)
    contrib = _ffn(idx.reshape(-1), dst.reshape(-1), h_rows, g_t, w["w_gate"], w["w_up"], w["w_down"], tmf, 512)
    ch, tt, flag = _combine_schedule(cs, ce, tb, cg)
    tile3 = lambda a: a.reshape(T // tb, tb // LANES, LANES)
    y = _combine(ch, tt, flag, x1, tile3(cs), tile3(ce), contrib, w["final_norm_w"], tb, cg)
    return y.reshape(B, S, D_MODEL)


def _prep_weights(rel_bias, norm1_w, w_in, conv_w, conv_b, dt_bias_fwd, dt_bias_bwd, a_log_fwd, a_log_bwd,
                  d_skip, ssm_norm_w, attn_sink, attn_norm_w, w_out, norm2_w, router_w, w_gate, w_up, w_down,
                  final_norm_w):
    o1 = D_ATTN; o2 = o1 + D_KV; o3 = o2 + D_KV; o4 = o3 + D_SSM; o5 = o4 + D_SSM + 2 * D_BC
    wi = w_in[0]
    w_all = jnp.concatenate([wi[:, :o1], wi[:, o3:o4], wi[:, o4:o5], wi[:, o1:o2], wi[:, o2:o3], wi[:, o5:],
                             jnp.zeros((D_MODEL, LANES - 2 * SSM_HEADS), F32)], axis=1).astype(BF16)
    rel = np.arange(3 * BLOCK)[None, :] - BLOCK - np.arange(BLOCK)[:, None]
    bias_tab = jnp.transpose(rel_bias[_t5_buckets(rel)], (2, 0, 1)).astype(F32)
    pad = jnp.zeros((LANES - 2 * SSM_HEADS,), F32)
    row = lambda a: a.reshape(1, -1).astype(F32)
    return dict(
        norm1_w=row(norm1_w[0]), w_in=w_all, bias_tab=bias_tab, attn_sink=attn_sink[0].astype(F32),
        attn_norm_w=row(attn_norm_w[0]), conv_w=conv_w[0].T.astype(F32), conv_b=row(conv_b[0]),
        dtb=row(jnp.concatenate([dt_bias_fwd[0], dt_bias_bwd[0], pad])),
        alog=row(jnp.concatenate([a_log_fwd[0], a_log_bwd[0], pad])),
        dskip=row(jnp.repeat(d_skip[0], SSM_HEAD_DIM)), ssm_norm_w=row(ssm_norm_w[0]),
        w_out=w_out[0].astype(BF16), norm2_w=row(norm2_w[0]), router_w=router_w[0].astype(F32),
        w_gate=w_gate[0].astype(BF16), w_up=w_up[0].astype(BF16), w_down=w_down[0].astype(BF16),
        final_norm_w=row(final_norm_w))


def kernel(x_prompt, x_sample, rel_bias, norm1_w, w_in, conv_w, conv_b, dt_bias_fwd, dt_bias_bwd, a_log_fwd, a_log_bwd, d_skip, ssm_norm_w, attn_sink, attn_norm_w, w_out, norm2_w, router_w, w_gate, w_up, w_down, final_norm_w):
    assert norm1_w.shape[0] == 1
    w = _prep_weights(rel_bias, norm1_w, w_in, conv_w, conv_b, dt_bias_fwd, dt_bias_bwd, a_log_fwd, a_log_bwd,
                      d_skip, ssm_norm_w, attn_sink, attn_norm_w, w_out, norm2_w, router_w, w_gate, w_up,
                      w_down, final_norm_w)
    return (_trunk(x_prompt, w), _trunk(x_sample, w))
```

```python
import functools
import math

import numpy as np
import jax
import jax.numpy as jnp
from jax import lax
from jax.experimental import pallas as pl
from jax.experimental.pallas import tpu as pltpu

F32 = jnp.float32
BF16 = jnp.bfloat16
I32 = jnp.int32

D_MODEL = 2048
HEAD_DIM = 64
N_HEADS = 16
N_KV_HEADS = 4
D_ATTN = 1024
D_KV = 256
WINDOW = 128
BLOCK = 128
NUM_BUCKETS = 32
MAX_DISTANCE = 128
SSM_HEAD_DIM = 64
SSM_HEADS = 16
D_SSM = 1024
SSM_STATE = 128
SSM_GROUPS = 2
HEADS_PER_GROUP = SSM_HEADS // SSM_GROUPS
D_BC = 256
CHUNK = 128
N_EXPERTS = 16
CAPACITY_FACTOR = 2
D_FF = 2048
EPS = 1e-6
NEG_INF = -1e30

LANES = 128
HALO_ROWS = 8
TOKEN_ROWS = D_MODEL // LANES

QKV_COLS = D_ATTN + 2 * D_KV
COL_QK, COL_QV = D_ATTN, D_ATTN + D_KV
COL_Z, COL_X, COL_B, COL_DT = 0, 1024, 2048, 2560
MAIN_COLS = COL_DT + LANES
MAIN_TN = 896

VMEM_LIMIT = 56 * 1024 * 1024
FFN_VMEM_LIMIT = 60 * 1024 * 1024
PREFETCH_STEPS = 2


def _cparams(n_axes, vmem=None):
    return pltpu.CompilerParams(dimension_semantics=("arbitrary",) * n_axes,
                                vmem_limit_bytes=vmem)


def _rms(x, w):
    return x * lax.rsqrt(jnp.mean(x * x, axis=-1, keepdims=True) + EPS) * w


def _silu(x):
    return x * jax.nn.sigmoid(x)


def _inproj_kernel(x_ref, nw_ref, wq_ref, wr_ref, oq_ref, om_ref, hn_ref):
    j = pl.program_id(1)

    @pl.when(j == 0)
    def _():
        hn = _rms(x_ref[...], nw_ref[...]).astype(BF16)
        hn_ref[...] = hn
        oq_ref[...] = jnp.dot(hn, wq_ref[...], preferred_element_type=F32).astype(BF16)

    @pl.when(j > 0)
    def _():
        om_ref[...] = jnp.dot(hn_ref[...], wr_ref[...], preferred_element_type=F32)


def _inproj(x2, norm_w, w_qkv, w_main, tm):
    T = x2.shape[0]
    rest = lambda j: jnp.maximum(j - 1, 0)
    return pl.pallas_call(
        _inproj_kernel,
        grid=(T // tm, 1 + MAIN_COLS // MAIN_TN),
        in_specs=[pl.BlockSpec((tm, D_MODEL), lambda i, j: (i, 0)),
                  pl.BlockSpec((1, D_MODEL), lambda i, j: (0, 0)),
                  pl.BlockSpec((D_MODEL, QKV_COLS), lambda i, j: (0, 0)),
                  pl.BlockSpec((D_MODEL, MAIN_TN), lambda i, j: (0, rest(j)))],
        out_specs=[pl.BlockSpec((tm, QKV_COLS), lambda i, j: (i, 0)),
                   pl.BlockSpec((tm, MAIN_TN), lambda i, j: (i, rest(j)))],
        out_shape=[jax.ShapeDtypeStruct((T, QKV_COLS), BF16), jax.ShapeDtypeStruct((T, MAIN_COLS), F32)],
        scratch_shapes=[pltpu.VMEM((tm, D_MODEL), BF16)],
        compiler_params=_cparams(2, VMEM_LIMIT),
        name="inproj",
    )(x2, norm_w, w_qkv, w_main)


def _attn_kernel(q_ref, kp_ref, ko_ref, kn_ref, vp_ref, vo_ref, vn_ref, bias_ref, sink_ref, nw_ref,
                 o_ref, *, nb):
    i = pl.program_id(1)
    kf = jnp.concatenate([kp_ref[...], ko_ref[...], kn_ref[...]], axis=0).astype(F32)
    vf = jnp.concatenate([vp_ref[...], vo_ref[...], vn_ref[...]], axis=0).astype(F32)
    low = lax.broadcasted_iota(I32, (3 * BLOCK, LANES), 1) < HEAD_DIM
    low_q = lax.broadcasted_iota(I32, (BLOCK, LANES), 1) < HEAD_DIM
    kcol = lax.broadcasted_iota(I32, (1, 3 * BLOCK), 1)
    edge = (jnp.where((kcol < BLOCK) & (i == 0), NEG_INF, 0.0)
            + jnp.where((kcol >= 2 * BLOCK) & (i == nb - 1), NEG_INF, 0.0))
    top = lax.broadcasted_iota(I32, (2 * BLOCK, 1), 0) < BLOCK
    group = N_HEADS // N_KV_HEADS
    tiles = []
    for g in range(N_KV_HEADS):
        t, upper = (g * HEAD_DIM) // LANES, (g * HEAD_DIM) % LANES != 0
        kt = kf[:, t * LANES:(t + 1) * LANES]
        vt = vf[:, t * LANES:(t + 1) * LANES]
        kr = pltpu.roll(kt, HEAD_DIM, 1)
        vr = pltpu.roll(vt, HEAD_DIM, 1)
        k_lo, k_up = (kr, kt) if upper else (kt, kr)
        v_lo, v_up = (vr, vt) if upper else (vt, vr)
        rhs = [(jnp.where(low, k_lo, 0.0).astype(BF16), jnp.where(low, v_lo, 1.0).astype(BF16)),
               (jnp.where(low, 0.0, k_up).astype(BF16), jnp.where(low, 1.0, v_up).astype(BF16))]
        h0 = g * group
        q2 = jnp.concatenate([q_ref[:, (h0 // 2) * LANES:(h0 // 2 + 1) * LANES],
                              q_ref[:, (h0 // 2 + 1) * LANES:(h0 // 2 + 2) * LANES]], axis=0)
        res = []
        for par in range(2):
            kz, vz = rhs[par]
            ha, hb = h0 + par, h0 + 2 + par
            s = lax.dot_general(q2, kz, (((1,), (1,)), ((), ())), preferred_element_type=F32)
            s = s + jnp.concatenate([bias_ref[ha], bias_ref[hb]], axis=0) + edge
            sk = jnp.where(top, sink_ref[ha], sink_ref[hb])
            m = jnp.maximum(jnp.max(s, axis=-1, keepdims=True), sk)
            p = jnp.exp(s - m).astype(BF16)
            pv = jnp.dot(p, vz, preferred_element_type=F32)
            den = (pv[:, LANES - 1:LANES] if par == 0 else pv[:, 0:1]) + jnp.exp(sk - m)
            res.append(pv * (1.0 / den))
        tiles.append(jnp.where(low_q, res[0][:BLOCK], res[1][:BLOCK]))
        tiles.append(jnp.where(low_q, res[0][BLOCK:], res[1][BLOCK:]))
    o = jnp.concatenate(tiles, axis=-1)
    o_ref[...] = _rms(o, nw_ref[...]).astype(BF16)


def _attention(qkv3, bias_tab, sink, norm_w):
    B, S, _ = qkv3.shape
    nb = S // BLOCK
    kcol, vcol = COL_QK // D_KV, COL_QV // D_KV
    prev = lambda i: jnp.maximum(i - 1, 0)
    nxt = lambda i: jnp.minimum(i + 1, nb - 1)
    kv = lambda col, f: pl.BlockSpec((None, BLOCK, D_KV), lambda b, i: (b, f(i), col))
    same = lambda i: i
    return pl.pallas_call(
        functools.partial(_attn_kernel, nb=nb),
        grid=(B, nb),
        in_specs=[pl.BlockSpec((None, BLOCK, D_ATTN), lambda b, i: (b, i, 0)),
                  kv(kcol, prev), kv(kcol, same), kv(kcol, nxt),
                  kv(vcol, prev), kv(vcol, same), kv(vcol, nxt),
                  pl.BlockSpec((N_HEADS, BLOCK, 3 * BLOCK), lambda b, i: (0, 0, 0)),
                  pl.BlockSpec(memory_space=pltpu.SMEM),
                  pl.BlockSpec((1, D_ATTN), lambda b, i: (0, 0))],
        out_specs=pl.BlockSpec((None, BLOCK, D_ATTN), lambda b, i: (b, i, 0)),
        out_shape=jax.ShapeDtypeStruct((B, S, D_ATTN), BF16),
        compiler_params=_cparams(2),
        name="attention",
    )(qkv3, qkv3, qkv3, qkv3, qkv3, qkv3, qkv3, bias_tab, sink, norm_w)


def _conv_silu(u, prev8, next8, w, b, has_prev, has_next):
    prev_row = jnp.where(has_prev, prev8[HALO_ROWS - 1:HALO_ROWS, :], 0.0)
    next_row = jnp.where(has_next, next8[0:1, :], 0.0)
    rid = lax.broadcasted_iota(I32, u.shape, 0)
    um = jnp.where(rid == 0, prev_row, pltpu.roll(u, 1, 0))
    up = jnp.where(rid == CHUNK - 1, next_row, pltpu.roll(u, CHUNK - 1, 0))
    y = b + um * w[0:1, :] + u * w[1:2, :] + up * w[2:3, :]
    return _silu(y)


def _ssd_kernel(*refs, reverse, nc):
    if reverse:
        (z_ref, yf_ref, x_ref, xp_ref, xn_ref, bc_ref, bcp_ref, bcn_ref, dt_ref, cw_ref, cb_ref, dtb_ref,
         alog_ref, dskip_ref, nw_ref, o_ref, h_ref) = refs
    else:
        (x_ref, xp_ref, xn_ref, bc_ref, bcp_ref, bcn_ref, dt_ref, cw_ref, cb_ref, dtb_ref,
         alog_ref, o_ref, h_ref) = refs
    step = pl.program_id(1)
    c = (nc - 1 - step) if reverse else step

    @pl.when(step == 0)
    def _():
        h_ref[...] = jnp.zeros_like(h_ref)

    has_prev, has_next = c > 0, c < nc - 1
    cw, cb = cw_ref[...], cb_ref[...]
    xc = _conv_silu(x_ref[...], xp_ref[...], xn_ref[...], cw[:, :D_SSM], cb[:, :D_SSM], has_prev, has_next)
    bcv = _conv_silu(bc_ref[...], bcp_ref[...], bcn_ref[...], cw[:, D_SSM:], cb[:, D_SSM:], has_prev, has_next)

    raw = dt_ref[...] + dtb_ref[...]
    dt = jnp.maximum(raw, 0.0) + jnp.log1p(jnp.exp(-jnp.abs(raw)))
    a = dt * (-jnp.exp(alog_ref[...]))
    li = lax.broadcasted_iota(I32, (CHUNK, CHUNK), 0)
    si = lax.broadcasted_iota(I32, (CHUNK, CHUNK), 1)
    incl = jnp.dot(jnp.where(li >= si, 1.0, 0.0).astype(F32), a, precision=lax.Precision.HIGHEST,
                   preferred_element_type=F32)
    tot = incl[CHUNK - 1:CHUNK, :]
    if reverse:
        pcs = incl - a
        dstate = jnp.exp(pcs)
        yscale = jnp.exp(tot - pcs)
        mask = si >= li
    else:
        pcs = incl
        dstate = jnp.exp(tot - pcs)
        yscale = jnp.exp(pcs)
        mask = li >= si
    pcs_t = pcs.T
    lane0 = SSM_HEADS if reverse else 0
    P = SSM_HEAD_DIM
    gw = HEADS_PER_GROUP * P

    def spread(x, width, pieces):
        k = lax.broadcasted_iota(I32, (LANES, SSM_HEADS * width), 0)
        c = lax.broadcasted_iota(I32, (LANES, SSM_HEADS * width), 1)
        sel = jnp.where(k == lane0 + c // width, 1.0, 0.0).astype(BF16)
        out, rem = None, x
        for _ in range(pieces):
            piece = rem.astype(BF16)
            rem = rem - piece.astype(F32)
            d = jnp.dot(piece, sel, preferred_element_type=F32)
            out = d if out is None else out + d
        return out

    scales = spread(jnp.concatenate([dt, dstate, yscale, jnp.broadcast_to(jnp.exp(tot), (HALO_ROWS, LANES))], axis=0),
                    P, 2)
    dt_e, ds_e, ys_e = scales[:CHUNK], scales[CHUNK:2 * CHUNK], scales[2 * CHUNK:3 * CHUNK]
    cdec_e = scales[3 * CHUNK:3 * CHUNK + 1]
    col_e = spread(pcs, CHUNK, 3)
    xdt = xc * dt_e
    xdt_b = xdt.astype(BF16)
    xs_b = (xdt * ds_e).astype(BF16)
    low = lax.broadcasted_iota(I32, (CHUNK, LANES), 1) < P

    y_groups = []
    for g in range(SSM_GROUPS):
        bg = bcv[:, g * SSM_STATE:(g + 1) * SSM_STATE]
        cg = bcv[:, D_BC + g * SSM_STATE:D_BC + (g + 1) * SSM_STATE].astype(BF16)
        cbm = lax.dot_general(cg, bg.astype(BF16), (((1,), (1,)), ((), ())), preferred_element_type=F32)
        hg = h_ref[g]
        yoff = jnp.dot(cg, hg.astype(BF16), preferred_element_type=F32)
        yd = []
        for jp in range(HEADS_PER_GROUP // 2):
            tile = (g * HEADS_PER_GROUP) // 2 + jp
            xpair = xdt_b[:, tile * LANES:(tile + 1) * LANES]
            halves = []
            for par in range(2):
                j = 2 * tile + par
                ln = lane0 + j
                col = col_e[:, j * CHUNK:(j + 1) * CHUNK]
                row = pcs_t[ln:ln + 1, :]
                seg = (row - col) if reverse else (col - row)
                lm = jnp.exp(jnp.where(mask, seg, -jnp.inf))
                halves.append(jnp.dot((cbm * lm).astype(BF16), xpair, preferred_element_type=F32))
            yd.append(jnp.where(low, halves[0], halves[1]))
        sl = slice(g * gw, (g + 1) * gw)
        y_groups.append(jnp.concatenate(yd, axis=1) + yoff * ys_e[:, sl])
        snew = jnp.dot(bg.T.astype(BF16), xs_b[:, sl], preferred_element_type=F32)
        h_ref[g] = hg * cdec_e[:, sl] + snew
    y = jnp.concatenate(y_groups, axis=1)

    if not reverse:
        o_ref[...] = y
    else:
        ytot = yf_ref[...] + y + dskip_ref[...] * xc
        yz = ytot * _silu(z_ref[...])
        halves = []
        for g in range(SSM_GROUPS):
            seg = yz[:, g * gw:(g + 1) * gw]
            halves.append(seg * lax.rsqrt(jnp.mean(seg * seg, axis=-1, keepdims=True) + EPS))
        o_ref[...] = (jnp.concatenate(halves, axis=1) * nw_ref[...]).astype(BF16)


def _ssd_pass(proj3, conv_w3, conv_b, dtb, alog, reverse, yf=None, dskip=None, norm_w=None):
    B, S, _ = proj3.shape
    nc = S // CHUNK
    hp = CHUNK // HALO_ROWS
    nh = S // HALO_ROWS
    ch = (lambda s: nc - 1 - s) if reverse else (lambda s: s)
    main = lambda w, col: pl.BlockSpec((None, CHUNK, w), lambda b, s: (b, ch(s), col // w))
    halo_p = lambda w, col: pl.BlockSpec((None, HALO_ROWS, w),
                                         lambda b, s: (b, jnp.maximum(ch(s) * hp - 1, 0), col // w))
    halo_n = lambda w, col: pl.BlockSpec((None, HALO_ROWS, w),
                                         lambda b, s: (b, jnp.minimum((ch(s) + 1) * hp, nh - 1), col // w))
    full = lambda a: pl.BlockSpec(a.shape, lambda b, s: (0,) * a.ndim)
    d_bc2 = 2 * D_BC
    in_specs = [main(D_SSM, COL_X), halo_p(D_SSM, COL_X), halo_n(D_SSM, COL_X),
                main(d_bc2, COL_B), halo_p(d_bc2, COL_B), halo_n(d_bc2, COL_B),
                main(LANES, COL_DT), full(conv_w3), full(conv_b), full(dtb), full(alog)]
    args = [proj3, proj3, proj3, proj3, proj3, proj3, proj3, conv_w3, conv_b, dtb, alog]
    if reverse:
        in_specs = [main(D_SSM, COL_Z), pl.BlockSpec((None, CHUNK, D_SSM), lambda b, s: (b, ch(s), 0))] + in_specs
        in_specs += [full(dskip), full(norm_w)]
        args = [proj3, yf] + args + [dskip, norm_w]
    return pl.pallas_call(
        functools.partial(_ssd_kernel, reverse=reverse, nc=nc),
        grid=(B, nc),
        in_specs=in_specs,
        out_specs=pl.BlockSpec((None, CHUNK, D_SSM), lambda b, s: (b, ch(s), 0)),
        out_shape=jax.ShapeDtypeStruct((B, S, D_SSM), BF16 if reverse else F32),
        scratch_shapes=[pltpu.VMEM((SSM_GROUPS, SSM_STATE, HEADS_PER_GROUP * SSM_HEAD_DIM), F32)],
        compiler_params=_cparams(2),
        name="ssd_bwd" if reverse else "ssd_fwd",
    )(*args)


def _outproj_kernel(x_ref, a_ref, s_ref, w_ref, n2_ref, rw2_ref, x1_ref, h_ref, aff_ref):
    tm = x_ref.shape[0]
    x1 = (x_ref[...]
          + jnp.dot(a_ref[...], w_ref[:D_ATTN, :], preferred_element_type=F32)
          + jnp.dot(s_ref[...], w_ref[D_ATTN:, :], preferred_element_type=F32))
    x1_ref[...] = x1
    hn = _rms(x1, n2_ref[...])
    for s in range(TOKEN_ROWS):
        h_ref[pl.ds(s, tm, stride=TOKEN_ROWS), :] = hn[:, s * LANES:(s + 1) * LANES]
    hn_hi = hn.astype(BF16)
    hn_lo = (hn - hn_hi.astype(F32)).astype(BF16)
    l_hi = jnp.dot(hn_hi, rw2_ref[...], preferred_element_type=F32)
    l_lo = jnp.dot(hn_lo, rw2_ref[:, :N_EXPERTS], preferred_element_type=F32)
    logits = l_hi[:, :N_EXPERTS] + l_hi[:, N_EXPERTS:] + l_lo
    e = jnp.exp(logits - jnp.max(logits, axis=-1, keepdims=True))
    aff_ref[...] = e / jnp.sum(e, axis=-1, keepdims=True)


def _outproj(x2, attn2, ssm2, w_out, norm2_w, router_w, tm):
    T = x2.shape[0]
    row = lambda w: pl.BlockSpec((tm, w), lambda i: (i, 0))
    full = lambda a: pl.BlockSpec(a.shape, lambda i: (0,) * a.ndim)
    return pl.pallas_call(
        _outproj_kernel,
        grid=(T // tm,),
        in_specs=[row(D_MODEL), row(D_ATTN), row(D_SSM), full(w_out), full(norm2_w), full(router_w)],
        out_specs=[row(D_MODEL), pl.BlockSpec((tm * TOKEN_ROWS, LANES), lambda i: (i, 0)), row(N_EXPERTS)],
        out_shape=[jax.ShapeDtypeStruct((T, D_MODEL), F32),
                   jax.ShapeDtypeStruct((T * TOKEN_ROWS, LANES), F32),
                   jax.ShapeDtypeStruct((T, N_EXPERTS), F32)],
        compiler_params=_cparams(1, VMEM_LIMIT),
        name="outproj_router",
    )(x2, attn2, ssm2, w_out, norm2_w, router_w)


def _count(m):
    c = jnp.sum(jnp.where(m, 1.0, 0.0), axis=0, keepdims=True)
    return jnp.sum(c, axis=1, keepdims=True)


def _tri(n, m, fn):
    return jnp.where(fn(lax.broadcasted_iota(I32, (n, m), 0), lax.broadcasted_iota(I32, (n, m), 1)),
                     1.0, 0.0).astype(BF16)


def _dot_u16(lhs01, x):
    hi = jnp.floor(x * (1.0 / 256.0))
    lo = x - hi * 256.0
    return (jnp.dot(lhs01, hi.astype(BF16), preferred_element_type=F32) * 256.0
            + jnp.dot(lhs01, lo.astype(BF16), preferred_element_type=F32))


def _cumsum_rowmajor(x):
    R = x.shape[0]
    within = jnp.dot(x.astype(BF16), _tri(LANES, LANES, lambda k, l: k <= l), preferred_element_type=F32)
    rowtot = jnp.broadcast_to(within[:, LANES - 1:LANES], (R, LANES))
    before = _dot_u16(_tri(R, R, lambda i, k: k < i), rowtot)
    return within + before


def _route_kernel(a_ref, idx_ref, dst_ref, g_ref, cs_ref, ce_ref, sel_s, cnt_s, cs_s, rank_s, *, cap):
    s = pl.program_id(0)
    R = a_ref.shape[0]

    @pl.when(s < N_EXPERTS)
    def _select():
        bits = pltpu.bitcast(a_ref[...], I32)
        capf = jnp.float32(cap)

        def body(k, tau):
            cand = tau | lax.shift_left(jnp.int32(1), 30 - k)
            return jnp.where(_count(bits >= cand) >= capf, cand, tau)

        tau = lax.fori_loop(0, 31, body, jnp.zeros((1, 1), I32))
        gt = bits > tau
        eq = bits == tau
        need = capf - _count(gt)
        ties = _cumsum_rowmajor(jnp.where(eq, 1.0, 0.0))
        sel = jnp.where(gt | (eq & (ties <= need)), 1.0, 0.0)
        sel_s[s] = sel

        @pl.when(s == 0)
        def _():
            cnt_s[...] = sel

        @pl.when(s > 0)
        def _():
            cnt_s[...] += sel

    @pl.when(s == N_EXPERTS)
    def _prefix():
        cnt = cnt_s[...]
        ce = _cumsum_rowmajor(cnt)
        cs_s[...] = ce - cnt
        rank_s[...] = jnp.zeros_like(rank_s)
        cs_ref[...] = ce - cnt
        ce_ref[...] = ce

    @pl.when(s >= N_EXPERTS)
    def _invert():
        e = s - N_EXPERTS
        sel = sel_s[e]
        rank = rank_s[...]
        q = cs_s[...] + rank
        rank_s[...] = rank + sel
        a = a_ref[...]
        within = jnp.dot(sel.astype(BF16), _tri(LANES, LANES, lambda k, l: k <= l), preferred_element_type=F32)
        n_b = jnp.broadcast_to(within[:, LANES - 1:LANES], (R, LANES))
        rowend = jnp.dot(_tri(R, R, lambda i, k: k <= i), n_b.astype(BF16), preferred_element_type=F32)
        slot = lax.broadcasted_iota(I32, (R, cap), 1).astype(F32)
        done = jnp.where(rowend[:, 0:1] <= slot, 1.0, 0.0)
        ones = jnp.ones((8, R), BF16)
        row_p = jnp.dot(ones, done.astype(BF16), preferred_element_type=F32)[0:1]
        start_p = jnp.dot(ones, (done * n_b[:, 0:1]).astype(BF16), preferred_element_type=F32)[0:1]
        onehot = jnp.where(lax.broadcasted_iota(I32, (R, cap), 0).astype(F32) == row_p, 1.0, 0.0).astype(BF16)
        w_t = jnp.dot(within.T.astype(BF16), onehot, preferred_element_type=F32)
        k_in_row = lax.broadcasted_iota(I32, (1, cap), 1).astype(F32) - start_p
        lane_p = jnp.sum(jnp.where(w_t <= k_in_row, 1.0, 0.0), axis=0, keepdims=True)
        pick = lax.broadcasted_iota(I32, (LANES, cap), 0).astype(F32) == lane_p

        def take(x):
            v = jnp.dot(x.T.astype(BF16), onehot, preferred_element_type=F32)
            return jnp.sum(jnp.where(pick, v, 0.0), axis=0, keepdims=True)

        q_hi = jnp.floor(q * (1.0 / 256.0))
        a_hi = a.astype(BF16).astype(F32)
        a_mid = (a - a_hi).astype(BF16).astype(F32)
        a_lo = a - a_hi - a_mid
        idx_ref[...] = (row_p * LANES + lane_p).astype(I32)
        dst_ref[...] = (take(q_hi) * 256.0 + take(q - q_hi * 256.0)).astype(I32)
        g_ref[...] = take(a_hi) + take(a_mid) + take(a_lo)


def _route(aff_t3, cap):
    E, R, _ = aff_t3.shape
    assert R <= 256 and E == N_EXPERTS
    slot_spec = pl.BlockSpec((None, 1, cap), lambda s: (jnp.maximum(s - N_EXPERTS, 0), 0, 0))
    tok_spec = pl.BlockSpec((R, LANES), lambda s: (0, 0))
    return pl.pallas_call(
        functools.partial(_route_kernel, cap=cap),
        grid=(2 * E,),
        in_specs=[pl.BlockSpec((None, R, LANES), lambda s: (s % N_EXPERTS, 0, 0))],
        out_specs=[slot_spec, slot_spec, slot_spec, tok_spec, tok_spec],
        out_shape=[jax.ShapeDtypeStruct((E, 1, cap), I32), jax.ShapeDtypeStruct((E, 1, cap), I32),
                   jax.ShapeDtypeStruct((E, 1, cap), F32),
                   jax.ShapeDtypeStruct((R, LANES), F32), jax.ShapeDtypeStruct((R, LANES), F32)],
        scratch_shapes=[pltpu.VMEM((E, R, LANES), F32), pltpu.VMEM((R, LANES), F32),
                        pltpu.VMEM((R, LANES), F32), pltpu.VMEM((R, LANES), F32)],
        compiler_params=_cparams(1, VMEM_LIMIT),
        name="route",
    )(aff_t3)


def _ffn_kernel(idx_ref, dst_ref, h_hbm, gt_ref, wg_ref, wu_ref, wd_ref, c_hbm,
                xs_stage, xsb, acc_ref, o_stage, gsem, ssem, *, tm, cap):
    e, m, f = pl.program_id(0), pl.program_id(1), pl.program_id(2)
    nf = pl.num_programs(2)
    base = e * cap + m * tm
    tr = TOKEN_ROWS

    def token_rows(t):
        return pl.ds(pl.multiple_of(t * tr, tr), tr)

    def gather_rows(tile_base, r0, n):
        def issue(r, carry):
            pltpu.make_async_copy(h_hbm.at[token_rows(idx_ref[tile_base + r]), :], xs_stage.at[token_rows(r), :],
                                  gsem).start()
            return carry

        lax.fori_loop(0, n, lambda r, c: issue(r0 + r, c), 0, unroll=8)

    is_first = (e == 0) & (m == 0)
    is_last = (e == pl.num_programs(0) - 1) & (m == pl.num_programs(1) - 1)

    @pl.when(f == 0)
    def _gather():
        @pl.when(is_first)
        def _():
            gather_rows(base, 0, tm)

        pltpu.make_async_copy(h_hbm.at[pl.ds(0, tm * tr), :], xs_stage, gsem).wait()
        for s in range(tr):
            xsb[:, s * LANES:(s + 1) * LANES] = xs_stage[pl.ds(s, tm, stride=tr), :].astype(BF16)

    @pl.when((f >= 1) & (f <= PREFETCH_STEPS) & jnp.logical_not(is_last))
    def _prefetch():
        gather_rows(base + tm, (f - 1) * (tm // PREFETCH_STEPS), tm // PREFETCH_STEPS)

    xs = xsb[...]
    g = jnp.dot(xs, wg_ref[...], preferred_element_type=F32)
    u = jnp.dot(xs, wu_ref[...], preferred_element_type=F32)
    d = jnp.dot((_silu(g) * u).astype(BF16), wd_ref[...], preferred_element_type=F32)

    @pl.when(f == 0)
    def _():
        acc_ref[...] = d

    @pl.when(f > 0)
    def _():
        acc_ref[...] += d

    def scatter_done():
        pltpu.make_async_copy(o_stage, c_hbm.at[pl.ds(0, tm * tr), :], ssem).wait()

    @pl.when(f == nf - 1)
    def _scatter():
        @pl.when(jnp.logical_not(is_first))
        def _():
            scatter_done()

        for j in range(tm // LANES):
            rows = acc_ref[j * LANES:(j + 1) * LANES, :] * gt_ref[:, j:j + 1]
            for s in range(tr):
                o_stage[pl.ds(j * LANES * tr + s, LANES, stride=tr), :] = rows[:, s * LANES:(s + 1) * LANES]

        def issue(r, carry):
            pltpu.make_async_copy(o_stage.at[token_rows(r), :], c_hbm.at[token_rows(dst_ref[base + r]), :],
                                  ssem).start()
            return carry

        lax.fori_loop(0, tm, issue, 0, unroll=8)

        @pl.when(is_last)
        def _():
            scatter_done()


def _ffn(idx, dst, h_rows, g_t, w_gate, w_up, w_down, tm, tf):
    E, n_m = g_t.shape[0], g_t.shape[1]
    cap = n_m * tm
    n_contrib = E * cap
    grid_spec = pltpu.PrefetchScalarGridSpec(
        num_scalar_prefetch=2,
        grid=(E, n_m, D_FF // tf),
        in_specs=[pl.BlockSpec(memory_space=pl.ANY),
                  pl.BlockSpec((None, None, LANES, tm // LANES), lambda e, m, f, i, d: (e, m, 0, 0)),
                  pl.BlockSpec((None, D_MODEL, tf), lambda e, m, f, i, d: (e, 0, f)),
                  pl.BlockSpec((None, D_MODEL, tf), lambda e, m, f, i, d: (e, 0, f)),
                  pl.BlockSpec((None, tf, D_MODEL), lambda e, m, f, i, d: (e, f, 0))],
        out_specs=pl.BlockSpec(memory_space=pl.ANY),
        scratch_shapes=[pltpu.VMEM((tm * TOKEN_ROWS, LANES), F32), pltpu.VMEM((tm, D_MODEL), BF16),
                        pltpu.VMEM((tm, D_MODEL), F32), pltpu.VMEM((tm * TOKEN_ROWS, LANES), F32),
                        pltpu.SemaphoreType.DMA, pltpu.SemaphoreType.DMA],
    )
    return pl.pallas_call(
        functools.partial(_ffn_kernel, tm=tm, cap=cap),
        grid_spec=grid_spec,
        out_shape=jax.ShapeDtypeStruct((n_contrib * TOKEN_ROWS, LANES), F32),
        compiler_params=_cparams(3, FFN_VMEM_LIMIT),
        name="expert_ffn",
    )(idx, dst, h_rows, g_t, w_gate, w_up, w_down)


def _combine_kernel(ch_ref, tt_ref, flag_ref, x1_ref, cs_ref, ce_ref, c_ref, nw_ref, out_ref, acc_ref, *, cg, rows):
    k = pl.program_id(0)
    flag = flag_ref[k]

    @pl.when((flag & 2) > 0)
    def _():
        acc_ref[...] = x1_ref[...]

    @pl.when((flag & 1) > 0)
    def _():
        cio = (lax.broadcasted_iota(I32, (cg, LANES), 0) + ch_ref[k] * cg).astype(F32)
        parts = [jnp.where((cs_ref[r:r + 1, :] <= cio) & (cio < ce_ref[r:r + 1, :]), 1.0, 0.0) for r in range(rows)]
        a = jnp.concatenate(parts, axis=1).T.astype(BF16)
        chunk = jnp.concatenate([c_ref[pl.ds(s, cg, stride=TOKEN_ROWS), :].astype(BF16)
                                 for s in range(TOKEN_ROWS)], axis=1)
        acc_ref[...] += jnp.dot(a, chunk, preferred_element_type=F32)

    @pl.when((flag & 4) > 0)
    def _():
        out_ref[...] = _rms(acc_ref[...], nw_ref[...])


def _combine(ch, tt, flag, x1, cs3, ce3, contrib, final_w, tb, cg):
    T = x1.shape[0]
    rows = tb // LANES
    n_pairs = ch.shape[0]
    grid_spec = pltpu.PrefetchScalarGridSpec(
        num_scalar_prefetch=3,
        grid=(n_pairs,),
        in_specs=[pl.BlockSpec((tb, D_MODEL), lambda k, ch, tt, fl: (tt[k], 0)),
                  pl.BlockSpec((None, rows, LANES), lambda k, ch, tt, fl: (tt[k], 0, 0)),
                  pl.BlockSpec((None, rows, LANES), lambda k, ch, tt, fl: (tt[k], 0, 0)),
                  pl.BlockSpec((cg * TOKEN_ROWS, LANES), lambda k, ch, tt, fl: (ch[k], 0)),
                  pl.BlockSpec((1, D_MODEL), lambda k, ch, tt, fl: (0, 0))],
        out_specs=pl.BlockSpec((tb, D_MODEL), lambda k, ch, tt, fl: (tt[k], 0)),
        scratch_shapes=[pltpu.VMEM((tb, D_MODEL), F32)],
    )
    return pl.pallas_call(
        functools.partial(_combine_kernel, cg=cg, rows=rows),
        grid_spec=grid_spec,
        out_shape=jax.ShapeDtypeStruct((T, D_MODEL), F32),
        compiler_params=_cparams(1, VMEM_LIMIT),
        name="combine_final",
    )(ch, tt, flag, x1, cs3, ce3, contrib, final_w)


def _combine_schedule(cs, ce, tb, cg):
    T = cs.size
    ntt = T // tb
    total = CAPACITY_FACTOR * T
    nch = total // cg
    lo = cs.reshape(ntt, tb)[:, 0].astype(I32)
    hi = ce.reshape(ntt, tb)[:, -1].astype(I32)
    first = jnp.minimum(lo, total - 1) // cg
    last = jnp.where(hi > lo, (hi - 1) // cg, first)
    n = last - first + 1
    ends = jnp.cumsum(n)
    offs = ends - n
    n_pairs = ntt + nch
    ks = jnp.arange(n_pairs, dtype=I32)
    tile = jnp.minimum(jnp.searchsorted(ends, ks, side="right").astype(I32), ntt - 1)
    valid = ks < ends[-1]
    chunk = jnp.where(valid, first[tile] + ks - offs[tile], last[-1])
    flag = jnp.where(valid, 1 + 2 * (ks == offs[tile]) + 4 * (ks == ends[tile] - 1), 0).astype(I32)
    return chunk.astype(I32), tile, flag


def _t5_buckets(rel):
    nb = NUM_BUCKETS // 2
    ret = (rel > 0).astype(np.int32) * nb
    n = np.abs(rel)
    max_exact = nb // 2
    large = max_exact + (np.log(np.maximum(n, 1) / max_exact) / np.log(MAX_DISTANCE / max_exact)
                         * (nb - max_exact)).astype(np.int32)
    large = np.minimum(large, nb - 1)
    return ret + np.where(n < max_exact, n, large)


def _tile(n, pref):
    t = min(n, pref)
    assert n % t == 0
    return t


def _trunk(x, w):
    B, S, _ = x.shape
    T = B * S
    cap = CAPACITY_FACTOR * T // N_EXPERTS
    x2 = x.reshape(T, D_MODEL)
    tm = _tile(T, 512)

    qkv, main = _inproj(x2, w["norm1_w"], w["w_qkv"], w["w_main"], tm)
    main3 = main.reshape(B, S, MAIN_COLS)
    attn = _attention(qkv.reshape(B, S, QKV_COLS), w["bias_tab"], w["attn_sink"], w["attn_norm_w"])
    yf = _ssd_pass(main3, w["conv_w"], w["conv_b"], w["dtb"], w["alog"], reverse=False)
    ssm = _ssd_pass(main3, w["conv_w"], w["conv_b"], w["dtb"], w["alog"], reverse=True,
                    yf=yf, dskip=w["dskip"], norm_w=w["ssm_norm_w"])
    x1, h_rows, aff = _outproj(x2, attn.reshape(T, D_ATTN), ssm.reshape(T, D_SSM), w["w_out"], w["norm2_w"],
                          w["router_w2"], tm)

    tb = _tile(T, 256)
    cg = 256
    idx, dst, g, cs, ce = _route(aff.T.reshape(N_EXPERTS, T // LANES, LANES), cap)
    tmf = _tile(cap, 1024)
    g_t = jnp.swapaxes(g.reshape(N_EXPERTS, cap // tmf, tmf // LANES, LANES), 2, 3)
    contrib = _ffn(idx.reshape(-1), dst.reshape(-1), h_rows, g_t, w["w_gate"], w["w_up"], w["w_down"], tmf, 512)
    ch, tt, flag = _combine_schedule(cs, ce, tb, cg)
    tile3 = lambda a: a.reshape(T // tb, tb // LANES, LANES)
    y = _combine(ch, tt, flag, x1, tile3(cs), tile3(ce), contrib, w["final_norm_w"], tb, cg)
    return y.reshape(B, S, D_MODEL)


def _prep_weights(rel_bias, norm1_w, w_in, conv_w, conv_b, dt_bias_fwd, dt_bias_bwd, a_log_fwd, a_log_bwd,
                  d_skip, ssm_norm_w, attn_sink, attn_norm_w, w_out, norm2_w, router_w, w_gate, w_up, w_down,
                  final_norm_w):
    o1 = D_ATTN; o2 = o1 + D_KV; o3 = o2 + D_KV; o4 = o3 + D_SSM; o5 = o4 + D_SSM + 2 * D_BC
    wi = w_in[0]
    scale = 1.0 / math.sqrt(HEAD_DIM)
    w_qkv = jnp.concatenate([wi[:, :o1] * scale, wi[:, o1:o3]], axis=1).astype(BF16)
    w_main = jnp.concatenate([wi[:, o3:], jnp.zeros((D_MODEL, LANES - 2 * SSM_HEADS), F32)], axis=1).astype(BF16)
    rel = np.arange(3 * BLOCK)[None, :] - BLOCK - np.arange(BLOCK)[:, None]
    onehot = (jnp.asarray(_t5_buckets(rel), I32)[..., None] == jnp.arange(NUM_BUCKETS, dtype=I32)).astype(F32)
    bias_tab = jnp.einsum("qsn,nh->hqs", onehot, rel_bias.astype(F32), precision=lax.Precision.HIGHEST)
    bias_tab = jnp.where(jnp.asarray(np.abs(rel) <= WINDOW)[None], bias_tab, NEG_INF)
    rw = router_w[0].astype(F32)
    rw_hi = rw.astype(BF16)
    router_w2 = jnp.concatenate([rw_hi, (rw - rw_hi.astype(F32)).astype(BF16)], axis=1)
    pad = jnp.zeros((LANES - 2 * SSM_HEADS,), F32)
    row = lambda a: a.reshape(1, -1).astype(F32)
    return dict(
        norm1_w=row(norm1_w[0]), w_qkv=w_qkv, w_main=w_main, bias_tab=bias_tab, attn_sink=attn_sink[0].astype(F32),
        attn_norm_w=row(attn_norm_w[0]), conv_w=conv_w[0].T.astype(F32), conv_b=row(conv_b[0]),
        dtb=row(jnp.concatenate([dt_bias_fwd[0], dt_bias_bwd[0], pad])),
        alog=row(jnp.concatenate([a_log_fwd[0], a_log_bwd[0], pad])),
        dskip=row(jnp.repeat(d_skip[0], SSM_HEAD_DIM)), ssm_norm_w=row(ssm_norm_w[0]),
        w_out=w_out[0].astype(BF16), norm2_w=row(norm2_w[0]), router_w2=router_w2,
        w_gate=w_gate[0].astype(BF16), w_up=w_up[0].astype(BF16), w_down=w_down[0].astype(BF16),
        final_norm_w=row(final_norm_w))


def kernel(x_prompt, x_sample, rel_bias, norm1_w, w_in, conv_w, conv_b, dt_bias_fwd, dt_bias_bwd, a_log_fwd, a_log_bwd, d_skip, ssm_norm_w, attn_sink, attn_norm_w, w_out, norm2_w, router_w, w_gate, w_up, w_down, final_norm_w):
    assert norm1_w.shape[0] == 1
    w = _prep_weights(rel_bias, norm1_w, w_in, conv_w, conv_b, dt_bias_fwd, dt_bias_bwd, a_log_fwd, a_log_bwd,
                      d_skip, ssm_norm_w, attn_sink, attn_norm_w, w_out, norm2_w, router_w, w_gate, w_up,
                      w_down, final_norm_w)
    return (_trunk(x_prompt, w), _trunk(x_sample, w))
```

```python
import functools
import math

import numpy as np
import jax
import jax.numpy as jnp
from jax import lax
from jax.experimental import pallas as pl
from jax.experimental.pallas import tpu as pltpu

F32 = jnp.float32
BF16 = jnp.bfloat16
I32 = jnp.int32

D_MODEL = 2048
HEAD_DIM = 64
N_HEADS = 16
N_KV_HEADS = 4
D_ATTN = 1024
D_KV = 256
WINDOW = 128
BLOCK = 128
NUM_BUCKETS = 32
MAX_DISTANCE = 128
SSM_HEAD_DIM = 64
SSM_HEADS = 16
D_SSM = 1024
SSM_STATE = 128
SSM_GROUPS = 2
HEADS_PER_GROUP = SSM_HEADS // SSM_GROUPS
D_BC = 256
CHUNK = 128
N_EXPERTS = 16
CAPACITY_FACTOR = 2
D_FF = 2048
EPS = 1e-6
NEG_INF = -1e30

LANES = 128
HALO_ROWS = 8
TOKEN_ROWS = D_MODEL // LANES

QKV_COLS = D_ATTN + 2 * D_KV
COL_QK, COL_QV = D_ATTN, D_ATTN + D_KV
COL_Z, COL_X, COL_B, COL_DT = 0, 1024, 2048, 2560
MAIN_COLS = COL_DT + LANES
MAIN_TN = 896

VMEM_LIMIT = 56 * 1024 * 1024
FFN_VMEM_LIMIT = 60 * 1024 * 1024


def _cparams(n_axes, vmem=None):
    return pltpu.CompilerParams(dimension_semantics=("arbitrary",) * n_axes,
                                vmem_limit_bytes=vmem)


def _rms(x, w):
    return x * lax.rsqrt(jnp.mean(x * x, axis=-1, keepdims=True) + EPS) * w


def _silu(x):
    return x * jax.nn.sigmoid(x)


def _inproj_kernel(x_ref, nw_ref, wq_ref, wr_ref, oq_ref, om_ref, hn_ref):
    j = pl.program_id(1)

    @pl.when(j == 0)
    def _():
        hn = _rms(x_ref[...], nw_ref[...]).astype(BF16)
        hn_ref[...] = hn
        oq_ref[...] = jnp.dot(hn, wq_ref[...], preferred_element_type=F32).astype(BF16)

    @pl.when(j > 0)
    def _():
        om_ref[...] = jnp.dot(hn_ref[...], wr_ref[...], preferred_element_type=F32)


def _inproj(x2, norm_w, w_qkv, w_main, tm):
    T = x2.shape[0]
    rest = lambda j: jnp.maximum(j - 1, 0)
    return pl.pallas_call(
        _inproj_kernel,
        grid=(T // tm, 1 + MAIN_COLS // MAIN_TN),
        in_specs=[pl.BlockSpec((tm, D_MODEL), lambda i, j: (i, 0)),
                  pl.BlockSpec((1, D_MODEL), lambda i, j: (0, 0)),
                  pl.BlockSpec((D_MODEL, QKV_COLS), lambda i, j: (0, 0), pipeline_mode=pl.Buffered(1)),
                  pl.BlockSpec((D_MODEL, MAIN_TN), lambda i, j: (0, rest(j)))],
        out_specs=[pl.BlockSpec((tm, QKV_COLS), lambda i, j: (i, 0)),
                   pl.BlockSpec((tm, MAIN_TN), lambda i, j: (i, rest(j)))],
        out_shape=[jax.ShapeDtypeStruct((T, QKV_COLS), BF16), jax.ShapeDtypeStruct((T, MAIN_COLS), F32)],
        scratch_shapes=[pltpu.VMEM((tm, D_MODEL), BF16)],
        compiler_params=_cparams(2, VMEM_LIMIT),
        name="inproj",
    )(x2, norm_w, w_qkv, w_main)


def _attn_kernel(q_ref, kp_ref, ko_ref, kn_ref, vp_ref, vo_ref, vn_ref, bias_ref, sink_ref, nw_ref,
                 o_ref, *, nb):
    i = pl.program_id(1)
    kf = jnp.concatenate([kp_ref[...], ko_ref[...], kn_ref[...]], axis=0).astype(F32)
    vf = jnp.concatenate([vp_ref[...], vo_ref[...], vn_ref[...]], axis=0).astype(F32)
    low = lax.broadcasted_iota(I32, (3 * BLOCK, LANES), 1) < HEAD_DIM
    low_q = lax.broadcasted_iota(I32, (BLOCK, LANES), 1) < HEAD_DIM
    kcol = lax.broadcasted_iota(I32, (1, 3 * BLOCK), 1)
    edge = (jnp.where((kcol < BLOCK) & (i == 0), NEG_INF, 0.0)
            + jnp.where((kcol >= 2 * BLOCK) & (i == nb - 1), NEG_INF, 0.0))
    top = lax.broadcasted_iota(I32, (2 * BLOCK, 1), 0) < BLOCK
    group = N_HEADS // N_KV_HEADS
    tiles = []
    for g in range(N_KV_HEADS):
        t, upper = (g * HEAD_DIM) // LANES, (g * HEAD_DIM) % LANES != 0
        kt = kf[:, t * LANES:(t + 1) * LANES]
        vt = vf[:, t * LANES:(t + 1) * LANES]
        kr = pltpu.roll(kt, HEAD_DIM, 1)
        vr = pltpu.roll(vt, HEAD_DIM, 1)
        k_lo, k_up = (kr, kt) if upper else (kt, kr)
        v_lo, v_up = (vr, vt) if upper else (vt, vr)
        rhs = [(jnp.where(low, k_lo, 0.0).astype(BF16), jnp.where(low, v_lo, 1.0).astype(BF16)),
               (jnp.where(low, 0.0, k_up).astype(BF16), jnp.where(low, 1.0, v_up).astype(BF16))]
        h0 = g * group
        q2 = jnp.concatenate([q_ref[:, (h0 // 2) * LANES:(h0 // 2 + 1) * LANES],
                              q_ref[:, (h0 // 2 + 1) * LANES:(h0 // 2 + 2) * LANES]], axis=0)
        res = []
        for par in range(2):
            kz, vz = rhs[par]
            ha, hb = h0 + par, h0 + 2 + par
            s = lax.dot_general(q2, kz, (((1,), (1,)), ((), ())), preferred_element_type=F32)
            s = s + jnp.concatenate([bias_ref[ha], bias_ref[hb]], axis=0) + edge
            sk = jnp.where(top, sink_ref[ha], sink_ref[hb])
            m = jnp.maximum(jnp.max(s, axis=-1, keepdims=True), sk)
            p = jnp.exp(s - m).astype(BF16)
            pv = jnp.dot(p, vz, preferred_element_type=F32)
            den = (pv[:, LANES - 1:LANES] if par == 0 else pv[:, 0:1]) + jnp.exp(sk - m)
            res.append(pv * (1.0 / den))
        tiles.append(jnp.where(low_q, res[0][:BLOCK], res[1][:BLOCK]))
        tiles.append(jnp.where(low_q, res[0][BLOCK:], res[1][BLOCK:]))
    o = jnp.concatenate(tiles, axis=-1)
    o_ref[...] = _rms(o, nw_ref[...]).astype(BF16)


def _attention(qkv3, bias_tab, sink, norm_w):
    B, S, _ = qkv3.shape
    nb = S // BLOCK
    kcol, vcol = COL_QK // D_KV, COL_QV // D_KV
    prev = lambda i: jnp.maximum(i - 1, 0)
    nxt = lambda i: jnp.minimum(i + 1, nb - 1)
    kv = lambda col, f: pl.BlockSpec((None, BLOCK, D_KV), lambda b, i: (b, f(i), col))
    same = lambda i: i
    return pl.pallas_call(
        functools.partial(_attn_kernel, nb=nb),
        grid=(B, nb),
        in_specs=[pl.BlockSpec((None, BLOCK, D_ATTN), lambda b, i: (b, i, 0)),
                  kv(kcol, prev), kv(kcol, same), kv(kcol, nxt),
                  kv(vcol, prev), kv(vcol, same), kv(vcol, nxt),
                  pl.BlockSpec((N_HEADS, BLOCK, 3 * BLOCK), lambda b, i: (0, 0, 0)),
                  pl.BlockSpec(memory_space=pltpu.SMEM),
                  pl.BlockSpec((1, D_ATTN), lambda b, i: (0, 0))],
        out_specs=pl.BlockSpec((None, BLOCK, D_ATTN), lambda b, i: (b, i, 0)),
        out_shape=jax.ShapeDtypeStruct((B, S, D_ATTN), BF16),
        compiler_params=_cparams(2),
        name="attention",
    )(qkv3, qkv3, qkv3, qkv3, qkv3, qkv3, qkv3, bias_tab, sink, norm_w)


def _conv_silu(u, prev8, next8, w, b, has_prev, has_next):
    prev_row = jnp.where(has_prev, prev8[HALO_ROWS - 1:HALO_ROWS, :], 0.0)
    next_row = jnp.where(has_next, next8[0:1, :], 0.0)
    rid = lax.broadcasted_iota(I32, u.shape, 0)
    um = jnp.where(rid == 0, prev_row, pltpu.roll(u, 1, 0))
    up = jnp.where(rid == CHUNK - 1, next_row, pltpu.roll(u, CHUNK - 1, 0))
    y = b + um * w[0:1, :] + u * w[1:2, :] + up * w[2:3, :]
    return _silu(y)


def _ssd_kernel(*refs, reverse, nc):
    if reverse:
        (z_ref, yf_ref, x_ref, xp_ref, xn_ref, bc_ref, bcp_ref, bcn_ref, dt_ref, cw_ref, cb_ref, dtb_ref,
         alog_ref, dskip_ref, nw_ref, o_ref, h_ref) = refs
    else:
        (x_ref, xp_ref, xn_ref, bc_ref, bcp_ref, bcn_ref, dt_ref, cw_ref, cb_ref, dtb_ref,
         alog_ref, o_ref, h_ref) = refs
    step = pl.program_id(1)
    c = (nc - 1 - step) if reverse else step

    @pl.when(step == 0)
    def _():
        h_ref[...] = jnp.zeros_like(h_ref)

    has_prev, has_next = c > 0, c < nc - 1
    cw, cb = cw_ref[...], cb_ref[...]
    xc = _conv_silu(x_ref[...], xp_ref[...], xn_ref[...], cw[:, :D_SSM], cb[:, :D_SSM], has_prev, has_next)
    bcv = _conv_silu(bc_ref[...], bcp_ref[...], bcn_ref[...], cw[:, D_SSM:], cb[:, D_SSM:], has_prev, has_next)

    raw = dt_ref[...] + dtb_ref[...]
    dt = jnp.maximum(raw, 0.0) + jnp.log1p(jnp.exp(-jnp.abs(raw)))
    a = dt * (-jnp.exp(alog_ref[...]))
    li = lax.broadcasted_iota(I32, (CHUNK, CHUNK), 0)
    si = lax.broadcasted_iota(I32, (CHUNK, CHUNK), 1)
    incl = jnp.dot(jnp.where(li >= si, 1.0, 0.0).astype(F32), a, precision=lax.Precision.HIGHEST,
                   preferred_element_type=F32)
    tot = incl[CHUNK - 1:CHUNK, :]
    if reverse:
        pcs = incl - a
        dstate = jnp.exp(pcs)
        yscale = jnp.exp(tot - pcs)
        mask = si >= li
    else:
        pcs = incl
        dstate = jnp.exp(tot - pcs)
        yscale = jnp.exp(pcs)
        mask = li >= si
    pcs_t = pcs.T
    lane0 = SSM_HEADS if reverse else 0
    P = SSM_HEAD_DIM
    gw = HEADS_PER_GROUP * P

    def spread(x, width, pieces):
        k = lax.broadcasted_iota(I32, (LANES, SSM_HEADS * width), 0)
        c = lax.broadcasted_iota(I32, (LANES, SSM_HEADS * width), 1)
        sel = jnp.where(k == lane0 + c // width, 1.0, 0.0).astype(BF16)
        out, rem = None, x
        for _ in range(pieces):
            piece = rem.astype(BF16)
            rem = rem - piece.astype(F32)
            d = jnp.dot(piece, sel, preferred_element_type=F32)
            out = d if out is None else out + d
        return out

    scales = spread(jnp.concatenate([dt, dstate, yscale, jnp.broadcast_to(jnp.exp(tot), (HALO_ROWS, LANES))], axis=0),
                    P, 2)
    dt_e, ds_e, ys_e = scales[:CHUNK], scales[CHUNK:2 * CHUNK], scales[2 * CHUNK:3 * CHUNK]
    cdec_e = scales[3 * CHUNK:3 * CHUNK + 1]
    col_e = spread(pcs, CHUNK, 3)
    xdt = xc * dt_e
    xdt_b = xdt.astype(BF16)
    xs_b = (xdt * ds_e).astype(BF16)
    low = lax.broadcasted_iota(I32, (CHUNK, LANES), 1) < P

    y_groups = []
    for g in range(SSM_GROUPS):
        bg = bcv[:, g * SSM_STATE:(g + 1) * SSM_STATE]
        cg = bcv[:, D_BC + g * SSM_STATE:D_BC + (g + 1) * SSM_STATE].astype(BF16)
        cbm = lax.dot_general(cg, bg.astype(BF16), (((1,), (1,)), ((), ())), preferred_element_type=F32)
        hg = h_ref[g]
        yoff = jnp.dot(cg, hg.astype(BF16), preferred_element_type=F32)
        yd = []
        for jp in range(HEADS_PER_GROUP // 2):
            tile = (g * HEADS_PER_GROUP) // 2 + jp
            xpair = xdt_b[:, tile * LANES:(tile + 1) * LANES]
            halves = []
            for par in range(2):
                j = 2 * tile + par
                ln = lane0 + j
                col = col_e[:, j * CHUNK:(j + 1) * CHUNK]
                row = pcs_t[ln:ln + 1, :]
                seg = (row - col) if reverse else (col - row)
                lm = jnp.exp(jnp.where(mask, seg, -jnp.inf))
                halves.append(jnp.dot((cbm * lm).astype(BF16), xpair, preferred_element_type=F32))
            yd.append(jnp.where(low, halves[0], halves[1]))
        sl = slice(g * gw, (g + 1) * gw)
        y_groups.append(jnp.concatenate(yd, axis=1) + yoff * ys_e[:, sl])
        snew = jnp.dot(bg.T.astype(BF16), xs_b[:, sl], preferred_element_type=F32)
        h_ref[g] = hg * cdec_e[:, sl] + snew
    y = jnp.concatenate(y_groups, axis=1)

    if not reverse:
        o_ref[...] = y
    else:
        ytot = yf_ref[...] + y + dskip_ref[...] * xc
        yz = ytot * _silu(z_ref[...])
        halves = []
        for g in range(SSM_GROUPS):
            seg = yz[:, g * gw:(g + 1) * gw]
            halves.append(seg * lax.rsqrt(jnp.mean(seg * seg, axis=-1, keepdims=True) + EPS))
        o_ref[...] = (jnp.concatenate(halves, axis=1) * nw_ref[...]).astype(BF16)


def _ssd_pass(proj3, conv_w3, conv_b, dtb, alog, reverse, yf=None, dskip=None, norm_w=None):
    B, S, _ = proj3.shape
    nc = S // CHUNK
    hp = CHUNK // HALO_ROWS
    nh = S // HALO_ROWS
    ch = (lambda s: nc - 1 - s) if reverse else (lambda s: s)
    main = lambda w, col: pl.BlockSpec((None, CHUNK, w), lambda b, s: (b, ch(s), col // w))
    halo_p = lambda w, col: pl.BlockSpec((None, HALO_ROWS, w),
                                         lambda b, s: (b, jnp.maximum(ch(s) * hp - 1, 0), col // w))
    halo_n = lambda w, col: pl.BlockSpec((None, HALO_ROWS, w),
                                         lambda b, s: (b, jnp.minimum((ch(s) + 1) * hp, nh - 1), col // w))
    full = lambda a: pl.BlockSpec(a.shape, lambda b, s: (0,) * a.ndim)
    d_bc2 = 2 * D_BC
    in_specs = [main(D_SSM, COL_X), halo_p(D_SSM, COL_X), halo_n(D_SSM, COL_X),
                main(d_bc2, COL_B), halo_p(d_bc2, COL_B), halo_n(d_bc2, COL_B),
                main(LANES, COL_DT), full(conv_w3), full(conv_b), full(dtb), full(alog)]
    args = [proj3, proj3, proj3, proj3, proj3, proj3, proj3, conv_w3, conv_b, dtb, alog]
    if reverse:
        in_specs = [main(D_SSM, COL_Z), pl.BlockSpec((None, CHUNK, D_SSM), lambda b, s: (b, ch(s), 0))] + in_specs
        in_specs += [full(dskip), full(norm_w)]
        args = [proj3, yf] + args + [dskip, norm_w]
    return pl.pallas_call(
        functools.partial(_ssd_kernel, reverse=reverse, nc=nc),
        grid=(B, nc),
        in_specs=in_specs,
        out_specs=pl.BlockSpec((None, CHUNK, D_SSM), lambda b, s: (b, ch(s), 0)),
        out_shape=jax.ShapeDtypeStruct((B, S, D_SSM), BF16 if reverse else F32),
        scratch_shapes=[pltpu.VMEM((SSM_GROUPS, SSM_STATE, HEADS_PER_GROUP * SSM_HEAD_DIM), F32)],
        compiler_params=_cparams(2),
        name="ssd_bwd" if reverse else "ssd_fwd",
    )(*args)


def _outproj_kernel(x_ref, a_ref, s_ref, w_ref, n2_ref, rw2_ref, x1_ref, h_ref, aff_ref):
    tm = x_ref.shape[0]
    x1 = (x_ref[...]
          + jnp.dot(a_ref[...], w_ref[:D_ATTN, :], preferred_element_type=F32)
          + jnp.dot(s_ref[...], w_ref[D_ATTN:, :], preferred_element_type=F32))
    x1_ref[...] = x1
    hn = _rms(x1, n2_ref[...])
    for s in range(TOKEN_ROWS):
        h_ref[pl.ds(s, tm, stride=TOKEN_ROWS), :] = hn[:, s * LANES:(s + 1) * LANES]
    hn_hi = hn.astype(BF16)
    hn_lo = (hn - hn_hi.astype(F32)).astype(BF16)
    l_hi = jnp.dot(hn_hi, rw2_ref[...], preferred_element_type=F32)
    l_lo = jnp.dot(hn_lo, rw2_ref[:, :N_EXPERTS], preferred_element_type=F32)
    logits = l_hi[:, :N_EXPERTS] + l_hi[:, N_EXPERTS:] + l_lo
    e = jnp.exp(logits - jnp.max(logits, axis=-1, keepdims=True))
    aff_ref[...] = e / jnp.sum(e, axis=-1, keepdims=True)


def _outproj(x2, attn2, ssm2, w_out, norm2_w, router_w, tm):
    T = x2.shape[0]
    row = lambda w: pl.BlockSpec((tm, w), lambda i: (i, 0))
    full = lambda a: pl.BlockSpec(a.shape, lambda i: (0,) * a.ndim)
    return pl.pallas_call(
        _outproj_kernel,
        grid=(T // tm,),
        in_specs=[row(D_MODEL), row(D_ATTN), row(D_SSM), full(w_out), full(norm2_w), full(router_w)],
        out_specs=[row(D_MODEL), pl.BlockSpec((tm * TOKEN_ROWS, LANES), lambda i: (i, 0)), row(N_EXPERTS)],
        out_shape=[jax.ShapeDtypeStruct((T, D_MODEL), F32),
                   jax.ShapeDtypeStruct((T * TOKEN_ROWS, LANES), F32),
                   jax.ShapeDtypeStruct((T, N_EXPERTS), F32)],
        compiler_params=_cparams(1, VMEM_LIMIT),
        name="outproj_router",
    )(x2, attn2, ssm2, w_out, norm2_w, router_w)


def _count(m):
    c = jnp.sum(jnp.where(m, 1.0, 0.0), axis=0, keepdims=True)
    return jnp.sum(c, axis=1, keepdims=True)


def _tri(n, m, fn):
    return jnp.where(fn(lax.broadcasted_iota(I32, (n, m), 0), lax.broadcasted_iota(I32, (n, m), 1)),
                     1.0, 0.0).astype(BF16)


def _dot_u16(lhs01, x):
    hi = jnp.floor(x * (1.0 / 256.0))
    lo = x - hi * 256.0
    return (jnp.dot(lhs01, hi.astype(BF16), preferred_element_type=F32) * 256.0
            + jnp.dot(lhs01, lo.astype(BF16), preferred_element_type=F32))


def _cumsum_rowmajor(x):
    R = x.shape[0]
    within = jnp.dot(x.astype(BF16), _tri(LANES, LANES, lambda k, l: k <= l), preferred_element_type=F32)
    rowtot = jnp.broadcast_to(within[:, LANES - 1:LANES], (R, LANES))
    before = _dot_u16(_tri(R, R, lambda i, k: k < i), rowtot)
    return within + before


def _route_kernel(a_ref, idx_ref, dst_ref, g_ref, cs_ref, ce_ref, sel_s, cnt_s, cs_s, rank_s, *, cap):
    s = pl.program_id(0)
    R = a_ref.shape[0]

    @pl.when(s < N_EXPERTS)
    def _select():
        bits = pltpu.bitcast(a_ref[...], I32)
        capf = jnp.float32(cap)

        def body(k, tau):
            cand = tau | lax.shift_left(jnp.int32(1), 30 - k)
            return jnp.where(_count(bits >= cand) >= capf, cand, tau)

        tau = lax.fori_loop(0, 31, body, jnp.zeros((1, 1), I32))
        gt = bits > tau
        eq = bits == tau
        need = capf - _count(gt)
        ties = _cumsum_rowmajor(jnp.where(eq, 1.0, 0.0))
        sel = jnp.where(gt | (eq & (ties <= need)), 1.0, 0.0)
        sel_s[s] = sel

        @pl.when(s == 0)
        def _():
            cnt_s[...] = sel

        @pl.when(s > 0)
        def _():
            cnt_s[...] += sel

    @pl.when(s == N_EXPERTS)
    def _prefix():
        cnt = cnt_s[...]
        ce = _cumsum_rowmajor(cnt)
        cs_s[...] = ce - cnt
        rank_s[...] = jnp.zeros_like(rank_s)
        cs_ref[...] = ce - cnt
        ce_ref[...] = ce

    @pl.when(s >= N_EXPERTS)
    def _invert():
        e = s - N_EXPERTS
        sel = sel_s[e]
        rank = rank_s[...]
        q = cs_s[...] + rank
        rank_s[...] = rank + sel
        a = a_ref[...]
        within = jnp.dot(sel.astype(BF16), _tri(LANES, LANES, lambda k, l: k <= l), preferred_element_type=F32)
        n_b = jnp.broadcast_to(within[:, LANES - 1:LANES], (R, LANES))
        rowend = jnp.dot(_tri(R, R, lambda i, k: k <= i), n_b.astype(BF16), preferred_element_type=F32)
        slot = lax.broadcasted_iota(I32, (R, cap), 1).astype(F32)
        done = jnp.where(rowend[:, 0:1] <= slot, 1.0, 0.0)
        ones = jnp.ones((8, R), BF16)
        row_p = jnp.dot(ones, done.astype(BF16), preferred_element_type=F32)[0:1]
        start_p = jnp.dot(ones, (done * n_b[:, 0:1]).astype(BF16), preferred_element_type=F32)[0:1]
        onehot = jnp.where(lax.broadcasted_iota(I32, (R, cap), 0).astype(F32) == row_p, 1.0, 0.0).astype(BF16)
        w_t = jnp.dot(within.T.astype(BF16), onehot, preferred_element_type=F32)
        k_in_row = lax.broadcasted_iota(I32, (1, cap), 1).astype(F32) - start_p
        lane_p = jnp.sum(jnp.where(w_t <= k_in_row, 1.0, 0.0), axis=0, keepdims=True)
        pick = lax.broadcasted_iota(I32, (LANES, cap), 0).astype(F32) == lane_p

        def take(x):
            v = jnp.dot(x.T.astype(BF16), onehot, preferred_element_type=F32)
            return jnp.sum(jnp.where(pick, v, 0.0), axis=0, keepdims=True)

        q_hi = jnp.floor(q * (1.0 / 256.0))
        a_hi = a.astype(BF16).astype(F32)
        a_mid = (a - a_hi).astype(BF16).astype(F32)
        a_lo = a - a_hi - a_mid
        idx_ref[...] = (row_p * LANES + lane_p).astype(I32)
        dst_ref[...] = (take(q_hi) * 256.0 + take(q - q_hi * 256.0)).astype(I32)
        g_ref[...] = take(a_hi) + take(a_mid) + take(a_lo)


def _route(aff_t3, cap):
    E, R, _ = aff_t3.shape
    assert R <= 256 and E == N_EXPERTS
    slot_spec = pl.BlockSpec((None, 1, cap), lambda s: (jnp.maximum(s - N_EXPERTS, 0), 0, 0))
    tok_spec = pl.BlockSpec((R, LANES), lambda s: (0, 0))
    return pl.pallas_call(
        functools.partial(_route_kernel, cap=cap),
        grid=(2 * E,),
        in_specs=[pl.BlockSpec((None, R, LANES), lambda s: (s % N_EXPERTS, 0, 0))],
        out_specs=[slot_spec, slot_spec, slot_spec, tok_spec, tok_spec],
        out_shape=[jax.ShapeDtypeStruct((E, 1, cap), I32), jax.ShapeDtypeStruct((E, 1, cap), I32),
                   jax.ShapeDtypeStruct((E, 1, cap), F32),
                   jax.ShapeDtypeStruct((R, LANES), F32), jax.ShapeDtypeStruct((R, LANES), F32)],
        scratch_shapes=[pltpu.VMEM((E, R, LANES), F32), pltpu.VMEM((R, LANES), F32),
                        pltpu.VMEM((R, LANES), F32), pltpu.VMEM((R, LANES), F32)],
        compiler_params=_cparams(1, VMEM_LIMIT),
        name="route",
    )(aff_t3)


def _ffn_kernel(idx_ref, dst_ref, h_hbm, gt_ref, wg_ref, wu_ref, wd_ref, c_hbm,
                xs_stage, xsb, hid, o_stage, gsem, ssem, *, tm, cap, nf):
    e, m, f = pl.program_id(0), pl.program_id(1), pl.program_id(2)
    base = e * cap + m * tm
    tr = TOKEN_ROWS
    part = tm // nf
    tf = wg_ref.shape[1]
    tn = wd_ref.shape[1]
    is_first = (e == 0) & (m == 0)
    is_last = (e == pl.num_programs(0) - 1) & (m == pl.num_programs(1) - 1)

    def token_rows(t):
        return pl.ds(pl.multiple_of(t * tr, tr), tr)

    def gather_rows(tile_base, r0, n):
        def issue(r, carry):
            pltpu.make_async_copy(h_hbm.at[token_rows(idx_ref[tile_base + r0 + r]), :],
                                  xs_stage.at[token_rows(r0 + r), :], gsem).start()
            return carry

        lax.fori_loop(0, n, issue, 0, unroll=8)

    def scatter_rows(tile_base, r0, n):
        def issue(r, carry):
            pltpu.make_async_copy(o_stage.at[token_rows(r0 + r), :],
                                  c_hbm.at[token_rows(dst_ref[tile_base + r0 + r]), :], ssem).start()
            return carry

        lax.fori_loop(0, n, issue, 0, unroll=8)

    def scatter_done():
        pltpu.make_async_copy(o_stage, c_hbm.at[pl.ds(0, tm * tr), :], ssem).wait()

    def gather_done():
        pltpu.make_async_copy(h_hbm.at[pl.ds(0, tm * tr), :], xs_stage, gsem).wait()

    @pl.when(f == 0)
    def _stage_in():
        @pl.when(is_first)
        def _():
            gather_rows(base, 0, tm)
            o_stage[...] = jnp.zeros_like(o_stage)

        gather_done()
        for s in range(tr):
            xsb[:, s * LANES:(s + 1) * LANES] = xs_stage[pl.ds(s, tm, stride=tr), :].astype(BF16)

    nxt = jnp.where(is_last, base, base + tm)
    prv = jnp.maximum(base - tm, 0)

    @pl.when(f < nf)
    def _gate_up():
        xs = xsb[...]
        g = jnp.dot(xs, wg_ref[...], preferred_element_type=F32)
        u = jnp.dot(xs, wu_ref[...], preferred_element_type=F32)
        hid[f] = (_silu(g) * u).astype(BF16)
        for r in range(part):
            rr = f * part + r
            pltpu.make_async_copy(h_hbm.at[token_rows(idx_ref[nxt + rr]), :],
                                  xs_stage.at[token_rows(rr), :], gsem).start()
            pltpu.make_async_copy(o_stage.at[token_rows(rr), :],
                                  c_hbm.at[token_rows(dst_ref[prv + rr]), :], ssem).start()

    for n in range(nf):
        @pl.when(f == nf + n)
        def _down(n=n):
            if n == 0:
                scatter_done()

            out = None
            for k in range(nf):
                d = jnp.dot(hid[k], wd_ref[k * tf:(k + 1) * tf, :], preferred_element_type=F32)
                out = d if out is None else out + d
            for j in range(tm // LANES):
                rows = out[j * LANES:(j + 1) * LANES, :] * gt_ref[:, j:j + 1]
                for c in range(tn // LANES):
                    o_stage[pl.ds(j * LANES * tr + n * (tn // LANES) + c, LANES, stride=tr), :] = (
                        rows[:, c * LANES:(c + 1) * LANES])

            if n == nf - 1:
                @pl.when(is_last)
                def _():
                    scatter_rows(base, 0, tm)
                    scatter_done()
                    gather_done()


def _ffn(idx, dst, h_rows, g_t, w_gate, w_up, w_down, tm, tf):
    E, n_m = g_t.shape[0], g_t.shape[1]
    cap = n_m * tm
    n_contrib = E * cap
    nf = D_FF // tf
    tn = D_MODEL // nf
    up = lambda f: jnp.minimum(f, nf - 1)
    down = lambda f: jnp.maximum(f - nf, 0)
    grid_spec = pltpu.PrefetchScalarGridSpec(
        num_scalar_prefetch=2,
        grid=(E, n_m, 2 * nf),
        in_specs=[pl.BlockSpec(memory_space=pl.ANY),
                  pl.BlockSpec((None, None, LANES, tm // LANES), lambda e, m, f, i, d: (e, m, 0, 0)),
                  pl.BlockSpec((None, D_MODEL, tf), lambda e, m, f, i, d: (e, 0, up(f))),
                  pl.BlockSpec((None, D_MODEL, tf), lambda e, m, f, i, d: (e, 0, up(f))),
                  pl.BlockSpec((None, D_FF, tn), lambda e, m, f, i, d: (e, 0, down(f)))],
        out_specs=pl.BlockSpec(memory_space=pl.ANY),
        scratch_shapes=[pltpu.VMEM((tm * TOKEN_ROWS, LANES), F32), pltpu.VMEM((tm, D_MODEL), BF16),
                        pltpu.VMEM((nf, tm, tf), BF16), pltpu.VMEM((tm * TOKEN_ROWS, LANES), F32),
                        pltpu.SemaphoreType.DMA, pltpu.SemaphoreType.DMA],
    )
    return pl.pallas_call(
        functools.partial(_ffn_kernel, tm=tm, cap=cap, nf=nf),
        grid_spec=grid_spec,
        out_shape=jax.ShapeDtypeStruct((n_contrib * TOKEN_ROWS, LANES), F32),
        compiler_params=_cparams(3, FFN_VMEM_LIMIT),
        name="expert_ffn",
    )(idx, dst, h_rows, g_t, w_gate, w_up, w_down)


def _combine_kernel(ch_ref, tt_ref, flag_ref, x1_ref, cs_ref, ce_ref, c_ref, nw_ref, out_ref, acc_ref, *, cg, rows):
    k = pl.program_id(0)
    flag = flag_ref[k]

    @pl.when((flag & 2) > 0)
    def _():
        acc_ref[...] = x1_ref[...]

    @pl.when((flag & 1) > 0)
    def _():
        cio = (lax.broadcasted_iota(I32, (cg, LANES), 0) + ch_ref[k] * cg).astype(F32)
        parts = [jnp.where((cs_ref[r:r + 1, :] <= cio) & (cio < ce_ref[r:r + 1, :]), 1.0, 0.0) for r in range(rows)]
        a = jnp.concatenate(parts, axis=1).T.astype(BF16)
        chunk = jnp.concatenate([c_ref[pl.ds(s, cg, stride=TOKEN_ROWS), :].astype(BF16)
                                 for s in range(TOKEN_ROWS)], axis=1)
        acc_ref[...] += jnp.dot(a, chunk, preferred_element_type=F32)

    @pl.when((flag & 4) > 0)
    def _():
        out_ref[...] = _rms(acc_ref[...], nw_ref[...])


def _combine(ch, tt, flag, x1, cs3, ce3, contrib, final_w, tb, cg):
    T = x1.shape[0]
    rows = tb // LANES
    n_pairs = ch.shape[0]
    grid_spec = pltpu.PrefetchScalarGridSpec(
        num_scalar_prefetch=3,
        grid=(n_pairs,),
        in_specs=[pl.BlockSpec((tb, D_MODEL), lambda k, ch, tt, fl: (tt[k], 0)),
                  pl.BlockSpec((None, rows, LANES), lambda k, ch, tt, fl: (tt[k], 0, 0)),
                  pl.BlockSpec((None, rows, LANES), lambda k, ch, tt, fl: (tt[k], 0, 0)),
                  pl.BlockSpec((cg * TOKEN_ROWS, LANES), lambda k, ch, tt, fl: (ch[k], 0)),
                  pl.BlockSpec((1, D_MODEL), lambda k, ch, tt, fl: (0, 0))],
        out_specs=pl.BlockSpec((tb, D_MODEL), lambda k, ch, tt, fl: (tt[k], 0)),
        scratch_shapes=[pltpu.VMEM((tb, D_MODEL), F32)],
    )
    return pl.pallas_call(
        functools.partial(_combine_kernel, cg=cg, rows=rows),
        grid_spec=grid_spec,
        out_shape=jax.ShapeDtypeStruct((T, D_MODEL), F32),
        compiler_params=_cparams(1, VMEM_LIMIT),
        name="combine_final",
    )(ch, tt, flag, x1, cs3, ce3, contrib, final_w)


def _combine_schedule(cs, ce, tb, cg):
    T = cs.size
    ntt = T // tb
    total = CAPACITY_FACTOR * T
    nch = total // cg
    lo = cs.reshape(ntt, tb)[:, 0].astype(I32)
    hi = ce.reshape(ntt, tb)[:, -1].astype(I32)
    first = jnp.minimum(lo, total - 1) // cg
    last = jnp.where(hi > lo, (hi - 1) // cg, first)
    n = last - first + 1
    ends = jnp.cumsum(n)
    offs = ends - n
    n_pairs = ntt + nch
    ks = jnp.arange(n_pairs, dtype=I32)
    tile = jnp.minimum(jnp.searchsorted(ends, ks, side="right").astype(I32), ntt - 1)
    valid = ks < ends[-1]
    chunk = jnp.where(valid, first[tile] + ks - offs[tile], last[-1])
    flag = jnp.where(valid, 1 + 2 * (ks == offs[tile]) + 4 * (ks == ends[tile] - 1), 0).astype(I32)
    return chunk.astype(I32), tile, flag


def _t5_buckets(rel):
    nb = NUM_BUCKETS // 2
    ret = (rel > 0).astype(np.int32) * nb
    n = np.abs(rel)
    max_exact = nb // 2
    large = max_exact + (np.log(np.maximum(n, 1) / max_exact) / np.log(MAX_DISTANCE / max_exact)
                         * (nb - max_exact)).astype(np.int32)
    large = np.minimum(large, nb - 1)
    return ret + np.where(n < max_exact, n, large)


def _tile(n, pref):
    t = min(n, pref)
    assert n % t == 0
    return t


def _trunk(x, w):
    B, S, _ = x.shape
    T = B * S
    cap = CAPACITY_FACTOR * T // N_EXPERTS
    x2 = x.reshape(T, D_MODEL)
    tm = _tile(T, 512)

    qkv, main = _inproj(x2, w["norm1_w"], w["w_qkv"], w["w_main"], _tile(T, 1024))
    main3 = main.reshape(B, S, MAIN_COLS)
    attn = _attention(qkv.reshape(B, S, QKV_COLS), w["bias_tab"], w["attn_sink"], w["attn_norm_w"])
    yf = _ssd_pass(main3, w["conv_w"], w["conv_b"], w["dtb"], w["alog"], reverse=False)
    ssm = _ssd_pass(main3, w["conv_w"], w["conv_b"], w["dtb"], w["alog"], reverse=True,
                    yf=yf, dskip=w["dskip"], norm_w=w["ssm_norm_w"])
    x1, h_rows, aff = _outproj(x2, attn.reshape(T, D_ATTN), ssm.reshape(T, D_SSM), w["w_out"], w["norm2_w"],
                          w["router_w2"], tm)

    tb = _tile(T, 256)
    cg = 256
    idx, dst, g, cs, ce = _route(aff.T.reshape(N_EXPERTS, T // LANES, LANES), cap)
    tmf = _tile(cap, 1024)
    g_t = jnp.swapaxes(g.reshape(N_EXPERTS, cap // tmf, tmf // LANES, LANES), 2, 3)
    contrib = _ffn(idx.reshape(-1), dst.reshape(-1), h_rows, g_t, w["w_gate"], w["w_up"], w["w_down"], tmf, 512)
    ch, tt, flag = _combine_schedule(cs, ce, tb, cg)
    tile3 = lambda a: a.reshape(T // tb, tb // LANES, LANES)
    y = _combine(ch, tt, flag, x1, tile3(cs), tile3(ce), contrib, w["final_norm_w"], tb, cg)
    return y.reshape(B, S, D_MODEL)


def _prep_weights(rel_bias, norm1_w, w_in, conv_w, conv_b, dt_bias_fwd, dt_bias_bwd, a_log_fwd, a_log_bwd,
                  d_skip, ssm_norm_w, attn_sink, attn_norm_w, w_out, norm2_w, router_w, w_gate, w_up, w_down,
                  final_norm_w):
    o1 = D_ATTN; o2 = o1 + D_KV; o3 = o2 + D_KV; o4 = o3 + D_SSM; o5 = o4 + D_SSM + 2 * D_BC
    wi = w_in[0]
    scale = 1.0 / math.sqrt(HEAD_DIM)
    w_qkv = jnp.concatenate([wi[:, :o1] * scale, wi[:, o1:o3]], axis=1).astype(BF16)
    w_main = jnp.concatenate([wi[:, o3:], jnp.zeros((D_MODEL, LANES - 2 * SSM_HEADS), F32)], axis=1).astype(BF16)
    rel = np.arange(3 * BLOCK)[None, :] - BLOCK - np.arange(BLOCK)[:, None]
    onehot = (jnp.asarray(_t5_buckets(rel), I32)[..., None] == jnp.arange(NUM_BUCKETS, dtype=I32)).astype(F32)
    bias_tab = jnp.einsum("qsn,nh->hqs", onehot, rel_bias.astype(F32), precision=lax.Precision.HIGHEST)
    bias_tab = jnp.where(jnp.asarray(np.abs(rel) <= WINDOW)[None], bias_tab, NEG_INF)
    rw = router_w[0].astype(F32)
    rw_hi = rw.astype(BF16)
    router_w2 = jnp.concatenate([rw_hi, (rw - rw_hi.astype(F32)).astype(BF16)], axis=1)
    pad = jnp.zeros((LANES - 2 * SSM_HEADS,), F32)
    row = lambda a: a.reshape(1, -1).astype(F32)
    return dict(
        norm1_w=row(norm1_w[0]), w_qkv=w_qkv, w_main=w_main, bias_tab=bias_tab, attn_sink=attn_sink[0].astype(F32),
        attn_norm_w=row(attn_norm_w[0]), conv_w=conv_w[0].T.astype(F32), conv_b=row(conv_b[0]),
        dtb=row(jnp.concatenate([dt_bias_fwd[0], dt_bias_bwd[0], pad])),
        alog=row(jnp.concatenate([a_log_fwd[0], a_log_bwd[0], pad])),
        dskip=row(jnp.repeat(d_skip[0], SSM_HEAD_DIM)), ssm_norm_w=row(ssm_norm_w[0]),
        w_out=w_out[0].astype(BF16), norm2_w=row(norm2_w[0]), router_w2=router_w2,
        w_gate=w_gate[0].astype(BF16), w_up=w_up[0].astype(BF16), w_down=w_down[0].astype(BF16),
        final_norm_w=row(final_norm_w))


def kernel(x_prompt, x_sample, rel_bias, norm1_w, w_in, conv_w, conv_b, dt_bias_fwd, dt_bias_bwd, a_log_fwd, a_log_bwd, d_skip, ssm_norm_w, attn_sink, attn_norm_w, w_out, norm2_w, router_w, w_gate, w_up, w_down, final_norm_w):
    assert norm1_w.shape[0] == 1
    w = _prep_weights(rel_bias, norm1_w, w_in, conv_w, conv_b, dt_bias_fwd, dt_bias_bwd, a_log_fwd, a_log_bwd,
                      d_skip, ssm_norm_w, attn_sink, attn_norm_w, w_out, norm2_w, router_w, w_gate, w_up,
                      w_down, final_norm_w)
    return (_trunk(x_prompt, w), _trunk(x_sample, w))
```

```python
import functools
import math

import numpy as np
import jax
import jax.numpy as jnp
from jax import lax
from jax.experimental import pallas as pl
from jax.experimental.pallas import tpu as pltpu

F32 = jnp.float32
BF16 = jnp.bfloat16
I32 = jnp.int32

D_MODEL = 2048
HEAD_DIM = 64
N_HEADS = 16
N_KV_HEADS = 4
D_ATTN = 1024
D_KV = 256
WINDOW = 128
BLOCK = 128
NUM_BUCKETS = 32
MAX_DISTANCE = 128
SSM_HEAD_DIM = 64
SSM_HEADS = 16
D_SSM = 1024
SSM_STATE = 128
SSM_GROUPS = 2
HEADS_PER_GROUP = SSM_HEADS // SSM_GROUPS
D_BC = 256
CHUNK = 128
N_EXPERTS = 16
CAPACITY_FACTOR = 2
D_FF = 2048
EPS = 1e-6
NEG_INF = -1e30

LANES = 128
HALO_ROWS = 8
TOKEN_ROWS = D_MODEL // LANES

QKV_COLS = D_ATTN + 2 * D_KV
COL_QK, COL_QV = D_ATTN, D_ATTN + D_KV
COL_Z, COL_X, COL_B, COL_DT = 0, 1024, 2048, 2560
MAIN_COLS = COL_DT + LANES
MAIN_TN = 896

ATTN_HEAD_ORDER = tuple(8 * t + 4 * par + i for t in range(2) for i in range(4) for par in range(2))

SSD_BATCH_BLOCK = 4
ATTN_BATCH_BLOCK = 4

VMEM_LIMIT = 56 * 1024 * 1024
FFN_VMEM_LIMIT = 60 * 1024 * 1024


def _cparams(n_axes, vmem=None):
    return pltpu.CompilerParams(dimension_semantics=("arbitrary",) * n_axes,
                                vmem_limit_bytes=vmem)


def _rms(x, w):
    return x * lax.rsqrt(jnp.mean(x * x, axis=-1, keepdims=True) + EPS) * w


def _silu(x):
    return x * jax.nn.sigmoid(x)


def _inproj_kernel(x_ref, nw_ref, wq_ref, wr_ref, oq_ref, om_ref, hn_ref):
    j = pl.program_id(1)

    @pl.when(j == 0)
    def _():
        hn = _rms(x_ref[...], nw_ref[...]).astype(BF16)
        hn_ref[...] = hn
        oq_ref[...] = jnp.dot(hn, wq_ref[...], preferred_element_type=F32).astype(BF16)

    @pl.when(j > 0)
    def _():
        om_ref[...] = jnp.dot(hn_ref[...], wr_ref[...], preferred_element_type=F32)


def _inproj(x2, norm_w, w_qkv, w_main, tm):
    T = x2.shape[0]
    rest = lambda j: jnp.maximum(j - 1, 0)
    return pl.pallas_call(
        _inproj_kernel,
        grid=(T // tm, 1 + MAIN_COLS // MAIN_TN),
        in_specs=[pl.BlockSpec((tm, D_MODEL), lambda i, j: (i, 0)),
                  pl.BlockSpec((1, D_MODEL), lambda i, j: (0, 0)),
                  pl.BlockSpec((D_MODEL, QKV_COLS), lambda i, j: (0, 0), pipeline_mode=pl.Buffered(1)),
                  pl.BlockSpec((D_MODEL, MAIN_TN), lambda i, j: (0, rest(j)))],
        out_specs=[pl.BlockSpec((tm, QKV_COLS), lambda i, j: (i, 0)),
                   pl.BlockSpec((tm, MAIN_TN), lambda i, j: (i, rest(j)))],
        out_shape=[jax.ShapeDtypeStruct((T, QKV_COLS), BF16), jax.ShapeDtypeStruct((T, MAIN_COLS), F32)],
        scratch_shapes=[pltpu.VMEM((tm, D_MODEL), BF16)],
        compiler_params=_cparams(2, VMEM_LIMIT),
        name="inproj",
    )(x2, norm_w, w_qkv, w_main)


def _attn_kernel(q_ref, kp_ref, ko_ref, kn_ref, vp_ref, vo_ref, vn_ref, bias_ref, sink_ref, nw_ref,
                 o_ref, *, nb):
    i = pl.program_id(1)
    lane = lax.broadcasted_iota(I32, (3 * BLOCK, LANES), 1)
    m_lo = jnp.where(lane < HEAD_DIM, 1.0, 0.0).astype(BF16)
    m_up = jnp.where(lane < HEAD_DIM, 0.0, 1.0).astype(BF16)
    low_q = lax.broadcasted_iota(I32, (BLOCK, LANES), 1) < HEAD_DIM
    top = lax.broadcasted_iota(I32, (2 * BLOCK, 1), 0) < BLOCK
    tiles_per_pair = D_ATTN // LANES // (D_KV // LANES)

    def body(bb, edge):
        tiles = []
        for t in range(D_KV // LANES):
            sl = slice(t * LANES, (t + 1) * LANES)
            kt = jnp.concatenate([kp_ref[bb, :, sl], ko_ref[bb, :, sl], kn_ref[bb, :, sl]], axis=0)
            vt = jnp.concatenate([vp_ref[bb, :, sl], vo_ref[bb, :, sl], vn_ref[bb, :, sl]], axis=0)
            rhs = [(kt * m_lo, vt * m_lo + m_up), (kt * m_up, vt * m_up + m_lo)]
            for j in range(0, tiles_per_pair, 2):
                qa, qb = t * tiles_per_pair + j, t * tiles_per_pair + j + 1
                q2 = jnp.concatenate([q_ref[bb, :, qa * LANES:(qa + 1) * LANES],
                                      q_ref[bb, :, qb * LANES:(qb + 1) * LANES]], axis=0)
                res = []
                for par in range(2):
                    kz, vz = rhs[par]
                    ha, hb = ATTN_HEAD_ORDER[2 * qa + par], ATTN_HEAD_ORDER[2 * qb + par]
                    s = lax.dot_general(q2, kz, (((1,), (1,)), ((), ())), preferred_element_type=F32)
                    s = s + jnp.concatenate([bias_ref[ha], bias_ref[hb]], axis=0) + edge
                    sk = jnp.where(top, sink_ref[ha], sink_ref[hb])
                    m = jnp.maximum(jnp.max(s, axis=-1, keepdims=True), sk)
                    p = jnp.exp(s - m).astype(BF16)
                    pv = jnp.dot(p, vz, preferred_element_type=F32)
                    den = (pv[:, LANES - 1:LANES] if par == 0 else pv[:, 0:1]) + jnp.exp(sk - m)
                    res.append(pv * (1.0 / den))
                tiles.append(jnp.where(low_q, res[0][:BLOCK], res[1][:BLOCK]))
                tiles.append(jnp.where(low_q, res[0][BLOCK:], res[1][BLOCK:]))
        o = jnp.concatenate(tiles, axis=-1)
        o_ref[bb] = _rms(o, nw_ref[...]).astype(BF16)

    kcol = lax.broadcasted_iota(I32, (1, 3 * BLOCK), 1)
    edge = (jnp.where((kcol < BLOCK) & (i == 0), NEG_INF, 0.0)
            + jnp.where((kcol >= 2 * BLOCK) & (i == nb - 1), NEG_INF, 0.0))
    for bb in range(q_ref.shape[0]):
        body(bb, edge)


def _attention(qkv3, bias_tab, sink, norm_w):
    B, S, _ = qkv3.shape
    nb = S // BLOCK
    kcol, vcol = COL_QK // D_KV, COL_QV // D_KV
    prev = lambda i: jnp.maximum(i - 1, 0)
    nxt = lambda i: jnp.minimum(i + 1, nb - 1)
    nbt = ATTN_BATCH_BLOCK if B % ATTN_BATCH_BLOCK == 0 else 1
    kv = lambda col, f: pl.BlockSpec((nbt, BLOCK, D_KV), lambda b, i: (b, f(i), col))
    same = lambda i: i
    return pl.pallas_call(
        functools.partial(_attn_kernel, nb=nb),
        grid=(B // nbt, nb),
        in_specs=[pl.BlockSpec((nbt, BLOCK, D_ATTN), lambda b, i: (b, i, 0)),
                  kv(kcol, prev), kv(kcol, same), kv(kcol, nxt),
                  kv(vcol, prev), kv(vcol, same), kv(vcol, nxt),
                  pl.BlockSpec((N_HEADS, BLOCK, 3 * BLOCK), lambda b, i: (0, 0, 0)),
                  pl.BlockSpec(memory_space=pltpu.SMEM),
                  pl.BlockSpec((1, D_ATTN), lambda b, i: (0, 0))],
        out_specs=pl.BlockSpec((nbt, BLOCK, D_ATTN), lambda b, i: (b, i, 0)),
        out_shape=jax.ShapeDtypeStruct((B, S, D_ATTN), BF16),
        compiler_params=_cparams(2),
        name="attention",
    )(qkv3, qkv3, qkv3, qkv3, qkv3, qkv3, qkv3, bias_tab, sink, norm_w)


def _conv_silu(u, prev8, next8, w, b, has_prev, has_next):
    prev_row = jnp.where(has_prev, prev8[HALO_ROWS - 1:HALO_ROWS, :], 0.0)
    next_row = jnp.where(has_next, next8[0:1, :], 0.0)
    rid = lax.broadcasted_iota(I32, u.shape, 0)
    um = jnp.where(rid == 0, prev_row, pltpu.roll(u, 1, 0))
    up = jnp.where(rid == CHUNK - 1, next_row, pltpu.roll(u, CHUNK - 1, 0))
    y = b + um * w[0:1, :] + u * w[1:2, :] + up * w[2:3, :]
    return _silu(y)


def _ssd_kernel(*refs, reverse, nc):
    h_ref = refs[-1]

    @pl.when(pl.program_id(1) == 0)
    def _():
        h_ref[...] = jnp.zeros_like(h_ref)

    for bb in range(h_ref.shape[0]):
        _ssd_chunk(bb, refs, reverse, nc)


def _ssd_chunk(bb, refs, reverse, nc):
    if reverse:
        (z_ref, yf_ref, x_ref, xp_ref, xn_ref, bc_ref, bcp_ref, bcn_ref, dt_ref, cw_ref, cb_ref, dtb_ref,
         alog_ref, dskip_ref, nw_ref, o_ref, h_ref) = refs
    else:
        (x_ref, xp_ref, xn_ref, bc_ref, bcp_ref, bcn_ref, dt_ref, cw_ref, cb_ref, dtb_ref,
         alog_ref, o_ref, h_ref) = refs
    step = pl.program_id(1)
    c = (nc - 1 - step) if reverse else step
    has_prev, has_next = c > 0, c < nc - 1
    cw, cb = cw_ref[...], cb_ref[...]
    xc = _conv_silu(x_ref[bb], xp_ref[bb], xn_ref[bb], cw[:, :D_SSM], cb[:, :D_SSM], has_prev, has_next)
    bcv = _conv_silu(bc_ref[bb], bcp_ref[bb], bcn_ref[bb], cw[:, D_SSM:], cb[:, D_SSM:], has_prev, has_next)

    raw = dt_ref[bb] + dtb_ref[...]
    dt = jnp.maximum(raw, 0.0) + jnp.log1p(jnp.exp(-jnp.abs(raw)))
    a = dt * (-jnp.exp(alog_ref[...]))
    li = lax.broadcasted_iota(I32, (CHUNK, CHUNK), 0)
    si = lax.broadcasted_iota(I32, (CHUNK, CHUNK), 1)
    incl = jnp.dot(jnp.where(li >= si, 1.0, 0.0).astype(F32), a, precision=lax.Precision.HIGHEST,
                   preferred_element_type=F32)
    tot = incl[CHUNK - 1:CHUNK, :]
    if reverse:
        pcs = incl - a
        dstate = jnp.exp(pcs)
        yscale = jnp.exp(tot - pcs)
        mask = si >= li
    else:
        pcs = incl
        dstate = jnp.exp(tot - pcs)
        yscale = jnp.exp(pcs)
        mask = li >= si
    pcs_t = pcs.T
    lane0 = SSM_HEADS if reverse else 0
    P = SSM_HEAD_DIM
    gw = HEADS_PER_GROUP * P

    def spread(x, width, pieces):
        k = lax.broadcasted_iota(I32, (LANES, SSM_HEADS * width), 0)
        c = lax.broadcasted_iota(I32, (LANES, SSM_HEADS * width), 1)
        sel = jnp.where(k == lane0 + c // width, 1.0, 0.0).astype(BF16)
        out, rem = None, x
        for _ in range(pieces):
            piece = rem.astype(BF16)
            rem = rem - piece.astype(F32)
            d = jnp.dot(piece, sel, preferred_element_type=F32)
            out = d if out is None else out + d
        return out

    scales = spread(jnp.concatenate([dt, dstate, yscale, jnp.broadcast_to(jnp.exp(tot), (HALO_ROWS, LANES))], axis=0),
                    P, 2)
    dt_e, ds_e, ys_e = scales[:CHUNK], scales[CHUNK:2 * CHUNK], scales[2 * CHUNK:3 * CHUNK]
    cdec_e = scales[3 * CHUNK:3 * CHUNK + 1]
    col_e = spread(pcs, CHUNK, 3)
    xdt = xc * dt_e
    xdt_b = xdt.astype(BF16)
    xs_b = (xdt * ds_e).astype(BF16)
    low = lax.broadcasted_iota(I32, (CHUNK, LANES), 1) < P

    y_groups = []
    for g in range(SSM_GROUPS):
        bg = bcv[:, g * SSM_STATE:(g + 1) * SSM_STATE]
        cg = bcv[:, D_BC + g * SSM_STATE:D_BC + (g + 1) * SSM_STATE].astype(BF16)
        cbm = lax.dot_general(cg, bg.astype(BF16), (((1,), (1,)), ((), ())), preferred_element_type=F32)
        hg = h_ref[bb, g]
        yoff = jnp.dot(cg, hg.astype(BF16), preferred_element_type=F32)
        yd = []
        for jp in range(HEADS_PER_GROUP // 2):
            tile = (g * HEADS_PER_GROUP) // 2 + jp
            xpair = xdt_b[:, tile * LANES:(tile + 1) * LANES]
            halves = []
            for par in range(2):
                j = 2 * tile + par
                ln = lane0 + j
                col = col_e[:, j * CHUNK:(j + 1) * CHUNK]
                row = pcs_t[ln:ln + 1, :]
                seg = (row - col) if reverse else (col - row)
                lm = jnp.exp(jnp.where(mask, seg, -jnp.inf))
                halves.append(jnp.dot((cbm * lm).astype(BF16), xpair, preferred_element_type=F32))
            yd.append(jnp.where(low, halves[0], halves[1]))
        sl = slice(g * gw, (g + 1) * gw)
        y_groups.append(jnp.concatenate(yd, axis=1) + yoff * ys_e[:, sl])
        snew = jnp.dot(bg.T.astype(BF16), xs_b[:, sl], preferred_element_type=F32)
        h_ref[bb, g] = hg * cdec_e[:, sl] + snew
    y = jnp.concatenate(y_groups, axis=1)

    if not reverse:
        o_ref[bb] = y
    else:
        ytot = yf_ref[bb] + y + dskip_ref[...] * xc
        yz = ytot * _silu(z_ref[bb])
        halves = []
        for g in range(SSM_GROUPS):
            seg = yz[:, g * gw:(g + 1) * gw]
            halves.append(seg * lax.rsqrt(jnp.mean(seg * seg, axis=-1, keepdims=True) + EPS))
        o_ref[bb] = (jnp.concatenate(halves, axis=1) * nw_ref[...]).astype(BF16)


def _ssd_pass(proj3, conv_w3, conv_b, dtb, alog, reverse, yf=None, dskip=None, norm_w=None):
    B, S, _ = proj3.shape
    nc = S // CHUNK
    hp = CHUNK // HALO_ROWS
    nh = S // HALO_ROWS
    ch = (lambda s: nc - 1 - s) if reverse else (lambda s: s)
    nbt = SSD_BATCH_BLOCK if B % SSD_BATCH_BLOCK == 0 else 1
    main = lambda w, col: pl.BlockSpec((nbt, CHUNK, w), lambda b, s: (b, ch(s), col // w))
    halo_p = lambda w, col: pl.BlockSpec((nbt, HALO_ROWS, w),
                                         lambda b, s: (b, jnp.maximum(ch(s) * hp - 1, 0), col // w))
    halo_n = lambda w, col: pl.BlockSpec((nbt, HALO_ROWS, w),
                                         lambda b, s: (b, jnp.minimum((ch(s) + 1) * hp, nh - 1), col // w))
    full = lambda a: pl.BlockSpec(a.shape, lambda b, s: (0,) * a.ndim)
    d_bc2 = 2 * D_BC
    in_specs = [main(D_SSM, COL_X), halo_p(D_SSM, COL_X), halo_n(D_SSM, COL_X),
                main(d_bc2, COL_B), halo_p(d_bc2, COL_B), halo_n(d_bc2, COL_B),
                main(LANES, COL_DT), full(conv_w3), full(conv_b), full(dtb), full(alog)]
    args = [proj3, proj3, proj3, proj3, proj3, proj3, proj3, conv_w3, conv_b, dtb, alog]
    if reverse:
        in_specs = [main(D_SSM, COL_Z), pl.BlockSpec((nbt, CHUNK, D_SSM), lambda b, s: (b, ch(s), 0))] + in_specs
        in_specs += [full(dskip), full(norm_w)]
        args = [proj3, yf] + args + [dskip, norm_w]
    return pl.pallas_call(
        functools.partial(_ssd_kernel, reverse=reverse, nc=nc),
        grid=(B // nbt, nc),
        in_specs=in_specs,
        out_specs=pl.BlockSpec((nbt, CHUNK, D_SSM), lambda b, s: (b, ch(s), 0)),
        out_shape=jax.ShapeDtypeStruct((B, S, D_SSM), BF16 if reverse else F32),
        scratch_shapes=[pltpu.VMEM((nbt, SSM_GROUPS, SSM_STATE, HEADS_PER_GROUP * SSM_HEAD_DIM), F32)],
        compiler_params=_cparams(2),
        name="ssd_bwd" if reverse else "ssd_fwd",
    )(*args)


def _outproj_kernel(x_ref, a_ref, s_ref, w_ref, n2_ref, rw2_ref, x1_ref, h_ref, aff_ref):
    tm = x_ref.shape[0]
    x1 = (x_ref[...]
          + jnp.dot(a_ref[...], w_ref[:D_ATTN, :], preferred_element_type=F32)
          + jnp.dot(s_ref[...], w_ref[D_ATTN:, :], preferred_element_type=F32))
    x1_ref[...] = x1
    hn = _rms(x1, n2_ref[...])
    for s in range(TOKEN_ROWS):
        h_ref[pl.ds(s, tm, stride=TOKEN_ROWS), :] = hn[:, s * LANES:(s + 1) * LANES]
    hn_hi = hn.astype(BF16)
    hn_lo = (hn - hn_hi.astype(F32)).astype(BF16)
    l_hi = jnp.dot(hn_hi, rw2_ref[...], preferred_element_type=F32)
    l_lo = jnp.dot(hn_lo, rw2_ref[:, :N_EXPERTS], preferred_element_type=F32)
    logits = l_hi[:, :N_EXPERTS] + l_hi[:, N_EXPERTS:] + l_lo
    e = jnp.exp(logits - jnp.max(logits, axis=-1, keepdims=True))
    aff_ref[...] = e / jnp.sum(e, axis=-1, keepdims=True)


def _outproj(x2, attn2, ssm2, w_out, norm2_w, router_w, tm):
    T = x2.shape[0]
    row = lambda w: pl.BlockSpec((tm, w), lambda i: (i, 0))
    full = lambda a: pl.BlockSpec(a.shape, lambda i: (0,) * a.ndim)
    return pl.pallas_call(
        _outproj_kernel,
        grid=(T // tm,),
        in_specs=[row(D_MODEL), row(D_ATTN), row(D_SSM), full(w_out), full(norm2_w), full(router_w)],
        out_specs=[row(D_MODEL), pl.BlockSpec((tm * TOKEN_ROWS, LANES), lambda i: (i, 0)), row(N_EXPERTS)],
        out_shape=[jax.ShapeDtypeStruct((T, D_MODEL), F32),
                   jax.ShapeDtypeStruct((T * TOKEN_ROWS, LANES), F32),
                   jax.ShapeDtypeStruct((T, N_EXPERTS), F32)],
        compiler_params=_cparams(1, VMEM_LIMIT),
        name="outproj_router",
    )(x2, attn2, ssm2, w_out, norm2_w, router_w)


def _count(m):
    c = jnp.sum(jnp.where(m, 1.0, 0.0), axis=0, keepdims=True)
    return jnp.sum(c, axis=1, keepdims=True)


def _tri(n, m, fn):
    return jnp.where(fn(lax.broadcasted_iota(I32, (n, m), 0), lax.broadcasted_iota(I32, (n, m), 1)),
                     1.0, 0.0).astype(BF16)


def _dot_u16(lhs01, x):
    hi = jnp.floor(x * (1.0 / 256.0))
    lo = x - hi * 256.0
    return (jnp.dot(lhs01, hi.astype(BF16), preferred_element_type=F32) * 256.0
            + jnp.dot(lhs01, lo.astype(BF16), preferred_element_type=F32))


def _cumsum_rowmajor(x):
    R = x.shape[0]
    within = jnp.dot(x.astype(BF16), _tri(LANES, LANES, lambda k, l: k <= l), preferred_element_type=F32)
    rowtot = jnp.broadcast_to(within[:, LANES - 1:LANES], (R, LANES))
    before = _dot_u16(_tri(R, R, lambda i, k: k < i), rowtot)
    return within + before


def _route_kernel(a_ref, idx_ref, dst_ref, g_ref, cs_ref, ce_ref, sel_s, cnt_s, cs_s, rank_s, *, cap):
    s = pl.program_id(0)
    R = a_ref.shape[0]

    @pl.when(s < N_EXPERTS)
    def _select():
        bits = pltpu.bitcast(a_ref[...], I32)
        capf = jnp.float32(cap)

        def body(k, tau):
            cand = tau | lax.shift_left(jnp.int32(1), 30 - k)
            return jnp.where(_count(bits >= cand) >= capf, cand, tau)

        tau = lax.fori_loop(0, 31, body, jnp.zeros((1, 1), I32))
        gt = bits > tau
        eq = bits == tau
        need = capf - _count(gt)
        ties = _cumsum_rowmajor(jnp.where(eq, 1.0, 0.0))
        sel = jnp.where(gt | (eq & (ties <= need)), 1.0, 0.0)
        sel_s[s] = sel

        @pl.when(s == 0)
        def _():
            cnt_s[...] = sel

        @pl.when(s > 0)
        def _():
            cnt_s[...] += sel

    @pl.when(s == N_EXPERTS)
    def _prefix():
        cnt = cnt_s[...]
        ce = _cumsum_rowmajor(cnt)
        cs_s[...] = ce - cnt
        rank_s[...] = jnp.zeros_like(rank_s)
        cs_ref[...] = ce - cnt
        ce_ref[...] = ce

    @pl.when(s >= N_EXPERTS)
    def _invert():
        e = s - N_EXPERTS
        sel = sel_s[e]
        rank = rank_s[...]
        q = cs_s[...] + rank
        rank_s[...] = rank + sel
        a = a_ref[...]
        within = jnp.dot(sel.astype(BF16), _tri(LANES, LANES, lambda k, l: k <= l), preferred_element_type=F32)
        n_b = jnp.broadcast_to(within[:, LANES - 1:LANES], (R, LANES))
        rowend = jnp.dot(_tri(R, R, lambda i, k: k <= i), n_b.astype(BF16), preferred_element_type=F32)
        slot = lax.broadcasted_iota(I32, (R, cap), 1).astype(F32)
        done = jnp.where(rowend[:, 0:1] <= slot, 1.0, 0.0)
        ones = jnp.ones((8, R), BF16)
        row_p = jnp.dot(ones, done.astype(BF16), preferred_element_type=F32)[0:1]
        start_p = jnp.dot(ones, (done * n_b[:, 0:1]).astype(BF16), preferred_element_type=F32)[0:1]
        onehot = jnp.where(lax.broadcasted_iota(I32, (R, cap), 0).astype(F32) == row_p, 1.0, 0.0).astype(BF16)
        w_t = jnp.dot(within.T.astype(BF16), onehot, preferred_element_type=F32)
        k_in_row = lax.broadcasted_iota(I32, (1, cap), 1).astype(F32) - start_p
        lane_p = jnp.sum(jnp.where(w_t <= k_in_row, 1.0, 0.0), axis=0, keepdims=True)
        pick = lax.broadcasted_iota(I32, (LANES, cap), 0).astype(F32) == lane_p

        def take(x):
            v = jnp.dot(x.T.astype(BF16), onehot, preferred_element_type=F32)
            return jnp.sum(jnp.where(pick, v, 0.0), axis=0, keepdims=True)

        q_hi = jnp.floor(q * (1.0 / 256.0))
        a_hi = a.astype(BF16).astype(F32)
        a_mid = (a - a_hi).astype(BF16).astype(F32)
        a_lo = a - a_hi - a_mid
        idx_ref[...] = (row_p * LANES + lane_p).astype(I32)
        dst_ref[...] = (take(q_hi) * 256.0 + take(q - q_hi * 256.0)).astype(I32)
        g_ref[...] = take(a_hi) + take(a_mid) + take(a_lo)


def _route(aff_t3, cap):
    E, R, _ = aff_t3.shape
    assert R <= 256 and E == N_EXPERTS
    slot_spec = pl.BlockSpec((None, 1, cap), lambda s: (jnp.maximum(s - N_EXPERTS, 0), 0, 0))
    tok_spec = pl.BlockSpec((R, LANES), lambda s: (0, 0))
    return pl.pallas_call(
        functools.partial(_route_kernel, cap=cap),
        grid=(2 * E,),
        in_specs=[pl.BlockSpec((None, R, LANES), lambda s: (s % N_EXPERTS, 0, 0))],
        out_specs=[slot_spec, slot_spec, slot_spec, tok_spec, tok_spec],
        out_shape=[jax.ShapeDtypeStruct((E, 1, cap), I32), jax.ShapeDtypeStruct((E, 1, cap), I32),
                   jax.ShapeDtypeStruct((E, 1, cap), F32),
                   jax.ShapeDtypeStruct((R, LANES), F32), jax.ShapeDtypeStruct((R, LANES), F32)],
        scratch_shapes=[pltpu.VMEM((E, R, LANES), F32), pltpu.VMEM((R, LANES), F32),
                        pltpu.VMEM((R, LANES), F32), pltpu.VMEM((R, LANES), F32)],
        compiler_params=_cparams(1, VMEM_LIMIT),
        name="route",
    )(aff_t3)


def _ffn_kernel(idx_ref, dst_ref, h_hbm, gt_ref, wg_ref, wu_ref, wd_ref, c_hbm,
                xs_stage, xsb, hid, o_stage, gsem, ssem, *, tm, cap, nf):
    e, m, f = pl.program_id(0), pl.program_id(1), pl.program_id(2)
    base = e * cap + m * tm
    tr = TOKEN_ROWS
    part = tm // nf
    tf = wg_ref.shape[1]
    tn = wd_ref.shape[1]
    is_first = (e == 0) & (m == 0)
    is_last = (e == pl.num_programs(0) - 1) & (m == pl.num_programs(1) - 1)

    def token_rows(t):
        return pl.ds(pl.multiple_of(t * tr, tr), tr)

    def gather_rows(tile_base, r0, n):
        def issue(r, carry):
            pltpu.make_async_copy(h_hbm.at[token_rows(idx_ref[tile_base + r0 + r]), :],
                                  xs_stage.at[token_rows(r0 + r), :], gsem).start()
            return carry

        lax.fori_loop(0, n, issue, 0, unroll=8)

    def scatter_rows(tile_base, r0, n):
        def issue(r, carry):
            pltpu.make_async_copy(o_stage.at[token_rows(r0 + r), :],
                                  c_hbm.at[token_rows(dst_ref[tile_base + r0 + r]), :], ssem).start()
            return carry

        lax.fori_loop(0, n, issue, 0, unroll=8)

    def scatter_done():
        pltpu.make_async_copy(o_stage, c_hbm.at[pl.ds(0, tm * tr), :], ssem).wait()

    def gather_done():
        pltpu.make_async_copy(h_hbm.at[pl.ds(0, tm * tr), :], xs_stage, gsem).wait()

    @pl.when(f == 0)
    def _stage_in():
        @pl.when(is_first)
        def _():
            gather_rows(base, 0, tm)
            o_stage[...] = jnp.zeros_like(o_stage)

        gather_done()
        for s in range(tr):
            xsb[:, s * LANES:(s + 1) * LANES] = xs_stage[pl.ds(s, tm, stride=tr), :].astype(BF16)

    nxt = jnp.where(is_last, base, base + tm)
    prv = jnp.maximum(base - tm, 0)

    @pl.when(f < nf)
    def _gate_up():
        xs = xsb[...]
        g = jnp.dot(xs, wg_ref[...], preferred_element_type=F32)
        u = jnp.dot(xs, wu_ref[...], preferred_element_type=F32)
        hid[f] = (_silu(g) * u).astype(BF16)
        for r in range(part):
            rr = f * part + r
            pltpu.make_async_copy(h_hbm.at[token_rows(idx_ref[nxt + rr]), :],
                                  xs_stage.at[token_rows(rr), :], gsem).start()
            pltpu.make_async_copy(o_stage.at[token_rows(rr), :],
                                  c_hbm.at[token_rows(dst_ref[prv + rr]), :], ssem).start()

    for n in range(nf):
        @pl.when(f == nf + n)
        def _down(n=n):
            if n == 0:
                scatter_done()

            out = None
            for k in range(nf):
                d = jnp.dot(hid[k], wd_ref[k * tf:(k + 1) * tf, :], preferred_element_type=F32)
                out = d if out is None else out + d
            for j in range(tm // LANES):
                rows = out[j * LANES:(j + 1) * LANES, :] * gt_ref[:, j:j + 1]
                for c in range(tn // LANES):
                    o_stage[pl.ds(j * LANES * tr + n * (tn // LANES) + c, LANES, stride=tr), :] = (
                        rows[:, c * LANES:(c + 1) * LANES])

            if n == nf - 1:
                @pl.when(is_last)
                def _():
                    scatter_rows(base, 0, tm)
                    scatter_done()
                    gather_done()


def _ffn(idx, dst, h_rows, g_t, w_gate, w_up, w_down, tm, tf):
    E, n_m = g_t.shape[0], g_t.shape[1]
    cap = n_m * tm
    n_contrib = E * cap
    nf = D_FF // tf
    tn = D_MODEL // nf
    up = lambda f: jnp.minimum(f, nf - 1)
    down = lambda f: jnp.maximum(f - nf, 0)
    grid_spec = pltpu.PrefetchScalarGridSpec(
        num_scalar_prefetch=2,
        grid=(E, n_m, 2 * nf),
        in_specs=[pl.BlockSpec(memory_space=pl.ANY),
                  pl.BlockSpec((None, None, LANES, tm // LANES), lambda e, m, f, i, d: (e, m, 0, 0)),
                  pl.BlockSpec((None, D_MODEL, tf), lambda e, m, f, i, d: (e, 0, up(f))),
                  pl.BlockSpec((None, D_MODEL, tf), lambda e, m, f, i, d: (e, 0, up(f))),
                  pl.BlockSpec((None, D_FF, tn), lambda e, m, f, i, d: (e, 0, down(f)))],
        out_specs=pl.BlockSpec(memory_space=pl.ANY),
        scratch_shapes=[pltpu.VMEM((tm * TOKEN_ROWS, LANES), F32), pltpu.VMEM((tm, D_MODEL), BF16),
                        pltpu.VMEM((nf, tm, tf), BF16), pltpu.VMEM((tm * TOKEN_ROWS, LANES), F32),
                        pltpu.SemaphoreType.DMA, pltpu.SemaphoreType.DMA],
    )
    return pl.pallas_call(
        functools.partial(_ffn_kernel, tm=tm, cap=cap, nf=nf),
        grid_spec=grid_spec,
        out_shape=jax.ShapeDtypeStruct((n_contrib * TOKEN_ROWS, LANES), F32),
        compiler_params=_cparams(3, FFN_VMEM_LIMIT),
        name="expert_ffn",
    )(idx, dst, h_rows, g_t, w_gate, w_up, w_down)


def _combine_kernel(ch_ref, tt_ref, flag_ref, x1_ref, cs_ref, ce_ref, c_ref, nw_ref, out_ref, acc_ref, *, cg, rows):
    k = pl.program_id(0)
    flag = flag_ref[k]

    @pl.when((flag & 2) > 0)
    def _():
        acc_ref[...] = x1_ref[...]

    @pl.when((flag & 1) > 0)
    def _():
        cio = (lax.broadcasted_iota(I32, (cg, LANES), 0) + ch_ref[k] * cg).astype(F32)
        parts = [jnp.where((cs_ref[r:r + 1, :] <= cio) & (cio < ce_ref[r:r + 1, :]), 1.0, 0.0) for r in range(rows)]
        a = jnp.concatenate(parts, axis=1).T.astype(BF16)
        chunk = jnp.concatenate([c_ref[pl.ds(s, cg, stride=TOKEN_ROWS), :].astype(BF16)
                                 for s in range(TOKEN_ROWS)], axis=1)
        acc_ref[...] += jnp.dot(a, chunk, preferred_element_type=F32)

    @pl.when((flag & 4) > 0)
    def _():
        out_ref[...] = _rms(acc_ref[...], nw_ref[...])


def _combine(ch, tt, flag, x1, cs3, ce3, contrib, final_w, tb, cg):
    T = x1.shape[0]
    rows = tb // LANES
    n_pairs = ch.shape[0]
    grid_spec = pltpu.PrefetchScalarGridSpec(
        num_scalar_prefetch=3,
        grid=(n_pairs,),
        in_specs=[pl.BlockSpec((tb, D_MODEL), lambda k, ch, tt, fl: (tt[k], 0)),
                  pl.BlockSpec((None, rows, LANES), lambda k, ch, tt, fl: (tt[k], 0, 0)),
                  pl.BlockSpec((None, rows, LANES), lambda k, ch, tt, fl: (tt[k], 0, 0)),
                  pl.BlockSpec((cg * TOKEN_ROWS, LANES), lambda k, ch, tt, fl: (ch[k], 0)),
                  pl.BlockSpec((1, D_MODEL), lambda k, ch, tt, fl: (0, 0))],
        out_specs=pl.BlockSpec((tb, D_MODEL), lambda k, ch, tt, fl: (tt[k], 0)),
        scratch_shapes=[pltpu.VMEM((tb, D_MODEL), F32)],
    )
    return pl.pallas_call(
        functools.partial(_combine_kernel, cg=cg, rows=rows),
        grid_spec=grid_spec,
        out_shape=jax.ShapeDtypeStruct((T, D_MODEL), F32),
        compiler_params=_cparams(1, VMEM_LIMIT),
        name="combine_final",
    )(ch, tt, flag, x1, cs3, ce3, contrib, final_w)


def _combine_schedule(cs, ce, tb, cg):
    T = cs.size
    ntt = T // tb
    total = CAPACITY_FACTOR * T
    nch = total // cg
    lo = cs.reshape(ntt, tb)[:, 0].astype(I32)
    hi = ce.reshape(ntt, tb)[:, -1].astype(I32)
    first = jnp.minimum(lo, total - 1) // cg
    last = jnp.where(hi > lo, (hi - 1) // cg, first)
    n = last - first + 1
    ends = jnp.cumsum(n)
    offs = ends - n
    n_pairs = ntt + nch
    ks = jnp.arange(n_pairs, dtype=I32)
    tile = jnp.minimum(jnp.searchsorted(ends, ks, side="right").astype(I32), ntt - 1)
    valid = ks < ends[-1]
    chunk = jnp.where(valid, first[tile] + ks - offs[tile], last[-1])
    flag = jnp.where(valid, 1 + 2 * (ks == offs[tile]) + 4 * (ks == ends[tile] - 1), 0).astype(I32)
    return chunk.astype(I32), tile, flag


def _t5_buckets(rel):
    nb = NUM_BUCKETS // 2
    ret = (rel > 0).astype(np.int32) * nb
    n = np.abs(rel)
    max_exact = nb // 2
    large = max_exact + (np.log(np.maximum(n, 1) / max_exact) / np.log(MAX_DISTANCE / max_exact)
                         * (nb - max_exact)).astype(np.int32)
    large = np.minimum(large, nb - 1)
    return ret + np.where(n < max_exact, n, large)


def _tile(n, pref):
    t = min(n, pref)
    assert n % t == 0
    return t


def _trunk(x, w):
    B, S, _ = x.shape
    T = B * S
    cap = CAPACITY_FACTOR * T // N_EXPERTS
    x2 = x.reshape(T, D_MODEL)
    tm = _tile(T, 512)

    qkv, main = _inproj(x2, w["norm1_w"], w["w_qkv"], w["w_main"], _tile(T, 1024))
    main3 = main.reshape(B, S, MAIN_COLS)
    attn = _attention(qkv.reshape(B, S, QKV_COLS), w["bias_tab"], w["attn_sink"], w["attn_norm_w"])
    yf = _ssd_pass(main3, w["conv_w"], w["conv_b"], w["dtb"], w["alog"], reverse=False)
    ssm = _ssd_pass(main3, w["conv_w"], w["conv_b"], w["dtb"], w["alog"], reverse=True,
                    yf=yf, dskip=w["dskip"], norm_w=w["ssm_norm_w"])
    x1, h_rows, aff = _outproj(x2, attn.reshape(T, D_ATTN), ssm.reshape(T, D_SSM), w["w_out"], w["norm2_w"],
                          w["router_w2"], tm)

    tb = _tile(T, 256)
    cg = 256
    idx, dst, g, cs, ce = _route(aff.T.reshape(N_EXPERTS, T // LANES, LANES), cap)
    tmf = _tile(cap, 1024)
    g_t = jnp.swapaxes(g.reshape(N_EXPERTS, cap // tmf, tmf // LANES, LANES), 2, 3)
    contrib = _ffn(idx.reshape(-1), dst.reshape(-1), h_rows, g_t, w["w_gate"], w["w_up"], w["w_down"], tmf, 512)
    ch, tt, flag = _combine_schedule(cs, ce, tb, cg)
    tile3 = lambda a: a.reshape(T // tb, tb // LANES, LANES)
    y = _combine(ch, tt, flag, x1, tile3(cs), tile3(ce), contrib, w["final_norm_w"], tb, cg)
    return y.reshape(B, S, D_MODEL)


def _prep_weights(rel_bias, norm1_w, w_in, conv_w, conv_b, dt_bias_fwd, dt_bias_bwd, a_log_fwd, a_log_bwd,
                  d_skip, ssm_norm_w, attn_sink, attn_norm_w, w_out, norm2_w, router_w, w_gate, w_up, w_down,
                  final_norm_w):
    o1 = D_ATTN; o2 = o1 + D_KV; o3 = o2 + D_KV; o4 = o3 + D_SSM; o5 = o4 + D_SSM + 2 * D_BC
    wi = w_in[0]
    scale = 1.0 / math.sqrt(HEAD_DIM)
    perm = np.concatenate([np.arange(h * HEAD_DIM, (h + 1) * HEAD_DIM) for h in ATTN_HEAD_ORDER])
    w_qkv = jnp.concatenate([(wi[:, :o1] * scale)[:, perm], wi[:, o1:o3]], axis=1).astype(BF16)
    w_out_p = jnp.concatenate([w_out[0][:D_ATTN][perm], w_out[0][D_ATTN:]], axis=0).astype(BF16)
    w_main = jnp.concatenate([wi[:, o3:], jnp.zeros((D_MODEL, LANES - 2 * SSM_HEADS), F32)], axis=1).astype(BF16)
    rel = np.arange(3 * BLOCK)[None, :] - BLOCK - np.arange(BLOCK)[:, None]
    onehot = (jnp.asarray(_t5_buckets(rel), I32)[..., None] == jnp.arange(NUM_BUCKETS, dtype=I32)).astype(F32)
    bias_tab = jnp.einsum("qsn,nh->hqs", onehot, rel_bias.astype(F32), precision=lax.Precision.HIGHEST)
    bias_tab = jnp.where(jnp.asarray(np.abs(rel) <= WINDOW)[None], bias_tab, NEG_INF)
    rw = router_w[0].astype(F32)
    rw_hi = rw.astype(BF16)
    router_w2 = jnp.concatenate([rw_hi, (rw - rw_hi.astype(F32)).astype(BF16)], axis=1)
    pad = jnp.zeros((LANES - 2 * SSM_HEADS,), F32)
    row = lambda a: a.reshape(1, -1).astype(F32)
    return dict(
        norm1_w=row(norm1_w[0]), w_qkv=w_qkv, w_main=w_main, bias_tab=bias_tab, attn_sink=attn_sink[0].astype(F32),
        attn_norm_w=row(attn_norm_w[0][perm]), conv_w=conv_w[0].T.astype(F32), conv_b=row(conv_b[0]),
        dtb=row(jnp.concatenate([dt_bias_fwd[0], dt_bias_bwd[0], pad])),
        alog=row(jnp.concatenate([a_log_fwd[0], a_log_bwd[0], pad])),
        dskip=row(jnp.repeat(d_skip[0], SSM_HEAD_DIM)), ssm_norm_w=row(ssm_norm_w[0]),
        w_out=w_out_p, norm2_w=row(norm2_w[0]), router_w2=router_w2,
        w_gate=w_gate[0].astype(BF16), w_up=w_up[0].astype(BF16), w_down=w_down[0].astype(BF16),
        final_norm_w=row(final_norm_w))


def kernel(x_prompt, x_sample, rel_bias, norm1_w, w_in, conv_w, conv_b, dt_bias_fwd, dt_bias_bwd, a_log_fwd, a_log_bwd, d_skip, ssm_norm_w, attn_sink, attn_norm_w, w_out, norm2_w, router_w, w_gate, w_up, w_down, final_norm_w):
    assert norm1_w.shape[0] == 1
    w = _prep_weights(rel_bias, norm1_w, w_in, conv_w, conv_b, dt_bias_fwd, dt_bias_bwd, a_log_fwd, a_log_bwd,
                      d_skip, ssm_norm_w, attn_sink, attn_norm_w, w_out, norm2_w, router_w, w_gate, w_up,
                      w_down, final_norm_w)
    return (_trunk(x_prompt, w), _trunk(x_sample, w))
```

```python
import functools
import math

import numpy as np
import jax
import jax.numpy as jnp
from jax import lax
from jax.experimental import pallas as pl
from jax.experimental.pallas import tpu as pltpu

F32 = jnp.float32
BF16 = jnp.bfloat16
I32 = jnp.int32

D_MODEL = 2048
HEAD_DIM = 64
N_HEADS = 16
N_KV_HEADS = 4
D_ATTN = 1024
D_KV = 256
WINDOW = 128
BLOCK = 128
NUM_BUCKETS = 32
MAX_DISTANCE = 128
SSM_HEAD_DIM = 64
SSM_HEADS = 16
D_SSM = 1024
SSM_STATE = 128
SSM_GROUPS = 2
HEADS_PER_GROUP = SSM_HEADS // SSM_GROUPS
D_BC = 256
CHUNK = 128
N_EXPERTS = 16
CAPACITY_FACTOR = 2
D_FF = 2048
EPS = 1e-6
NEG_INF = -1e30

LANES = 128
HALO_ROWS = 8
TOKEN_ROWS = D_MODEL // LANES

QKV_COLS = D_ATTN + 2 * D_KV
COL_QK, COL_QV = D_ATTN, D_ATTN + D_KV
COL_Z, COL_X, COL_B, COL_DT = 0, 1024, 2048, 2560
MAIN_COLS = COL_DT + LANES
MAIN_TN = 896

ATTN_HEAD_ORDER = tuple(8 * t + 4 * par + i for t in range(2) for i in range(4) for par in range(2))

SSD_BATCH_BLOCK = 4
ATTN_BATCH_BLOCK = 4

VMEM_LIMIT = 56 * 1024 * 1024
FFN_VMEM_LIMIT = 60 * 1024 * 1024


def _cparams(n_axes, vmem=None):
    return pltpu.CompilerParams(dimension_semantics=("arbitrary",) * n_axes,
                                vmem_limit_bytes=vmem)


def _rms(x, w):
    return x * lax.rsqrt(jnp.mean(x * x, axis=-1, keepdims=True) + EPS) * w


def _silu(x):
    return x * jax.nn.sigmoid(x)


def _inproj_kernel(x_ref, nw_ref, wq_ref, wr_ref, oq_ref, om_ref, hn_ref):
    j = pl.program_id(1)

    @pl.when(j == 0)
    def _():
        hn = _rms(x_ref[...], nw_ref[...]).astype(BF16)
        hn_ref[...] = hn
        oq_ref[...] = jnp.dot(hn, wq_ref[...], preferred_element_type=F32).astype(BF16)

    @pl.when(j > 0)
    def _():
        om_ref[...] = jnp.dot(hn_ref[...], wr_ref[...], preferred_element_type=F32)


def _inproj(x2, norm_w, w_qkv, w_main, tm):
    T = x2.shape[0]
    rest = lambda j: jnp.maximum(j - 1, 0)
    return pl.pallas_call(
        _inproj_kernel,
        grid=(T // tm, 1 + MAIN_COLS // MAIN_TN),
        in_specs=[pl.BlockSpec((tm, D_MODEL), lambda i, j: (i, 0)),
                  pl.BlockSpec((1, D_MODEL), lambda i, j: (0, 0)),
                  pl.BlockSpec((D_MODEL, QKV_COLS), lambda i, j: (0, 0), pipeline_mode=pl.Buffered(1)),
                  pl.BlockSpec((D_MODEL, MAIN_TN), lambda i, j: (0, rest(j)))],
        out_specs=[pl.BlockSpec((tm, QKV_COLS), lambda i, j: (i, 0)),
                   pl.BlockSpec((tm, MAIN_TN), lambda i, j: (i, rest(j)))],
        out_shape=[jax.ShapeDtypeStruct((T, QKV_COLS), BF16), jax.ShapeDtypeStruct((T, MAIN_COLS), F32)],
        scratch_shapes=[pltpu.VMEM((tm, D_MODEL), BF16)],
        compiler_params=_cparams(2, VMEM_LIMIT),
        name="inproj",
    )(x2, norm_w, w_qkv, w_main)


def _attn_kernel(q_ref, kp_ref, ko_ref, kn_ref, vp_ref, vo_ref, vn_ref, bias_ref, sink_ref, nw_ref,
                 o_ref, *, nb):
    i = pl.program_id(1)
    lane = lax.broadcasted_iota(I32, (3 * BLOCK, LANES), 1)
    m_lo = jnp.where(lane < HEAD_DIM, 1.0, 0.0).astype(BF16)
    m_up = jnp.where(lane < HEAD_DIM, 0.0, 1.0).astype(BF16)
    low_q = lax.broadcasted_iota(I32, (BLOCK, LANES), 1) < HEAD_DIM
    top = lax.broadcasted_iota(I32, (2 * BLOCK, 1), 0) < BLOCK
    tiles_per_pair = D_ATTN // LANES // (D_KV // LANES)

    def body(bb, edge):
        tiles = []
        for t in range(D_KV // LANES):
            sl = slice(t * LANES, (t + 1) * LANES)
            kt = jnp.concatenate([kp_ref[bb, :, sl], ko_ref[bb, :, sl], kn_ref[bb, :, sl]], axis=0)
            vt = jnp.concatenate([vp_ref[bb, :, sl], vo_ref[bb, :, sl], vn_ref[bb, :, sl]], axis=0)
            rhs = [(kt * m_lo, vt * m_lo + m_up), (kt * m_up, vt * m_up + m_lo)]
            for j in range(0, tiles_per_pair, 2):
                qa, qb = t * tiles_per_pair + j, t * tiles_per_pair + j + 1
                q2 = jnp.concatenate([q_ref[bb, :, qa * LANES:(qa + 1) * LANES],
                                      q_ref[bb, :, qb * LANES:(qb + 1) * LANES]], axis=0)
                res = []
                for par in range(2):
                    kz, vz = rhs[par]
                    ha, hb = ATTN_HEAD_ORDER[2 * qa + par], ATTN_HEAD_ORDER[2 * qb + par]
                    s = lax.dot_general(q2, kz, (((1,), (1,)), ((), ())), preferred_element_type=F32)
                    s = s + jnp.concatenate([bias_ref[ha], bias_ref[hb]], axis=0) + edge
                    sk = jnp.where(top, sink_ref[ha], sink_ref[hb])
                    m = jnp.maximum(jnp.max(s, axis=-1, keepdims=True), sk)
                    p = jnp.exp(s - m).astype(BF16)
                    pv = jnp.dot(p, vz, preferred_element_type=F32)
                    den = (pv[:, LANES - 1:LANES] if par == 0 else pv[:, 0:1]) + jnp.exp(sk - m)
                    res.append(pv * (1.0 / den))
                tiles.append(jnp.where(low_q, res[0][:BLOCK], res[1][:BLOCK]))
                tiles.append(jnp.where(low_q, res[0][BLOCK:], res[1][BLOCK:]))
        o = jnp.concatenate(tiles, axis=-1)
        o_ref[bb] = _rms(o, nw_ref[...]).astype(BF16)

    kcol = lax.broadcasted_iota(I32, (1, 3 * BLOCK), 1)
    edge = (jnp.where((kcol < BLOCK) & (i == 0), NEG_INF, 0.0)
            + jnp.where((kcol >= 2 * BLOCK) & (i == nb - 1), NEG_INF, 0.0))
    for bb in range(q_ref.shape[0]):
        body(bb, edge)


def _attention(qkv3, bias_tab, sink, norm_w):
    B, S, _ = qkv3.shape
    nb = S // BLOCK
    kcol, vcol = COL_QK // D_KV, COL_QV // D_KV
    prev = lambda i: jnp.maximum(i - 1, 0)
    nxt = lambda i: jnp.minimum(i + 1, nb - 1)
    nbt = ATTN_BATCH_BLOCK if B % ATTN_BATCH_BLOCK == 0 else 1
    kv = lambda col, f: pl.BlockSpec((nbt, BLOCK, D_KV), lambda b, i: (b, f(i), col))
    same = lambda i: i
    return pl.pallas_call(
        functools.partial(_attn_kernel, nb=nb),
        grid=(B // nbt, nb),
        in_specs=[pl.BlockSpec((nbt, BLOCK, D_ATTN), lambda b, i: (b, i, 0)),
                  kv(kcol, prev), kv(kcol, same), kv(kcol, nxt),
                  kv(vcol, prev), kv(vcol, same), kv(vcol, nxt),
                  pl.BlockSpec((N_HEADS, BLOCK, 3 * BLOCK), lambda b, i: (0, 0, 0)),
                  pl.BlockSpec(memory_space=pltpu.SMEM),
                  pl.BlockSpec((1, D_ATTN), lambda b, i: (0, 0))],
        out_specs=pl.BlockSpec((nbt, BLOCK, D_ATTN), lambda b, i: (b, i, 0)),
        out_shape=jax.ShapeDtypeStruct((B, S, D_ATTN), BF16),
        compiler_params=_cparams(2),
        name="attention",
    )(qkv3, qkv3, qkv3, qkv3, qkv3, qkv3, qkv3, bias_tab, sink, norm_w)


def _conv_silu(u, prev8, next8, w, b, has_prev, has_next):
    prev_row = jnp.where(has_prev, prev8[HALO_ROWS - 1:HALO_ROWS, :], 0.0)
    next_row = jnp.where(has_next, next8[0:1, :], 0.0)
    rid = lax.broadcasted_iota(I32, u.shape, 0)
    um = jnp.where(rid == 0, prev_row, pltpu.roll(u, 1, 0))
    up = jnp.where(rid == CHUNK - 1, next_row, pltpu.roll(u, CHUNK - 1, 0))
    y = b + um * w[0:1, :] + u * w[1:2, :] + up * w[2:3, :]
    return _silu(y)


def _ssd_kernel(*refs, reverse, nc):
    h_ref = refs[-1]

    @pl.when(pl.program_id(1) == 0)
    def _():
        h_ref[...] = jnp.zeros_like(h_ref)

    for bb in range(h_ref.shape[0]):
        _ssd_chunk(bb, refs, reverse, nc)


def _ssd_chunk(bb, refs, reverse, nc):
    if reverse:
        (z_ref, yf_ref, x_ref, xp_ref, xn_ref, bc_ref, bcp_ref, bcn_ref, dt_ref, cw_ref, cb_ref, dtb_ref,
         alog_ref, dskip_ref, nw_ref, o_ref, h_ref) = refs
    else:
        (x_ref, xp_ref, xn_ref, bc_ref, bcp_ref, bcn_ref, dt_ref, cw_ref, cb_ref, dtb_ref,
         alog_ref, o_ref, h_ref) = refs
    step = pl.program_id(1)
    c = (nc - 1 - step) if reverse else step
    has_prev, has_next = c > 0, c < nc - 1
    cw, cb = cw_ref[...], cb_ref[...]
    xc = _conv_silu(x_ref[bb], xp_ref[bb], xn_ref[bb], cw[:, :D_SSM], cb[:, :D_SSM], has_prev, has_next)
    bcv = _conv_silu(bc_ref[bb], bcp_ref[bb], bcn_ref[bb], cw[:, D_SSM:], cb[:, D_SSM:], has_prev, has_next)

    raw = dt_ref[bb] + dtb_ref[...]
    dt = jnp.maximum(raw, 0.0) + jnp.log1p(jnp.exp(-jnp.abs(raw)))
    a = dt * (-jnp.exp(alog_ref[...]))
    li = lax.broadcasted_iota(I32, (CHUNK, CHUNK), 0)
    si = lax.broadcasted_iota(I32, (CHUNK, CHUNK), 1)
    incl = jnp.dot(jnp.where(li >= si, 1.0, 0.0).astype(F32), a, precision=lax.Precision.HIGHEST,
                   preferred_element_type=F32)
    tot = incl[CHUNK - 1:CHUNK, :]
    if reverse:
        pcs = incl - a
        dstate = jnp.exp(pcs)
        yscale = jnp.exp(tot - pcs)
        mask = si >= li
    else:
        pcs = incl
        dstate = jnp.exp(tot - pcs)
        yscale = jnp.exp(pcs)
        mask = li >= si
    pcs_t = pcs.T
    lane0 = SSM_HEADS if reverse else 0
    P = SSM_HEAD_DIM
    gw = HEADS_PER_GROUP * P

    def spread(x, width, pieces):
        k = lax.broadcasted_iota(I32, (LANES, SSM_HEADS * width), 0)
        c = lax.broadcasted_iota(I32, (LANES, SSM_HEADS * width), 1)
        sel = jnp.where(k == lane0 + c // width, 1.0, 0.0).astype(BF16)
        out, rem = None, x
        for _ in range(pieces):
            piece = rem.astype(BF16)
            rem = rem - piece.astype(F32)
            d = jnp.dot(piece, sel, preferred_element_type=F32)
            out = d if out is None else out + d
        return out

    scales = spread(jnp.concatenate([dt, dstate, yscale, jnp.broadcast_to(jnp.exp(tot), (HALO_ROWS, LANES))], axis=0),
                    P, 2)
    dt_e, ds_e, ys_e = scales[:CHUNK], scales[CHUNK:2 * CHUNK], scales[2 * CHUNK:3 * CHUNK]
    cdec_e = scales[3 * CHUNK:3 * CHUNK + 1]
    col_e = spread(pcs, CHUNK, 3)
    xdt = xc * dt_e
    xdt_b = xdt.astype(BF16)
    xs_b = (xdt * ds_e).astype(BF16)
    low = lax.broadcasted_iota(I32, (CHUNK, LANES), 1) < P

    y_groups = []
    for g in range(SSM_GROUPS):
        bg = bcv[:, g * SSM_STATE:(g + 1) * SSM_STATE]
        cg = bcv[:, D_BC + g * SSM_STATE:D_BC + (g + 1) * SSM_STATE].astype(BF16)
        cbm = lax.dot_general(cg, bg.astype(BF16), (((1,), (1,)), ((), ())), preferred_element_type=F32)
        hg = h_ref[bb, g]
        yoff = jnp.dot(cg, hg.astype(BF16), preferred_element_type=F32)
        yd = []
        for jp in range(HEADS_PER_GROUP // 2):
            tile = (g * HEADS_PER_GROUP) // 2 + jp
            xpair = xdt_b[:, tile * LANES:(tile + 1) * LANES]
            halves = []
            for par in range(2):
                j = 2 * tile + par
                ln = lane0 + j
                col = col_e[:, j * CHUNK:(j + 1) * CHUNK]
                row = pcs_t[ln:ln + 1, :]
                seg = (row - col) if reverse else (col - row)
                lm = jnp.exp(jnp.where(mask, seg, -jnp.inf))
                halves.append(jnp.dot((cbm * lm).astype(BF16), xpair, preferred_element_type=F32))
            yd.append(jnp.where(low, halves[0], halves[1]))
        sl = slice(g * gw, (g + 1) * gw)
        y_groups.append(jnp.concatenate(yd, axis=1) + yoff * ys_e[:, sl])
        snew = jnp.dot(bg.T.astype(BF16), xs_b[:, sl], preferred_element_type=F32)
        h_ref[bb, g] = hg * cdec_e[:, sl] + snew
    y = jnp.concatenate(y_groups, axis=1)

    if not reverse:
        o_ref[bb] = y
    else:
        ytot = yf_ref[bb] + y + dskip_ref[...] * xc
        yz = ytot * _silu(z_ref[bb])
        halves = []
        for g in range(SSM_GROUPS):
            seg = yz[:, g * gw:(g + 1) * gw]
            halves.append(seg * lax.rsqrt(jnp.mean(seg * seg, axis=-1, keepdims=True) + EPS))
        o_ref[bb] = (jnp.concatenate(halves, axis=1) * nw_ref[...]).astype(BF16)


def _ssd_pass(proj3, conv_w3, conv_b, dtb, alog, reverse, yf=None, dskip=None, norm_w=None):
    B, S, _ = proj3.shape
    nc = S // CHUNK
    hp = CHUNK // HALO_ROWS
    nh = S // HALO_ROWS
    ch = (lambda s: nc - 1 - s) if reverse else (lambda s: s)
    nbt = SSD_BATCH_BLOCK if B % SSD_BATCH_BLOCK == 0 else 1
    main = lambda w, col: pl.BlockSpec((nbt, CHUNK, w), lambda b, s: (b, ch(s), col // w))
    halo_p = lambda w, col: pl.BlockSpec((nbt, HALO_ROWS, w),
                                         lambda b, s: (b, jnp.maximum(ch(s) * hp - 1, 0), col // w))
    halo_n = lambda w, col: pl.BlockSpec((nbt, HALO_ROWS, w),
                                         lambda b, s: (b, jnp.minimum((ch(s) + 1) * hp, nh - 1), col // w))
    full = lambda a: pl.BlockSpec(a.shape, lambda b, s: (0,) * a.ndim)
    d_bc2 = 2 * D_BC
    in_specs = [main(D_SSM, COL_X), halo_p(D_SSM, COL_X), halo_n(D_SSM, COL_X),
                main(d_bc2, COL_B), halo_p(d_bc2, COL_B), halo_n(d_bc2, COL_B),
                main(LANES, COL_DT), full(conv_w3), full(conv_b), full(dtb), full(alog)]
    args = [proj3, proj3, proj3, proj3, proj3, proj3, proj3, conv_w3, conv_b, dtb, alog]
    if reverse:
        in_specs = [main(D_SSM, COL_Z), pl.BlockSpec((nbt, CHUNK, D_SSM), lambda b, s: (b, ch(s), 0))] + in_specs
        in_specs += [full(dskip), full(norm_w)]
        args = [proj3, yf] + args + [dskip, norm_w]
    return pl.pallas_call(
        functools.partial(_ssd_kernel, reverse=reverse, nc=nc),
        grid=(B // nbt, nc),
        in_specs=in_specs,
        out_specs=pl.BlockSpec((nbt, CHUNK, D_SSM), lambda b, s: (b, ch(s), 0)),
        out_shape=jax.ShapeDtypeStruct((B, S, D_SSM), BF16 if reverse else F32),
        scratch_shapes=[pltpu.VMEM((nbt, SSM_GROUPS, SSM_STATE, HEADS_PER_GROUP * SSM_HEAD_DIM), F32)],
        compiler_params=_cparams(2),
        name="ssd_bwd" if reverse else "ssd_fwd",
    )(*args)


def _outproj_kernel(x_ref, a_ref, s_ref, w_ref, n2_ref, rw2_ref, x1_ref, h_ref, aff_ref):
    tm = x_ref.shape[0]
    x1 = (x_ref[...]
          + jnp.dot(a_ref[...], w_ref[:D_ATTN, :], preferred_element_type=F32)
          + jnp.dot(s_ref[...], w_ref[D_ATTN:, :], preferred_element_type=F32))
    x1_ref[...] = x1
    hn = _rms(x1, n2_ref[...])
    for s in range(TOKEN_ROWS):
        h_ref[pl.ds(s, tm, stride=TOKEN_ROWS), :] = hn[:, s * LANES:(s + 1) * LANES]
    hn_hi = hn.astype(BF16)
    hn_lo = (hn - hn_hi.astype(F32)).astype(BF16)
    l_hi = jnp.dot(hn_hi, rw2_ref[...], preferred_element_type=F32)
    l_lo = jnp.dot(hn_lo, rw2_ref[:, :N_EXPERTS], preferred_element_type=F32)
    logits = l_hi[:, :N_EXPERTS] + l_hi[:, N_EXPERTS:] + l_lo
    e = jnp.exp(logits - jnp.max(logits, axis=-1, keepdims=True))
    aff_ref[...] = e / jnp.sum(e, axis=-1, keepdims=True)


def _outproj(x2, attn2, ssm2, w_out, norm2_w, router_w, tm):
    T = x2.shape[0]
    row = lambda w: pl.BlockSpec((tm, w), lambda i: (i, 0))
    full = lambda a: pl.BlockSpec(a.shape, lambda i: (0,) * a.ndim)
    return pl.pallas_call(
        _outproj_kernel,
        grid=(T // tm,),
        in_specs=[row(D_MODEL), row(D_ATTN), row(D_SSM), full(w_out), full(norm2_w), full(router_w)],
        out_specs=[row(D_MODEL), pl.BlockSpec((tm * TOKEN_ROWS, LANES), lambda i: (i, 0)), row(N_EXPERTS)],
        out_shape=[jax.ShapeDtypeStruct((T, D_MODEL), F32),
                   jax.ShapeDtypeStruct((T * TOKEN_ROWS, LANES), F32),
                   jax.ShapeDtypeStruct((T, N_EXPERTS), F32)],
        compiler_params=_cparams(1, VMEM_LIMIT),
        name="outproj_router",
    )(x2, attn2, ssm2, w_out, norm2_w, router_w)


def _count(m):
    c = jnp.sum(jnp.where(m, 1.0, 0.0), axis=0, keepdims=True)
    return jnp.sum(c, axis=1, keepdims=True)


def _tri(n, m, fn):
    return jnp.where(fn(lax.broadcasted_iota(I32, (n, m), 0), lax.broadcasted_iota(I32, (n, m), 1)),
                     1.0, 0.0).astype(BF16)


def _dot_u16(lhs01, x):
    hi = jnp.floor(x * (1.0 / 256.0))
    lo = x - hi * 256.0
    return (jnp.dot(lhs01, hi.astype(BF16), preferred_element_type=F32) * 256.0
            + jnp.dot(lhs01, lo.astype(BF16), preferred_element_type=F32))


def _cumsum_rowmajor(x):
    R = x.shape[0]
    within = jnp.dot(x.astype(BF16), _tri(LANES, LANES, lambda k, l: k <= l), preferred_element_type=F32)
    rowtot = jnp.broadcast_to(within[:, LANES - 1:LANES], (R, LANES))
    before = _dot_u16(_tri(R, R, lambda i, k: k < i), rowtot)
    return within + before


def _route_kernel(a_ref, idx_ref, dst_ref, g_ref, cs_ref, ce_ref, sel_s, cnt_s, cs_s, rank_s, *, cap):
    s = pl.program_id(0)
    R = a_ref.shape[0]

    @pl.when(s < N_EXPERTS)
    def _select():
        bits = pltpu.bitcast(a_ref[...], I32)
        capf = jnp.float32(cap)

        def body(k, tau):
            cand = tau | lax.shift_left(jnp.int32(1), 30 - k)
            return jnp.where(_count(bits >= cand) >= capf, cand, tau)

        tau = lax.fori_loop(0, 31, body, jnp.zeros((1, 1), I32))
        gt = bits > tau
        eq = bits == tau
        need = capf - _count(gt)
        ties = _cumsum_rowmajor(jnp.where(eq, 1.0, 0.0))
        sel = jnp.where(gt | (eq & (ties <= need)), 1.0, 0.0)
        sel_s[s] = sel

        @pl.when(s == 0)
        def _():
            cnt_s[...] = sel

        @pl.when(s > 0)
        def _():
            cnt_s[...] += sel

    @pl.when(s == N_EXPERTS)
    def _prefix():
        cnt = cnt_s[...]
        ce = _cumsum_rowmajor(cnt)
        cs_s[...] = ce - cnt
        rank_s[...] = jnp.zeros_like(rank_s)
        cs_ref[...] = ce - cnt
        ce_ref[...] = ce

    @pl.when(s >= N_EXPERTS)
    def _invert():
        e = s - N_EXPERTS
        sel = sel_s[e]
        rank = rank_s[...]
        q = cs_s[...] + rank
        rank_s[...] = rank + sel
        a = a_ref[...]
        within = jnp.dot(sel.astype(BF16), _tri(LANES, LANES, lambda k, l: k <= l), preferred_element_type=F32)
        n_b = jnp.broadcast_to(within[:, LANES - 1:LANES], (R, LANES))
        rowend = jnp.dot(_tri(R, R, lambda i, k: k <= i), n_b.astype(BF16), preferred_element_type=F32)
        slot = lax.broadcasted_iota(I32, (R, cap), 1).astype(F32)
        done = jnp.where(rowend[:, 0:1] <= slot, 1.0, 0.0)
        ones = jnp.ones((8, R), BF16)
        row_p = jnp.dot(ones, done.astype(BF16), preferred_element_type=F32)[0:1]
        start_p = jnp.dot(ones, (done * n_b[:, 0:1]).astype(BF16), preferred_element_type=F32)[0:1]
        onehot = jnp.where(lax.broadcasted_iota(I32, (R, cap), 0).astype(F32) == row_p, 1.0, 0.0).astype(BF16)
        w_t = jnp.dot(within.T.astype(BF16), onehot, preferred_element_type=F32)
        k_in_row = lax.broadcasted_iota(I32, (1, cap), 1).astype(F32) - start_p
        lane_p = jnp.sum(jnp.where(w_t <= k_in_row, 1.0, 0.0), axis=0, keepdims=True)
        pick = lax.broadcasted_iota(I32, (LANES, cap), 0).astype(F32) == lane_p

        def take(x):
            v = jnp.dot(x.T.astype(BF16), onehot, preferred_element_type=F32)
            return jnp.sum(jnp.where(pick, v, 0.0), axis=0, keepdims=True)

        q_hi = jnp.floor(q * (1.0 / 256.0))
        a_hi = a.astype(BF16).astype(F32)
        a_mid = (a - a_hi).astype(BF16).astype(F32)
        a_lo = a - a_hi - a_mid
        idx_ref[...] = (row_p * LANES + lane_p).astype(I32)
        dst_ref[...] = (take(q_hi) * 256.0 + take(q - q_hi * 256.0)).astype(I32)
        g_ref[...] = take(a_hi) + take(a_mid) + take(a_lo)


def _route(aff_t3, cap):
    E, R, _ = aff_t3.shape
    assert R <= 256 and E == N_EXPERTS
    slot_spec = pl.BlockSpec((None, 1, cap), lambda s: (jnp.maximum(s - N_EXPERTS, 0), 0, 0))
    tok_spec = pl.BlockSpec((R, LANES), lambda s: (0, 0))
    return pl.pallas_call(
        functools.partial(_route_kernel, cap=cap),
        grid=(2 * E,),
        in_specs=[pl.BlockSpec((None, R, LANES), lambda s: (s % N_EXPERTS, 0, 0))],
        out_specs=[slot_spec, slot_spec, slot_spec, tok_spec, tok_spec],
        out_shape=[jax.ShapeDtypeStruct((E, 1, cap), I32), jax.ShapeDtypeStruct((E, 1, cap), I32),
                   jax.ShapeDtypeStruct((E, 1, cap), F32),
                   jax.ShapeDtypeStruct((R, LANES), F32), jax.ShapeDtypeStruct((R, LANES), F32)],
        scratch_shapes=[pltpu.VMEM((E, R, LANES), F32), pltpu.VMEM((R, LANES), F32),
                        pltpu.VMEM((R, LANES), F32), pltpu.VMEM((R, LANES), F32)],
        compiler_params=_cparams(1, VMEM_LIMIT),
        name="route",
    )(aff_t3)


def _ffn_kernel(idx_ref, dst_ref, h_hbm, gt_ref, wg_ref, wu_ref, wd_ref, c_hbm,
                xs_stage, xsb, hid, o_stage, gsem, ssem, *, tm, cap, nf):
    e, m, f = pl.program_id(0), pl.program_id(1), pl.program_id(2)
    base = e * cap + m * tm
    tr = TOKEN_ROWS
    part = tm // nf
    tf = wg_ref.shape[1]
    tn = wd_ref.shape[1]
    is_first = (e == 0) & (m == 0)
    is_last = (e == pl.num_programs(0) - 1) & (m == pl.num_programs(1) - 1)

    def token_rows(t):
        return pl.ds(pl.multiple_of(t * tr, tr), tr)

    def gather_rows(tile_base, r0, n):
        def issue(r, carry):
            pltpu.make_async_copy(h_hbm.at[token_rows(idx_ref[tile_base + r0 + r]), :],
                                  xs_stage.at[token_rows(r0 + r), :], gsem).start()
            return carry

        lax.fori_loop(0, n, issue, 0, unroll=8)

    def scatter_rows(tile_base, r0, n):
        def issue(r, carry):
            pltpu.make_async_copy(o_stage.at[token_rows(r0 + r), :],
                                  c_hbm.at[token_rows(dst_ref[tile_base + r0 + r]), :], ssem).start()
            return carry

        lax.fori_loop(0, n, issue, 0, unroll=8)

    def scatter_done():
        pltpu.make_async_copy(o_stage, c_hbm.at[pl.ds(0, tm * tr), :], ssem).wait()

    def gather_done():
        pltpu.make_async_copy(h_hbm.at[pl.ds(0, tm * tr), :], xs_stage, gsem).wait()

    @pl.when(f == 0)
    def _stage_in():
        @pl.when(is_first)
        def _():
            gather_rows(base, 0, tm)
            o_stage[...] = jnp.zeros_like(o_stage)

        gather_done()
        for s in range(tr):
            xsb[:, s * LANES:(s + 1) * LANES] = xs_stage[pl.ds(s, tm, stride=tr), :].astype(BF16)

    nxt = jnp.where(is_last, base, base + tm)
    prv = jnp.maximum(base - tm, 0)

    for k in range(nf):
        @pl.when(f == k)
        def _gate_up(k=k):
            xs = xsb[...]
            g = jnp.dot(xs, wg_ref[...], preferred_element_type=F32)
            u = jnp.dot(xs, wu_ref[...], preferred_element_type=F32)
            hid[:, k * tf:(k + 1) * tf] = (_silu(g) * u).astype(BF16)
            for r in range(k * part, (k + 1) * part):
                pltpu.make_async_copy(h_hbm.at[token_rows(idx_ref[nxt + r]), :],
                                      xs_stage.at[token_rows(r), :], gsem).start()
                pltpu.make_async_copy(o_stage.at[token_rows(r), :],
                                      c_hbm.at[token_rows(dst_ref[prv + r]), :], ssem).start()

    for n in range(nf):
        @pl.when(f == nf + n)
        def _down(n=n):
            if n == 0:
                scatter_done()

            out = jnp.dot(hid[...], wd_ref[...], preferred_element_type=F32)
            for j in range(tm // LANES):
                rows = out[j * LANES:(j + 1) * LANES, :] * gt_ref[:, j:j + 1]
                for c in range(tn // LANES):
                    o_stage[pl.ds(j * LANES * tr + n * (tn // LANES) + c, LANES, stride=tr), :] = (
                        rows[:, c * LANES:(c + 1) * LANES])

            if n == nf - 1:
                @pl.when(is_last)
                def _():
                    scatter_rows(base, 0, tm)
                    scatter_done()
                    gather_done()


def _ffn(idx, dst, h_rows, g_t, w_gate, w_up, w_down, tm, tf):
    E, n_m = g_t.shape[0], g_t.shape[1]
    cap = n_m * tm
    n_contrib = E * cap
    nf = D_FF // tf
    tn = D_MODEL // nf
    up = lambda f: jnp.minimum(f, nf - 1)
    down = lambda f: jnp.maximum(f - nf, 0)
    grid_spec = pltpu.PrefetchScalarGridSpec(
        num_scalar_prefetch=2,
        grid=(E, n_m, 2 * nf),
        in_specs=[pl.BlockSpec(memory_space=pl.ANY),
                  pl.BlockSpec((None, None, LANES, tm // LANES), lambda e, m, f, i, d: (e, m, 0, 0)),
                  pl.BlockSpec((None, D_MODEL, tf), lambda e, m, f, i, d: (e, 0, up(f))),
                  pl.BlockSpec((None, D_MODEL, tf), lambda e, m, f, i, d: (e, 0, up(f))),
                  pl.BlockSpec((None, D_FF, tn), lambda e, m, f, i, d: (e, 0, down(f)))],
        out_specs=pl.BlockSpec(memory_space=pl.ANY),
        scratch_shapes=[pltpu.VMEM((tm * TOKEN_ROWS, LANES), F32), pltpu.VMEM((tm, D_MODEL), BF16),
                        pltpu.VMEM((tm, D_FF), BF16), pltpu.VMEM((tm * TOKEN_ROWS, LANES), F32),
                        pltpu.SemaphoreType.DMA, pltpu.SemaphoreType.DMA],
    )
    return pl.pallas_call(
        functools.partial(_ffn_kernel, tm=tm, cap=cap, nf=nf),
        grid_spec=grid_spec,
        out_shape=jax.ShapeDtypeStruct((n_contrib * TOKEN_ROWS, LANES), F32),
        compiler_params=_cparams(3, FFN_VMEM_LIMIT),
        name="expert_ffn",
    )(idx, dst, h_rows, g_t, w_gate, w_up, w_down)


def _combine_kernel(ch_ref, tt_ref, flag_ref, x1_ref, cs_ref, ce_ref, c_ref, nw_ref, out_ref, acc_ref, *, cg, rows):
    k = pl.program_id(0)
    flag = flag_ref[k]

    @pl.when((flag & 2) > 0)
    def _():
        acc_ref[...] = x1_ref[...]

    @pl.when((flag & 1) > 0)
    def _():
        cio = (lax.broadcasted_iota(I32, (cg, LANES), 0) + ch_ref[k] * cg).astype(F32)
        parts = [jnp.where((cs_ref[r:r + 1, :] <= cio) & (cio < ce_ref[r:r + 1, :]), 1.0, 0.0) for r in range(rows)]
        a = jnp.concatenate(parts, axis=1).T.astype(BF16)
        chunk = jnp.concatenate([c_ref[pl.ds(s, cg, stride=TOKEN_ROWS), :].astype(BF16)
                                 for s in range(TOKEN_ROWS)], axis=1)
        acc_ref[...] += jnp.dot(a, chunk, preferred_element_type=F32)

    @pl.when((flag & 4) > 0)
    def _():
        out_ref[...] = _rms(acc_ref[...], nw_ref[...])


def _combine(ch, tt, flag, x1, cs3, ce3, contrib, final_w, tb, cg):
    T = x1.shape[0]
    rows = tb // LANES
    n_pairs = ch.shape[0]
    grid_spec = pltpu.PrefetchScalarGridSpec(
        num_scalar_prefetch=3,
        grid=(n_pairs,),
        in_specs=[pl.BlockSpec((tb, D_MODEL), lambda k, ch, tt, fl: (tt[k], 0)),
                  pl.BlockSpec((None, rows, LANES), lambda k, ch, tt, fl: (tt[k], 0, 0)),
                  pl.BlockSpec((None, rows, LANES), lambda k, ch, tt, fl: (tt[k], 0, 0)),
                  pl.BlockSpec((cg * TOKEN_ROWS, LANES), lambda k, ch, tt, fl: (ch[k], 0)),
                  pl.BlockSpec((1, D_MODEL), lambda k, ch, tt, fl: (0, 0))],
        out_specs=pl.BlockSpec((tb, D_MODEL), lambda k, ch, tt, fl: (tt[k], 0)),
        scratch_shapes=[pltpu.VMEM((tb, D_MODEL), F32)],
    )
    return pl.pallas_call(
        functools.partial(_combine_kernel, cg=cg, rows=rows),
        grid_spec=grid_spec,
        out_shape=jax.ShapeDtypeStruct((T, D_MODEL), F32),
        compiler_params=_cparams(1, VMEM_LIMIT),
        name="combine_final",
    )(ch, tt, flag, x1, cs3, ce3, contrib, final_w)


def _combine_schedule(cs, ce, tb, cg):
    T = cs.size
    ntt = T // tb
    total = CAPACITY_FACTOR * T
    nch = total // cg
    lo = cs.reshape(ntt, tb)[:, 0].astype(I32)
    hi = ce.reshape(ntt, tb)[:, -1].astype(I32)
    first = jnp.minimum(lo, total - 1) // cg
    last = jnp.where(hi > lo, (hi - 1) // cg, first)
    n = last - first + 1
    ends = jnp.cumsum(n)
    offs = ends - n
    n_pairs = ntt + nch
    ks = jnp.arange(n_pairs, dtype=I32)
    tile = jnp.minimum(jnp.searchsorted(ends, ks, side="right").astype(I32), ntt - 1)
    valid = ks < ends[-1]
    chunk = jnp.where(valid, first[tile] + ks - offs[tile], last[-1])
    flag = jnp.where(valid, 1 + 2 * (ks == offs[tile]) + 4 * (ks == ends[tile] - 1), 0).astype(I32)
    return chunk.astype(I32), tile, flag


def _t5_buckets(rel):
    nb = NUM_BUCKETS // 2
    ret = (rel > 0).astype(np.int32) * nb
    n = np.abs(rel)
    max_exact = nb // 2
    large = max_exact + (np.log(np.maximum(n, 1) / max_exact) / np.log(MAX_DISTANCE / max_exact)
                         * (nb - max_exact)).astype(np.int32)
    large = np.minimum(large, nb - 1)
    return ret + np.where(n < max_exact, n, large)


def _tile(n, pref):
    t = min(n, pref)
    assert n % t == 0
    return t


def _trunk(x, w):
    B, S, _ = x.shape
    T = B * S
    cap = CAPACITY_FACTOR * T // N_EXPERTS
    x2 = x.reshape(T, D_MODEL)
    tm = _tile(T, 512)

    qkv, main = _inproj(x2, w["norm1_w"], w["w_qkv"], w["w_main"], _tile(T, 1024))
    main3 = main.reshape(B, S, MAIN_COLS)
    attn = _attention(qkv.reshape(B, S, QKV_COLS), w["bias_tab"], w["attn_sink"], w["attn_norm_w"])
    yf = _ssd_pass(main3, w["conv_w"], w["conv_b"], w["dtb"], w["alog"], reverse=False)
    ssm = _ssd_pass(main3, w["conv_w"], w["conv_b"], w["dtb"], w["alog"], reverse=True,
                    yf=yf, dskip=w["dskip"], norm_w=w["ssm_norm_w"])
    x1, h_rows, aff = _outproj(x2, attn.reshape(T, D_ATTN), ssm.reshape(T, D_SSM), w["w_out"], w["norm2_w"],
                          w["router_w2"], tm)

    tb = _tile(T, 512)
    cg = 512
    idx, dst, g, cs, ce = _route(aff.T.reshape(N_EXPERTS, T // LANES, LANES), cap)
    tmf = _tile(cap, 1024)
    g_t = jnp.swapaxes(g.reshape(N_EXPERTS, cap // tmf, tmf // LANES, LANES), 2, 3)
    contrib = _ffn(idx.reshape(-1), dst.reshape(-1), h_rows, g_t, w["w_gate"], w["w_up"], w["w_down"], tmf, 512)
    ch, tt, flag = _combine_schedule(cs, ce, tb, cg)
    tile3 = lambda a: a.reshape(T // tb, tb // LANES, LANES)
    y = _combine(ch, tt, flag, x1, tile3(cs), tile3(ce), contrib, w["final_norm_w"], tb, cg)
    return y.reshape(B, S, D_MODEL)


def _prep_weights(rel_bias, norm1_w, w_in, conv_w, conv_b, dt_bias_fwd, dt_bias_bwd, a_log_fwd, a_log_bwd,
                  d_skip, ssm_norm_w, attn_sink, attn_norm_w, w_out, norm2_w, router_w, w_gate, w_up, w_down,
                  final_norm_w):
    o1 = D_ATTN; o2 = o1 + D_KV; o3 = o2 + D_KV; o4 = o3 + D_SSM; o5 = o4 + D_SSM + 2 * D_BC
    wi = w_in[0]
    scale = 1.0 / math.sqrt(HEAD_DIM)
    perm = np.concatenate([np.arange(h * HEAD_DIM, (h + 1) * HEAD_DIM) for h in ATTN_HEAD_ORDER])
    w_qkv = jnp.concatenate([(wi[:, :o1] * scale)[:, perm], wi[:, o1:o3]], axis=1).astype(BF16)
    w_out_p = jnp.concatenate([w_out[0][:D_ATTN][perm], w_out[0][D_ATTN:]], axis=0).astype(BF16)
    w_main = jnp.concatenate([wi[:, o3:], jnp.zeros((D_MODEL, LANES - 2 * SSM_HEADS), F32)], axis=1).astype(BF16)
    rel = np.arange(3 * BLOCK)[None, :] - BLOCK - np.arange(BLOCK)[:, None]
    onehot = (jnp.asarray(_t5_buckets(rel), I32)[..., None] == jnp.arange(NUM_BUCKETS, dtype=I32)).astype(F32)
    bias_tab = jnp.einsum("qsn,nh->hqs", onehot, rel_bias.astype(F32), precision=lax.Precision.HIGHEST)
    bias_tab = jnp.where(jnp.asarray(np.abs(rel) <= WINDOW)[None], bias_tab, NEG_INF)
    rw = router_w[0].astype(F32)
    rw_hi = rw.astype(BF16)
    router_w2 = jnp.concatenate([rw_hi, (rw - rw_hi.astype(F32)).astype(BF16)], axis=1)
    pad = jnp.zeros((LANES - 2 * SSM_HEADS,), F32)
    row = lambda a: a.reshape(1, -1).astype(F32)
    return dict(
        norm1_w=row(norm1_w[0]), w_qkv=w_qkv, w_main=w_main, bias_tab=bias_tab, attn_sink=attn_sink[0].astype(F32),
        attn_norm_w=row(attn_norm_w[0][perm]), conv_w=conv_w[0].T.astype(F32), conv_b=row(conv_b[0]),
        dtb=row(jnp.concatenate([dt_bias_fwd[0], dt_bias_bwd[0], pad])),
        alog=row(jnp.concatenate([a_log_fwd[0], a_log_bwd[0], pad])),
        dskip=row(jnp.repeat(d_skip[0], SSM_HEAD_DIM)), ssm_norm_w=row(ssm_norm_w[0]),
        w_out=w_out_p, norm2_w=row(norm2_w[0]), router_w2=router_w2,
        w_gate=w_gate[0].astype(BF16), w_up=w_up[0].astype(BF16), w_down=w_down[0].astype(BF16),
        final_norm_w=row(final_norm_w))


def kernel(x_prompt, x_sample, rel_bias, norm1_w, w_in, conv_w, conv_b, dt_bias_fwd, dt_bias_bwd, a_log_fwd, a_log_bwd, d_skip, ssm_norm_w, attn_sink, attn_norm_w, w_out, norm2_w, router_w, w_gate, w_up, w_down, final_norm_w):
    assert norm1_w.shape[0] == 1
    w = _prep_weights(rel_bias, norm1_w, w_in, conv_w, conv_b, dt_bias_fwd, dt_bias_bwd, a_log_fwd, a_log_bwd,
                      d_skip, ssm_norm_w, attn_sink, attn_norm_w, w_out, norm2_w, router_w, w_gate, w_up,
                      w_down, final_norm_w)
    return (_trunk(x_prompt, w), _trunk(x_sample, w))
```

```python
import functools
import math

import numpy as np
import jax
import jax.numpy as jnp
from jax import lax
from jax.experimental import pallas as pl
from jax.experimental.pallas import tpu as pltpu

F32 = jnp.float32
BF16 = jnp.bfloat16
I32 = jnp.int32

D_MODEL = 2048
HEAD_DIM = 64
N_HEADS = 16
N_KV_HEADS = 4
D_ATTN = 1024
D_KV = 256
WINDOW = 128
BLOCK = 128
NUM_BUCKETS = 32
MAX_DISTANCE = 128
SSM_HEAD_DIM = 64
SSM_HEADS = 16
D_SSM = 1024
SSM_STATE = 128
SSM_GROUPS = 2
HEADS_PER_GROUP = SSM_HEADS // SSM_GROUPS
D_BC = 256
CHUNK = 128
N_EXPERTS = 16
CAPACITY_FACTOR = 2
D_FF = 2048
EPS = 1e-6
NEG_INF = -1e30

LANES = 128
HALO_ROWS = 8
TOKEN_ROWS = D_MODEL // LANES

QKV_COLS = D_ATTN + 2 * D_KV
COL_QK, COL_QV = D_ATTN, D_ATTN + D_KV
COL_Z, COL_X, COL_B, COL_DT = 0, 1024, 2048, 2560
MAIN_COLS = COL_DT + LANES
MAIN_TN = 896

ATTN_HEAD_ORDER = tuple(8 * t + 4 * par + i for t in range(2) for i in range(4) for par in range(2))

SSD_BATCH_BLOCK = 4
ATTN_BATCH_BLOCK = 4

VMEM_LIMIT = 56 * 1024 * 1024
FFN_VMEM_LIMIT = 60 * 1024 * 1024


def _cparams(n_axes, vmem=None):
    return pltpu.CompilerParams(dimension_semantics=("arbitrary",) * n_axes,
                                vmem_limit_bytes=vmem)


def _rms(x, w):
    return x * lax.rsqrt(jnp.mean(x * x, axis=-1, keepdims=True) + EPS) * w


def _silu(x):
    return x * jax.nn.sigmoid(x)


def _inproj_kernel(x_ref, nw_ref, wq_ref, wr_ref, oq_ref, om_ref, hn_ref):
    j = pl.program_id(1)

    @pl.when(j == 0)
    def _():
        hn = _rms(x_ref[...], nw_ref[...]).astype(BF16)
        hn_ref[...] = hn
        oq_ref[...] = jnp.dot(hn, wq_ref[...], preferred_element_type=F32).astype(BF16)

    @pl.when(j > 0)
    def _():
        om_ref[...] = jnp.dot(hn_ref[...], wr_ref[...], preferred_element_type=F32)


def _inproj(x2, norm_w, w_qkv, w_main, tm):
    T = x2.shape[0]
    rest = lambda j: jnp.maximum(j - 1, 0)
    return pl.pallas_call(
        _inproj_kernel,
        grid=(T // tm, 1 + MAIN_COLS // MAIN_TN),
        in_specs=[pl.BlockSpec((tm, D_MODEL), lambda i, j: (i, 0)),
                  pl.BlockSpec((1, D_MODEL), lambda i, j: (0, 0)),
                  pl.BlockSpec((D_MODEL, QKV_COLS), lambda i, j: (0, 0), pipeline_mode=pl.Buffered(1)),
                  pl.BlockSpec((D_MODEL, MAIN_TN), lambda i, j: (0, rest(j)))],
        out_specs=[pl.BlockSpec((tm, QKV_COLS), lambda i, j: (i, 0)),
                   pl.BlockSpec((tm, MAIN_TN), lambda i, j: (i, rest(j)))],
        out_shape=[jax.ShapeDtypeStruct((T, QKV_COLS), BF16), jax.ShapeDtypeStruct((T, MAIN_COLS), F32)],
        scratch_shapes=[pltpu.VMEM((tm, D_MODEL), BF16)],
        compiler_params=_cparams(2, VMEM_LIMIT),
        name="inproj",
    )(x2, norm_w, w_qkv, w_main)


def _attn_kernel(q_ref, kp_ref, ko_ref, kn_ref, vp_ref, vo_ref, vn_ref, bias_ref, sink_ref, nw_ref,
                 o_ref, *, nb):
    i = pl.program_id(1)
    lane = lax.broadcasted_iota(I32, (3 * BLOCK, LANES), 1)
    m_lo = jnp.where(lane < HEAD_DIM, 1.0, 0.0).astype(BF16)
    m_up = jnp.where(lane < HEAD_DIM, 0.0, 1.0).astype(BF16)
    low_q = lax.broadcasted_iota(I32, (BLOCK, LANES), 1) < HEAD_DIM
    top = lax.broadcasted_iota(I32, (2 * BLOCK, 1), 0) < BLOCK
    tiles_per_pair = D_ATTN // LANES // (D_KV // LANES)

    def body(bb, edge):
        tiles = []
        for t in range(D_KV // LANES):
            sl = slice(t * LANES, (t + 1) * LANES)
            kt = jnp.concatenate([kp_ref[bb, :, sl], ko_ref[bb, :, sl], kn_ref[bb, :, sl]], axis=0)
            vt = jnp.concatenate([vp_ref[bb, :, sl], vo_ref[bb, :, sl], vn_ref[bb, :, sl]], axis=0)
            rhs = [(kt * m_lo, vt * m_lo + m_up), (kt * m_up, vt * m_up + m_lo)]
            for j in range(0, tiles_per_pair, 2):
                qa, qb = t * tiles_per_pair + j, t * tiles_per_pair + j + 1
                q2 = jnp.concatenate([q_ref[bb, :, qa * LANES:(qa + 1) * LANES],
                                      q_ref[bb, :, qb * LANES:(qb + 1) * LANES]], axis=0)
                res = []
                for par in range(2):
                    kz, vz = rhs[par]
                    ha, hb = ATTN_HEAD_ORDER[2 * qa + par], ATTN_HEAD_ORDER[2 * qb + par]
                    s = lax.dot_general(q2, kz, (((1,), (1,)), ((), ())), preferred_element_type=F32)
                    s = s + jnp.concatenate([bias_ref[ha], bias_ref[hb]], axis=0) + edge
                    sk = jnp.where(top, sink_ref[ha], sink_ref[hb])
                    m = jnp.maximum(jnp.max(s, axis=-1, keepdims=True), sk)
                    p = jnp.exp(s - m).astype(BF16)
                    pv = jnp.dot(p, vz, preferred_element_type=F32)
                    den = (pv[:, LANES - 1:LANES] if par == 0 else pv[:, 0:1]) + jnp.exp(sk - m)
                    res.append(pv * (1.0 / den))
                tiles.append(jnp.where(low_q, res[0][:BLOCK], res[1][:BLOCK]))
                tiles.append(jnp.where(low_q, res[0][BLOCK:], res[1][BLOCK:]))
        o = jnp.concatenate(tiles, axis=-1)
        o_ref[bb] = _rms(o, nw_ref[...]).astype(BF16)

    kcol = lax.broadcasted_iota(I32, (1, 3 * BLOCK), 1)
    edge = (jnp.where((kcol < BLOCK) & (i == 0), NEG_INF, 0.0)
            + jnp.where((kcol >= 2 * BLOCK) & (i == nb - 1), NEG_INF, 0.0))
    for bb in range(q_ref.shape[0]):
        body(bb, edge)


def _attention(qkv3, bias_tab, sink, norm_w):
    B, S, _ = qkv3.shape
    nb = S // BLOCK
    kcol, vcol = COL_QK // D_KV, COL_QV // D_KV
    prev = lambda i: jnp.maximum(i - 1, 0)
    nxt = lambda i: jnp.minimum(i + 1, nb - 1)
    nbt = ATTN_BATCH_BLOCK if B % ATTN_BATCH_BLOCK == 0 else 1
    kv = lambda col, f: pl.BlockSpec((nbt, BLOCK, D_KV), lambda b, i: (b, f(i), col))
    same = lambda i: i
    return pl.pallas_call(
        functools.partial(_attn_kernel, nb=nb),
        grid=(B // nbt, nb),
        in_specs=[pl.BlockSpec((nbt, BLOCK, D_ATTN), lambda b, i: (b, i, 0)),
                  kv(kcol, prev), kv(kcol, same), kv(kcol, nxt),
                  kv(vcol, prev), kv(vcol, same), kv(vcol, nxt),
                  pl.BlockSpec((N_HEADS, BLOCK, 3 * BLOCK), lambda b, i: (0, 0, 0)),
                  pl.BlockSpec(memory_space=pltpu.SMEM),
                  pl.BlockSpec((1, D_ATTN), lambda b, i: (0, 0))],
        out_specs=pl.BlockSpec((nbt, BLOCK, D_ATTN), lambda b, i: (b, i, 0)),
        out_shape=jax.ShapeDtypeStruct((B, S, D_ATTN), BF16),
        compiler_params=_cparams(2),
        name="attention",
    )(qkv3, qkv3, qkv3, qkv3, qkv3, qkv3, qkv3, bias_tab, sink, norm_w)


def _conv_silu(u, prev8, next8, w, b, has_prev, has_next):
    prev_row = jnp.where(has_prev, prev8[HALO_ROWS - 1:HALO_ROWS, :], 0.0)
    next_row = jnp.where(has_next, next8[0:1, :], 0.0)
    rid = lax.broadcasted_iota(I32, u.shape, 0)
    um = jnp.where(rid == 0, prev_row, pltpu.roll(u, 1, 0))
    up = jnp.where(rid == CHUNK - 1, next_row, pltpu.roll(u, CHUNK - 1, 0))
    y = b + um * w[0:1, :] + u * w[1:2, :] + up * w[2:3, :]
    return _silu(y)


def _ssd_kernel(*refs, reverse, nc):
    h_ref = refs[-1]

    @pl.when(pl.program_id(1) == 0)
    def _():
        h_ref[...] = jnp.zeros_like(h_ref)

    for bb in range(h_ref.shape[0]):
        _ssd_chunk(bb, refs, reverse, nc)


def _ssd_chunk(bb, refs, reverse, nc):
    if reverse:
        (z_ref, yf_ref, x_ref, xp_ref, xn_ref, bc_ref, bcp_ref, bcn_ref, dt_ref, cw_ref, cb_ref, dtb_ref,
         alog_ref, dskip_ref, nw_ref, o_ref, h_ref) = refs
    else:
        (x_ref, xp_ref, xn_ref, bc_ref, bcp_ref, bcn_ref, dt_ref, cw_ref, cb_ref, dtb_ref,
         alog_ref, o_ref, h_ref) = refs
    step = pl.program_id(1)
    c = (nc - 1 - step) if reverse else step
    has_prev, has_next = c > 0, c < nc - 1
    cw, cb = cw_ref[...], cb_ref[...]
    xc = _conv_silu(x_ref[bb], xp_ref[bb], xn_ref[bb], cw[:, :D_SSM], cb[:, :D_SSM], has_prev, has_next)
    bcv = _conv_silu(bc_ref[bb], bcp_ref[bb], bcn_ref[bb], cw[:, D_SSM:], cb[:, D_SSM:], has_prev, has_next)

    raw = dt_ref[bb] + dtb_ref[...]
    dt = jnp.maximum(raw, 0.0) + jnp.log1p(jnp.exp(-jnp.abs(raw)))
    a = dt * (-jnp.exp(alog_ref[...]))
    li = lax.broadcasted_iota(I32, (CHUNK, CHUNK), 0)
    si = lax.broadcasted_iota(I32, (CHUNK, CHUNK), 1)
    incl = jnp.dot(jnp.where(li >= si, 1.0, 0.0).astype(F32), a, precision=lax.Precision.HIGHEST,
                   preferred_element_type=F32)
    tot = incl[CHUNK - 1:CHUNK, :]
    if reverse:
        pcs = incl - a
        dstate = jnp.exp(pcs)
        yscale = jnp.exp(tot - pcs)
        mask = si >= li
    else:
        pcs = incl
        dstate = jnp.exp(tot - pcs)
        yscale = jnp.exp(pcs)
        mask = li >= si
    pcs_t = pcs.T
    lane0 = SSM_HEADS if reverse else 0
    P = SSM_HEAD_DIM
    gw = HEADS_PER_GROUP * P

    def spread(x, width, pieces):
        k = lax.broadcasted_iota(I32, (LANES, SSM_HEADS * width), 0)
        c = lax.broadcasted_iota(I32, (LANES, SSM_HEADS * width), 1)
        sel = jnp.where(k == lane0 + c // width, 1.0, 0.0).astype(BF16)
        out, rem = None, x
        for _ in range(pieces):
            piece = rem.astype(BF16)
            rem = rem - piece.astype(F32)
            d = jnp.dot(piece, sel, preferred_element_type=F32)
            out = d if out is None else out + d
        return out

    scales = spread(jnp.concatenate([dt, dstate, yscale, jnp.broadcast_to(jnp.exp(tot), (HALO_ROWS, LANES))], axis=0),
                    P, 2)
    dt_e, ds_e, ys_e = scales[:CHUNK], scales[CHUNK:2 * CHUNK], scales[2 * CHUNK:3 * CHUNK]
    cdec_e = scales[3 * CHUNK:3 * CHUNK + 1]
    col_e = spread(pcs, CHUNK, 3)
    xdt = xc * dt_e
    xdt_b = xdt.astype(BF16)
    xs_b = (xdt * ds_e).astype(BF16)
    low = lax.broadcasted_iota(I32, (CHUNK, LANES), 1) < P

    y_groups = []
    for g in range(SSM_GROUPS):
        bg = bcv[:, g * SSM_STATE:(g + 1) * SSM_STATE]
        cg = bcv[:, D_BC + g * SSM_STATE:D_BC + (g + 1) * SSM_STATE].astype(BF16)
        cbm = lax.dot_general(cg, bg.astype(BF16), (((1,), (1,)), ((), ())), preferred_element_type=F32)
        hg = h_ref[bb, g]
        yoff = jnp.dot(cg, hg.astype(BF16), preferred_element_type=F32)
        yd = []
        for jp in range(HEADS_PER_GROUP // 2):
            tile = (g * HEADS_PER_GROUP) // 2 + jp
            xpair = xdt_b[:, tile * LANES:(tile + 1) * LANES]
            halves = []
            for par in range(2):
                j = 2 * tile + par
                ln = lane0 + j
                col = col_e[:, j * CHUNK:(j + 1) * CHUNK]
                row = pcs_t[ln:ln + 1, :]
                seg = (row - col) if reverse else (col - row)
                lm = jnp.exp(jnp.where(mask, seg, -jnp.inf))
                halves.append(jnp.dot((cbm * lm).astype(BF16), xpair, preferred_element_type=F32))
            yd.append(jnp.where(low, halves[0], halves[1]))
        sl = slice(g * gw, (g + 1) * gw)
        y_groups.append(jnp.concatenate(yd, axis=1) + yoff * ys_e[:, sl])
        snew = jnp.dot(bg.T.astype(BF16), xs_b[:, sl], preferred_element_type=F32)
        h_ref[bb, g] = hg * cdec_e[:, sl] + snew
    y = jnp.concatenate(y_groups, axis=1)

    if not reverse:
        o_ref[bb] = y
    else:
        ytot = yf_ref[bb] + y + dskip_ref[...] * xc
        yz = ytot * _silu(z_ref[bb])
        halves = []
        for g in range(SSM_GROUPS):
            seg = yz[:, g * gw:(g + 1) * gw]
            halves.append(seg * lax.rsqrt(jnp.mean(seg * seg, axis=-1, keepdims=True) + EPS))
        o_ref[bb] = (jnp.concatenate(halves, axis=1) * nw_ref[...]).astype(BF16)


def _ssd_pass(proj3, conv_w3, conv_b, dtb, alog, reverse, yf=None, dskip=None, norm_w=None):
    B, S, _ = proj3.shape
    nc = S // CHUNK
    hp = CHUNK // HALO_ROWS
    nh = S // HALO_ROWS
    ch = (lambda s: nc - 1 - s) if reverse else (lambda s: s)
    nbt = SSD_BATCH_BLOCK if B % SSD_BATCH_BLOCK == 0 else 1
    main = lambda w, col: pl.BlockSpec((nbt, CHUNK, w), lambda b, s: (b, ch(s), col // w))
    halo_p = lambda w, col: pl.BlockSpec((nbt, HALO_ROWS, w),
                                         lambda b, s: (b, jnp.maximum(ch(s) * hp - 1, 0), col // w))
    halo_n = lambda w, col: pl.BlockSpec((nbt, HALO_ROWS, w),
                                         lambda b, s: (b, jnp.minimum((ch(s) + 1) * hp, nh - 1), col // w))
    full = lambda a: pl.BlockSpec(a.shape, lambda b, s: (0,) * a.ndim)
    d_bc2 = 2 * D_BC
    in_specs = [main(D_SSM, COL_X), halo_p(D_SSM, COL_X), halo_n(D_SSM, COL_X),
                main(d_bc2, COL_B), halo_p(d_bc2, COL_B), halo_n(d_bc2, COL_B),
                main(LANES, COL_DT), full(conv_w3), full(conv_b), full(dtb), full(alog)]
    args = [proj3, proj3, proj3, proj3, proj3, proj3, proj3, conv_w3, conv_b, dtb, alog]
    if reverse:
        in_specs = [main(D_SSM, COL_Z), pl.BlockSpec((nbt, CHUNK, D_SSM), lambda b, s: (b, ch(s), 0))] + in_specs
        in_specs += [full(dskip), full(norm_w)]
        args = [proj3, yf] + args + [dskip, norm_w]
    return pl.pallas_call(
        functools.partial(_ssd_kernel, reverse=reverse, nc=nc),
        grid=(B // nbt, nc),
        in_specs=in_specs,
        out_specs=pl.BlockSpec((nbt, CHUNK, D_SSM), lambda b, s: (b, ch(s), 0)),
        out_shape=jax.ShapeDtypeStruct((B, S, D_SSM), BF16 if reverse else F32),
        scratch_shapes=[pltpu.VMEM((nbt, SSM_GROUPS, SSM_STATE, HEADS_PER_GROUP * SSM_HEAD_DIM), F32)],
        compiler_params=_cparams(2),
        name="ssd_bwd" if reverse else "ssd_fwd",
    )(*args)


def _outproj_kernel(x_ref, a_ref, s_ref, w_ref, n2_ref, rw2_ref, x1_ref, h_ref, aff_ref):
    tm = x_ref.shape[0]
    x1 = (x_ref[...]
          + jnp.dot(a_ref[...], w_ref[:D_ATTN, :], preferred_element_type=F32)
          + jnp.dot(s_ref[...], w_ref[D_ATTN:, :], preferred_element_type=F32))
    x1_ref[...] = x1
    hn = _rms(x1, n2_ref[...])
    for s in range(TOKEN_ROWS):
        h_ref[pl.ds(s, tm, stride=TOKEN_ROWS), :] = hn[:, s * LANES:(s + 1) * LANES]
    hn_hi = hn.astype(BF16)
    hn_lo = (hn - hn_hi.astype(F32)).astype(BF16)
    l_hi = jnp.dot(hn_hi, rw2_ref[...], preferred_element_type=F32)
    l_lo = jnp.dot(hn_lo, rw2_ref[:, :N_EXPERTS], preferred_element_type=F32)
    logits = l_hi[:, :N_EXPERTS] + l_hi[:, N_EXPERTS:] + l_lo
    e = jnp.exp(logits - jnp.max(logits, axis=-1, keepdims=True))
    aff_ref[...] = e / jnp.sum(e, axis=-1, keepdims=True)


def _outproj(x2, attn2, ssm2, w_out, norm2_w, router_w, tm):
    T = x2.shape[0]
    row = lambda w: pl.BlockSpec((tm, w), lambda i: (i, 0))
    full = lambda a: pl.BlockSpec(a.shape, lambda i: (0,) * a.ndim)
    return pl.pallas_call(
        _outproj_kernel,
        grid=(T // tm,),
        in_specs=[row(D_MODEL), row(D_ATTN), row(D_SSM), full(w_out), full(norm2_w), full(router_w)],
        out_specs=[row(D_MODEL), pl.BlockSpec((tm * TOKEN_ROWS, LANES), lambda i: (i, 0)), row(N_EXPERTS)],
        out_shape=[jax.ShapeDtypeStruct((T, D_MODEL), F32),
                   jax.ShapeDtypeStruct((T * TOKEN_ROWS, LANES), F32),
                   jax.ShapeDtypeStruct((T, N_EXPERTS), F32)],
        compiler_params=_cparams(1, VMEM_LIMIT),
        name="outproj_router",
    )(x2, attn2, ssm2, w_out, norm2_w, router_w)


def _count(m):
    c = jnp.sum(jnp.where(m, 1.0, 0.0), axis=0, keepdims=True)
    return jnp.sum(c, axis=1, keepdims=True)


def _tri(n, m, fn):
    return jnp.where(fn(lax.broadcasted_iota(I32, (n, m), 0), lax.broadcasted_iota(I32, (n, m), 1)),
                     1.0, 0.0).astype(BF16)


def _dot_u16(lhs01, x):
    hi = jnp.floor(x * (1.0 / 256.0))
    lo = x - hi * 256.0
    return (jnp.dot(lhs01, hi.astype(BF16), preferred_element_type=F32) * 256.0
            + jnp.dot(lhs01, lo.astype(BF16), preferred_element_type=F32))


def _cumsum_rowmajor(x):
    R = x.shape[0]
    within = jnp.dot(x.astype(BF16), _tri(LANES, LANES, lambda k, l: k <= l), preferred_element_type=F32)
    rowtot = jnp.broadcast_to(within[:, LANES - 1:LANES], (R, LANES))
    before = _dot_u16(_tri(R, R, lambda i, k: k < i), rowtot)
    return within + before


def _route_kernel(a_ref, idx_ref, dst_ref, g_ref, cs_ref, ce_ref, sel_s, cnt_s, cs_s, rank_s, *, cap):
    s = pl.program_id(0)
    R = a_ref.shape[0]

    @pl.when(s < N_EXPERTS)
    def _select():
        bits = pltpu.bitcast(a_ref[...], I32)
        capf = jnp.float32(cap)

        def body(k, tau):
            cand = tau | lax.shift_left(jnp.int32(1), 30 - k)
            return jnp.where(_count(bits >= cand) >= capf, cand, tau)

        tau = lax.fori_loop(0, 31, body, jnp.zeros((1, 1), I32))
        gt = bits > tau
        eq = bits == tau
        need = capf - _count(gt)
        ties = _cumsum_rowmajor(jnp.where(eq, 1.0, 0.0))
        sel = jnp.where(gt | (eq & (ties <= need)), 1.0, 0.0)
        sel_s[s] = sel

        @pl.when(s == 0)
        def _():
            cnt_s[...] = sel

        @pl.when(s > 0)
        def _():
            cnt_s[...] += sel

    @pl.when(s == N_EXPERTS)
    def _prefix():
        cnt = cnt_s[...]
        ce = _cumsum_rowmajor(cnt)
        cs_s[...] = ce - cnt
        rank_s[...] = jnp.zeros_like(rank_s)
        cs_ref[...] = ce - cnt
        ce_ref[...] = ce

    @pl.when(s >= N_EXPERTS)
    def _invert():
        e = s - N_EXPERTS
        sel = sel_s[e]
        rank = rank_s[...]
        q = cs_s[...] + rank
        rank_s[...] = rank + sel
        a = a_ref[...]
        within = jnp.dot(sel.astype(BF16), _tri(LANES, LANES, lambda k, l: k <= l), preferred_element_type=F32)
        n_b = jnp.broadcast_to(within[:, LANES - 1:LANES], (R, LANES))
        rowend = jnp.dot(_tri(R, R, lambda i, k: k <= i), n_b.astype(BF16), preferred_element_type=F32)
        slot = lax.broadcasted_iota(I32, (R, cap), 1).astype(F32)
        done = jnp.where(rowend[:, 0:1] <= slot, 1.0, 0.0)
        ones = jnp.ones((8, R), BF16)
        row_p = jnp.dot(ones, done.astype(BF16), preferred_element_type=F32)[0:1]
        start_p = jnp.dot(ones, (done * n_b[:, 0:1]).astype(BF16), preferred_element_type=F32)[0:1]
        onehot = jnp.where(lax.broadcasted_iota(I32, (R, cap), 0).astype(F32) == row_p, 1.0, 0.0).astype(BF16)
        w_t = jnp.dot(within.T.astype(BF16), onehot, preferred_element_type=F32)
        k_in_row = lax.broadcasted_iota(I32, (1, cap), 1).astype(F32) - start_p
        lane_p = jnp.sum(jnp.where(w_t <= k_in_row, 1.0, 0.0), axis=0, keepdims=True)
        pick = lax.broadcasted_iota(I32, (LANES, cap), 0).astype(F32) == lane_p

        def take(x):
            v = jnp.dot(x.T.astype(BF16), onehot, preferred_element_type=F32)
            return jnp.sum(jnp.where(pick, v, 0.0), axis=0, keepdims=True)

        q_hi = jnp.floor(q * (1.0 / 256.0))
        a_hi = a.astype(BF16).astype(F32)
        a_mid = (a - a_hi).astype(BF16).astype(F32)
        a_lo = a - a_hi - a_mid
        idx_ref[...] = (row_p * LANES + lane_p).astype(I32)
        dst_ref[...] = (take(q_hi) * 256.0 + take(q - q_hi * 256.0)).astype(I32)
        g_ref[...] = take(a_hi) + take(a_mid) + take(a_lo)


def _route(aff_t3, cap):
    E, R, _ = aff_t3.shape
    assert R <= 256 and E == N_EXPERTS
    slot_spec = pl.BlockSpec((None, 1, cap), lambda s: (jnp.maximum(s - N_EXPERTS, 0), 0, 0))
    tok_spec = pl.BlockSpec((R, LANES), lambda s: (0, 0))
    return pl.pallas_call(
        functools.partial(_route_kernel, cap=cap),
        grid=(2 * E,),
        in_specs=[pl.BlockSpec((None, R, LANES), lambda s: (s % N_EXPERTS, 0, 0))],
        out_specs=[slot_spec, slot_spec, slot_spec, tok_spec, tok_spec],
        out_shape=[jax.ShapeDtypeStruct((E, 1, cap), I32), jax.ShapeDtypeStruct((E, 1, cap), I32),
                   jax.ShapeDtypeStruct((E, 1, cap), F32),
                   jax.ShapeDtypeStruct((R, LANES), F32), jax.ShapeDtypeStruct((R, LANES), F32)],
        scratch_shapes=[pltpu.VMEM((E, R, LANES), F32), pltpu.VMEM((R, LANES), F32),
                        pltpu.VMEM((R, LANES), F32), pltpu.VMEM((R, LANES), F32)],
        compiler_params=_cparams(1, VMEM_LIMIT),
        name="route",
    )(aff_t3)


def _ffn_kernel(idx_ref, dst_ref, h_hbm, gt_ref, wg_ref, wu_ref, wd_ref, c_hbm,
                xs_stage, xsb, hid, o_stage, gsem, ssem, *, tm, cap, nf):
    e, m, f = pl.program_id(0), pl.program_id(1), pl.program_id(2)
    base = e * cap + m * tm
    tr = TOKEN_ROWS
    part = tm // nf
    tf = wg_ref.shape[1]
    tn = wd_ref.shape[1]
    is_first = (e == 0) & (m == 0)
    is_last = (e == pl.num_programs(0) - 1) & (m == pl.num_programs(1) - 1)

    def token_rows(t):
        return pl.ds(pl.multiple_of(t * tr, tr), tr)

    def gather_rows(tile_base, r0, n):
        def issue(r, carry):
            pltpu.make_async_copy(h_hbm.at[token_rows(idx_ref[tile_base + r0 + r]), :],
                                  xs_stage.at[token_rows(r0 + r), :], gsem).start()
            return carry

        lax.fori_loop(0, n, issue, 0, unroll=8)

    def scatter_rows(tile_base, r0, n):
        def issue(r, carry):
            pltpu.make_async_copy(o_stage.at[token_rows(r0 + r), :],
                                  c_hbm.at[token_rows(dst_ref[tile_base + r0 + r]), :], ssem).start()
            return carry

        lax.fori_loop(0, n, issue, 0, unroll=8)

    def scatter_done():
        pltpu.make_async_copy(o_stage, c_hbm.at[pl.ds(0, tm * tr), :], ssem).wait()

    def gather_done():
        pltpu.make_async_copy(h_hbm.at[pl.ds(0, tm * tr), :], xs_stage, gsem).wait()

    @pl.when(f == 0)
    def _stage_in():
        @pl.when(is_first)
        def _():
            gather_rows(base, 0, tm)
            o_stage[...] = jnp.zeros_like(o_stage)

        gather_done()
        for s in range(tr):
            xsb[:, s * LANES:(s + 1) * LANES] = xs_stage[pl.ds(s, tm, stride=tr), :].astype(BF16)

    nxt = jnp.where(is_last, base, base + tm)
    prv = jnp.maximum(base - tm, 0)

    for k in range(nf):
        @pl.when(f == k)
        def _gate_up(k=k):
            xs = xsb[...]
            g = jnp.dot(xs, wg_ref[...].astype(BF16), preferred_element_type=F32)
            u = jnp.dot(xs, wu_ref[...].astype(BF16), preferred_element_type=F32)
            hid[:, k * tf:(k + 1) * tf] = (_silu(g) * u).astype(BF16)
            for r in range(k * part, (k + 1) * part):
                pltpu.make_async_copy(h_hbm.at[token_rows(idx_ref[nxt + r]), :],
                                      xs_stage.at[token_rows(r), :], gsem).start()
                pltpu.make_async_copy(o_stage.at[token_rows(r), :],
                                      c_hbm.at[token_rows(dst_ref[prv + r]), :], ssem).start()

    for n in range(nf):
        @pl.when(f == nf + n)
        def _down(n=n):
            if n == 0:
                scatter_done()

            out = jnp.dot(hid[...], wd_ref[...].astype(BF16), preferred_element_type=F32)
            for j in range(tm // LANES):
                rows = out[j * LANES:(j + 1) * LANES, :] * gt_ref[:, j:j + 1]
                for c in range(tn // LANES):
                    o_stage[pl.ds(j * LANES * tr + n * (tn // LANES) + c, LANES, stride=tr), :] = (
                        rows[:, c * LANES:(c + 1) * LANES])

            if n == nf - 1:
                @pl.when(is_last)
                def _():
                    scatter_rows(base, 0, tm)
                    scatter_done()
                    gather_done()


def _ffn(idx, dst, h_rows, g_t, w_gate, w_up, w_down, tm, tf):
    E, n_m = g_t.shape[0], g_t.shape[1]
    cap = n_m * tm
    n_contrib = E * cap
    nf = D_FF // tf
    tn = D_MODEL // nf
    up = lambda f: jnp.minimum(f, nf - 1)
    down = lambda f: jnp.maximum(f - nf, 0)
    grid_spec = pltpu.PrefetchScalarGridSpec(
        num_scalar_prefetch=2,
        grid=(E, n_m, 2 * nf),
        in_specs=[pl.BlockSpec(memory_space=pl.ANY),
                  pl.BlockSpec((None, None, LANES, tm // LANES), lambda e, m, f, i, d: (e, m, 0, 0)),
                  pl.BlockSpec((None, D_MODEL, tf), lambda e, m, f, i, d: (e, 0, up(f))),
                  pl.BlockSpec((None, D_MODEL, tf), lambda e, m, f, i, d: (e, 0, up(f))),
                  pl.BlockSpec((None, D_FF, tn), lambda e, m, f, i, d: (e, 0, down(f)))],
        out_specs=pl.BlockSpec(memory_space=pl.ANY),
        scratch_shapes=[pltpu.VMEM((tm * TOKEN_ROWS, LANES), F32), pltpu.VMEM((tm, D_MODEL), BF16),
                        pltpu.VMEM((tm, D_FF), BF16), pltpu.VMEM((tm * TOKEN_ROWS, LANES), F32),
                        pltpu.SemaphoreType.DMA, pltpu.SemaphoreType.DMA],
    )
    return pl.pallas_call(
        functools.partial(_ffn_kernel, tm=tm, cap=cap, nf=nf),
        grid_spec=grid_spec,
        out_shape=jax.ShapeDtypeStruct((n_contrib * TOKEN_ROWS, LANES), F32),
        compiler_params=_cparams(3, FFN_VMEM_LIMIT),
        name="expert_ffn",
    )(idx, dst, h_rows, g_t, w_gate, w_up, w_down)


def _combine_kernel(ch_ref, tt_ref, flag_ref, x1_ref, cs_ref, ce_ref, c_ref, nw_ref, out_ref, acc_ref, *, cg, rows):
    k = pl.program_id(0)
    flag = flag_ref[k]

    @pl.when((flag & 2) > 0)
    def _():
        acc_ref[...] = x1_ref[...]

    @pl.when((flag & 1) > 0)
    def _():
        cio = (lax.broadcasted_iota(I32, (cg, LANES), 0) + ch_ref[k] * cg).astype(F32)
        parts = [jnp.where((cs_ref[r:r + 1, :] <= cio) & (cio < ce_ref[r:r + 1, :]), 1.0, 0.0) for r in range(rows)]
        a = jnp.concatenate(parts, axis=1).T.astype(BF16)
        chunk = jnp.concatenate([c_ref[pl.ds(s, cg, stride=TOKEN_ROWS), :].astype(BF16)
                                 for s in range(TOKEN_ROWS)], axis=1)
        acc_ref[...] += jnp.dot(a, chunk, preferred_element_type=F32)

    @pl.when((flag & 4) > 0)
    def _():
        out_ref[...] = _rms(acc_ref[...], nw_ref[...])


def _combine(ch, tt, flag, x1, cs3, ce3, contrib, final_w, tb, cg):
    T = x1.shape[0]
    rows = tb // LANES
    n_pairs = ch.shape[0]
    grid_spec = pltpu.PrefetchScalarGridSpec(
        num_scalar_prefetch=3,
        grid=(n_pairs,),
        in_specs=[pl.BlockSpec((tb, D_MODEL), lambda k, ch, tt, fl: (tt[k], 0)),
                  pl.BlockSpec((None, rows, LANES), lambda k, ch, tt, fl: (tt[k], 0, 0)),
                  pl.BlockSpec((None, rows, LANES), lambda k, ch, tt, fl: (tt[k], 0, 0)),
                  pl.BlockSpec((cg * TOKEN_ROWS, LANES), lambda k, ch, tt, fl: (ch[k], 0)),
                  pl.BlockSpec((1, D_MODEL), lambda k, ch, tt, fl: (0, 0))],
        out_specs=pl.BlockSpec((tb, D_MODEL), lambda k, ch, tt, fl: (tt[k], 0)),
        scratch_shapes=[pltpu.VMEM((tb, D_MODEL), F32)],
    )
    return pl.pallas_call(
        functools.partial(_combine_kernel, cg=cg, rows=rows),
        grid_spec=grid_spec,
        out_shape=jax.ShapeDtypeStruct((T, D_MODEL), F32),
        compiler_params=_cparams(1, VMEM_LIMIT),
        name="combine_final",
    )(ch, tt, flag, x1, cs3, ce3, contrib, final_w)


def _combine_schedule(cs, ce, tb, cg):
    T = cs.size
    ntt = T // tb
    total = CAPACITY_FACTOR * T
    nch = total // cg
    lo = cs.reshape(ntt, tb)[:, 0].astype(I32)
    hi = ce.reshape(ntt, tb)[:, -1].astype(I32)
    first = jnp.minimum(lo, total - 1) // cg
    last = jnp.where(hi > lo, (hi - 1) // cg, first)
    n = last - first + 1
    ends = jnp.cumsum(n)
    offs = ends - n
    n_pairs = ntt + nch
    ks = jnp.arange(n_pairs, dtype=I32)
    tile = jnp.minimum(jnp.searchsorted(ends, ks, side="right").astype(I32), ntt - 1)
    valid = ks < ends[-1]
    chunk = jnp.where(valid, first[tile] + ks - offs[tile], last[-1])
    flag = jnp.where(valid, 1 + 2 * (ks == offs[tile]) + 4 * (ks == ends[tile] - 1), 0).astype(I32)
    return chunk.astype(I32), tile, flag


def _t5_buckets(rel):
    nb = NUM_BUCKETS // 2
    ret = (rel > 0).astype(np.int32) * nb
    n = np.abs(rel)
    max_exact = nb // 2
    large = max_exact + (np.log(np.maximum(n, 1) / max_exact) / np.log(MAX_DISTANCE / max_exact)
                         * (nb - max_exact)).astype(np.int32)
    large = np.minimum(large, nb - 1)
    return ret + np.where(n < max_exact, n, large)


def _tile(n, pref):
    t = min(n, pref)
    assert n % t == 0
    return t


def _trunk(x, w):
    B, S, _ = x.shape
    T = B * S
    cap = CAPACITY_FACTOR * T // N_EXPERTS
    x2 = x.reshape(T, D_MODEL)
    tm = _tile(T, 512)

    qkv, main = _inproj(x2, w["norm1_w"], w["w_qkv"], w["w_main"], _tile(T, 1024))
    main3 = main.reshape(B, S, MAIN_COLS)
    attn = _attention(qkv.reshape(B, S, QKV_COLS), w["bias_tab"], w["attn_sink"], w["attn_norm_w"])
    yf = _ssd_pass(main3, w["conv_w"], w["conv_b"], w["dtb"], w["alog"], reverse=False)
    ssm = _ssd_pass(main3, w["conv_w"], w["conv_b"], w["dtb"], w["alog"], reverse=True,
                    yf=yf, dskip=w["dskip"], norm_w=w["ssm_norm_w"])
    x1, h_rows, aff = _outproj(x2, attn.reshape(T, D_ATTN), ssm.reshape(T, D_SSM), w["w_out"], w["norm2_w"],
                          w["router_w2"], tm)

    tb = _tile(T, 512)
    cg = 512
    idx, dst, g, cs, ce = _route(aff.T.reshape(N_EXPERTS, T // LANES, LANES), cap)
    tmf = _tile(cap, 1024)
    g_t = jnp.swapaxes(g.reshape(N_EXPERTS, cap // tmf, tmf // LANES, LANES), 2, 3)
    contrib = _ffn(idx.reshape(-1), dst.reshape(-1), h_rows, g_t, w["w_gate"], w["w_up"], w["w_down"], tmf, 512)
    ch, tt, flag = _combine_schedule(cs, ce, tb, cg)
    tile3 = lambda a: a.reshape(T // tb, tb // LANES, LANES)
    y = _combine(ch, tt, flag, x1, tile3(cs), tile3(ce), contrib, w["final_norm_w"], tb, cg)
    return y.reshape(B, S, D_MODEL)


def _prep_weights(rel_bias, norm1_w, w_in, conv_w, conv_b, dt_bias_fwd, dt_bias_bwd, a_log_fwd, a_log_bwd,
                  d_skip, ssm_norm_w, attn_sink, attn_norm_w, w_out, norm2_w, router_w, w_gate, w_up, w_down,
                  final_norm_w):
    o1 = D_ATTN; o2 = o1 + D_KV; o3 = o2 + D_KV; o4 = o3 + D_SSM; o5 = o4 + D_SSM + 2 * D_BC
    wi = w_in[0]
    scale = 1.0 / math.sqrt(HEAD_DIM)
    perm = np.concatenate([np.arange(h * HEAD_DIM, (h + 1) * HEAD_DIM) for h in ATTN_HEAD_ORDER])
    w_qkv = jnp.concatenate([(wi[:, :o1] * scale)[:, perm], wi[:, o1:o3]], axis=1).astype(BF16)
    w_out_p = jnp.concatenate([w_out[0][:D_ATTN][perm], w_out[0][D_ATTN:]], axis=0).astype(BF16)
    w_main = jnp.concatenate([wi[:, o3:], jnp.zeros((D_MODEL, LANES - 2 * SSM_HEADS), F32)], axis=1).astype(BF16)
    rel = np.arange(3 * BLOCK)[None, :] - BLOCK - np.arange(BLOCK)[:, None]
    onehot = (jnp.asarray(_t5_buckets(rel), I32)[..., None] == jnp.arange(NUM_BUCKETS, dtype=I32)).astype(F32)
    bias_tab = jnp.einsum("qsn,nh->hqs", onehot, rel_bias.astype(F32), precision=lax.Precision.HIGHEST)
    bias_tab = jnp.where(jnp.asarray(np.abs(rel) <= WINDOW)[None], bias_tab, NEG_INF)
    rw = router_w[0].astype(F32)
    rw_hi = rw.astype(BF16)
    router_w2 = jnp.concatenate([rw_hi, (rw - rw_hi.astype(F32)).astype(BF16)], axis=1)
    pad = jnp.zeros((LANES - 2 * SSM_HEADS,), F32)
    row = lambda a: a.reshape(1, -1).astype(F32)
    return dict(
        norm1_w=row(norm1_w[0]), w_qkv=w_qkv, w_main=w_main, bias_tab=bias_tab, attn_sink=attn_sink[0].astype(F32),
        attn_norm_w=row(attn_norm_w[0][perm]), conv_w=conv_w[0].T.astype(F32), conv_b=row(conv_b[0]),
        dtb=row(jnp.concatenate([dt_bias_fwd[0], dt_bias_bwd[0], pad])),
        alog=row(jnp.concatenate([a_log_fwd[0], a_log_bwd[0], pad])),
        dskip=row(jnp.repeat(d_skip[0], SSM_HEAD_DIM)), ssm_norm_w=row(ssm_norm_w[0]),
        w_out=w_out_p, norm2_w=row(norm2_w[0]), router_w2=router_w2,
        w_gate=w_gate[0], w_up=w_up[0], w_down=w_down[0],
        final_norm_w=row(final_norm_w))


def kernel(x_prompt, x_sample, rel_bias, norm1_w, w_in, conv_w, conv_b, dt_bias_fwd, dt_bias_bwd, a_log_fwd, a_log_bwd, d_skip, ssm_norm_w, attn_sink, attn_norm_w, w_out, norm2_w, router_w, w_gate, w_up, w_down, final_norm_w):
    assert norm1_w.shape[0] == 1
    w = _prep_weights(rel_bias, norm1_w, w_in, conv_w, conv_b, dt_bias_fwd, dt_bias_bwd, a_log_fwd, a_log_bwd,
                      d_skip, ssm_norm_w, attn_sink, attn_norm_w, w_out, norm2_w, router_w, w_gate, w_up,
                      w_down, final_norm_w)
    return (_trunk(x_prompt, w), _trunk(x_sample, w))
```

```python
import functools
import math

import numpy as np
import jax
import jax.numpy as jnp
from jax import lax
from jax.experimental import pallas as pl
from jax.experimental.pallas import tpu as pltpu

F32 = jnp.float32
BF16 = jnp.bfloat16
I32 = jnp.int32

D_MODEL = 2048
HEAD_DIM = 64
N_HEADS = 16
N_KV_HEADS = 4
D_ATTN = 1024
D_KV = 256
WINDOW = 128
BLOCK = 128
NUM_BUCKETS = 32
MAX_DISTANCE = 128
SSM_HEAD_DIM = 64
SSM_HEADS = 16
D_SSM = 1024
SSM_STATE = 128
SSM_GROUPS = 2
HEADS_PER_GROUP = SSM_HEADS // SSM_GROUPS
D_BC = 256
CHUNK = 128
N_EXPERTS = 16
CAPACITY_FACTOR = 2
D_FF = 2048
EPS = 1e-6
NEG_INF = -1e30

LANES = 128
HALO_ROWS = 8
TOKEN_ROWS = D_MODEL // LANES

QKV_COLS = D_ATTN + 2 * D_KV
COL_QK, COL_QV = D_ATTN, D_ATTN + D_KV
COL_Z, COL_X, COL_B, COL_DT = 0, 1024, 2048, 2560
MAIN_COLS = COL_DT + LANES

ATTN_HEAD_ORDER = tuple(8 * t + 4 * par + i for t in range(2) for i in range(4) for par in range(2))

SSD_BATCH_BLOCK = 4
ATTN_BATCH_BLOCK = 4

VMEM_LIMIT = 56 * 1024 * 1024
FFN_VMEM_LIMIT = 60 * 1024 * 1024


def _cparams(n_axes, vmem=None):
    return pltpu.CompilerParams(dimension_semantics=("arbitrary",) * n_axes,
                                vmem_limit_bytes=vmem)


def _rms(x, w):
    return x * lax.rsqrt(jnp.mean(x * x, axis=-1, keepdims=True) + EPS) * w


def _silu(x):
    return x * jax.nn.sigmoid(x)


def _inproj_kernel(x_ref, nw_ref, wq_ref, wr_ref, oq_ref, om_ref):
    hn = _rms(x_ref[...], nw_ref[...]).astype(BF16)
    oq_ref[...] = jnp.dot(hn, wq_ref[...], preferred_element_type=F32).astype(BF16)
    om_ref[...] = jnp.dot(hn, wr_ref[...], preferred_element_type=F32)


def _inproj(x2, norm_w, w_qkv, w_main, tm):
    T = x2.shape[0]
    resident = lambda a: pl.BlockSpec(a.shape, lambda i: (0, 0), pipeline_mode=pl.Buffered(1))
    return pl.pallas_call(
        _inproj_kernel,
        grid=(T // tm,),
        in_specs=[pl.BlockSpec((tm, D_MODEL), lambda i: (i, 0)), resident(norm_w), resident(w_qkv),
                  resident(w_main)],
        out_specs=[pl.BlockSpec((tm, QKV_COLS), lambda i: (i, 0)),
                   pl.BlockSpec((tm, MAIN_COLS), lambda i: (i, 0))],
        out_shape=[jax.ShapeDtypeStruct((T, QKV_COLS), BF16), jax.ShapeDtypeStruct((T, MAIN_COLS), F32)],
        compiler_params=_cparams(1, VMEM_LIMIT),
        name="inproj",
    )(x2, norm_w, w_qkv, w_main)


def _attn_kernel(q_ref, kp_ref, ko_ref, kn_ref, vp_ref, vo_ref, vn_ref, bias_ref, sink_ref, nw_ref,
                 o_ref, *, nb):
    i = pl.program_id(1)
    lane = lax.broadcasted_iota(I32, (3 * BLOCK, LANES), 1)
    m_lo = jnp.where(lane < HEAD_DIM, 1.0, 0.0).astype(BF16)
    m_up = jnp.where(lane < HEAD_DIM, 0.0, 1.0).astype(BF16)
    low_q = lax.broadcasted_iota(I32, (BLOCK, LANES), 1) < HEAD_DIM
    top = lax.broadcasted_iota(I32, (2 * BLOCK, 1), 0) < BLOCK
    tiles_per_pair = D_ATTN // LANES // (D_KV // LANES)

    def body(bb, edge):
        tiles = []
        for t in range(D_KV // LANES):
            sl = slice(t * LANES, (t + 1) * LANES)
            kt = jnp.concatenate([kp_ref[bb, :, sl], ko_ref[bb, :, sl], kn_ref[bb, :, sl]], axis=0)
            vt = jnp.concatenate([vp_ref[bb, :, sl], vo_ref[bb, :, sl], vn_ref[bb, :, sl]], axis=0)
            rhs = [(kt * m_lo, vt * m_lo + m_up), (kt * m_up, vt * m_up + m_lo)]
            for j in range(0, tiles_per_pair, 2):
                qa, qb = t * tiles_per_pair + j, t * tiles_per_pair + j + 1
                q2 = jnp.concatenate([q_ref[bb, :, qa * LANES:(qa + 1) * LANES],
                                      q_ref[bb, :, qb * LANES:(qb + 1) * LANES]], axis=0)
                res = []
                for par in range(2):
                    kz, vz = rhs[par]
                    ha, hb = ATTN_HEAD_ORDER[2 * qa + par], ATTN_HEAD_ORDER[2 * qb + par]
                    s = lax.dot_general(q2, kz, (((1,), (1,)), ((), ())), preferred_element_type=F32)
                    s = s + jnp.concatenate([bias_ref[ha], bias_ref[hb]], axis=0) + edge
                    sk = jnp.where(top, sink_ref[ha], sink_ref[hb])
                    m = jnp.maximum(jnp.max(s, axis=-1, keepdims=True), sk)
                    p = jnp.exp(s - m).astype(BF16)
                    pv = jnp.dot(p, vz, preferred_element_type=F32)
                    den = (pv[:, LANES - 1:LANES] if par == 0 else pv[:, 0:1]) + jnp.exp(sk - m)
                    res.append(pv * (1.0 / den))
                tiles.append(jnp.where(low_q, res[0][:BLOCK], res[1][:BLOCK]))
                tiles.append(jnp.where(low_q, res[0][BLOCK:], res[1][BLOCK:]))
        o = jnp.concatenate(tiles, axis=-1)
        o_ref[bb] = _rms(o, nw_ref[...]).astype(BF16)

    kcol = lax.broadcasted_iota(I32, (1, 3 * BLOCK), 1)
    edge = (jnp.where((kcol < BLOCK) & (i == 0), NEG_INF, 0.0)
            + jnp.where((kcol >= 2 * BLOCK) & (i == nb - 1), NEG_INF, 0.0))
    for bb in range(q_ref.shape[0]):
        body(bb, edge)


def _attention(qkv3, bias_tab, sink, norm_w):
    B, S, _ = qkv3.shape
    nb = S // BLOCK
    kcol, vcol = COL_QK // D_KV, COL_QV // D_KV
    prev = lambda i: jnp.maximum(i - 1, 0)
    nxt = lambda i: jnp.minimum(i + 1, nb - 1)
    nbt = ATTN_BATCH_BLOCK if B % ATTN_BATCH_BLOCK == 0 else 1
    kv = lambda col, f: pl.BlockSpec((nbt, BLOCK, D_KV), lambda b, i: (b, f(i), col))
    same = lambda i: i
    return pl.pallas_call(
        functools.partial(_attn_kernel, nb=nb),
        grid=(B // nbt, nb),
        in_specs=[pl.BlockSpec((nbt, BLOCK, D_ATTN), lambda b, i: (b, i, 0)),
                  kv(kcol, prev), kv(kcol, same), kv(kcol, nxt),
                  kv(vcol, prev), kv(vcol, same), kv(vcol, nxt),
                  pl.BlockSpec((N_HEADS, BLOCK, 3 * BLOCK), lambda b, i: (0, 0, 0)),
                  pl.BlockSpec(memory_space=pltpu.SMEM),
                  pl.BlockSpec((1, D_ATTN), lambda b, i: (0, 0))],
        out_specs=pl.BlockSpec((nbt, BLOCK, D_ATTN), lambda b, i: (b, i, 0)),
        out_shape=jax.ShapeDtypeStruct((B, S, D_ATTN), BF16),
        compiler_params=_cparams(2),
        name="attention",
    )(qkv3, qkv3, qkv3, qkv3, qkv3, qkv3, qkv3, bias_tab, sink, norm_w)


def _conv_silu(u, prev8, next8, w, b, has_prev, has_next):
    prev_row = jnp.where(has_prev, prev8[HALO_ROWS - 1:HALO_ROWS, :], 0.0)
    next_row = jnp.where(has_next, next8[0:1, :], 0.0)
    rid = lax.broadcasted_iota(I32, u.shape, 0)
    um = jnp.where(rid == 0, prev_row, pltpu.roll(u, 1, 0))
    up = jnp.where(rid == CHUNK - 1, next_row, pltpu.roll(u, CHUNK - 1, 0))
    y = b + um * w[0:1, :] + u * w[1:2, :] + up * w[2:3, :]
    return _silu(y)


def _ssd_kernel(*refs, reverse, nc):
    h_ref = refs[-1]

    @pl.when(pl.program_id(1) == 0)
    def _():
        h_ref[...] = jnp.zeros_like(h_ref)

    for bb in range(h_ref.shape[0]):
        _ssd_chunk(bb, refs, reverse, nc)


def _ssd_chunk(bb, refs, reverse, nc):
    if reverse:
        (z_ref, yf_ref, x_ref, xp_ref, xn_ref, bc_ref, bcp_ref, bcn_ref, dt_ref, cw_ref, cb_ref, dtb_ref,
         alog_ref, dskip_ref, nw_ref, o_ref, h_ref) = refs
    else:
        (x_ref, xp_ref, xn_ref, bc_ref, bcp_ref, bcn_ref, dt_ref, cw_ref, cb_ref, dtb_ref,
         alog_ref, o_ref, h_ref) = refs
    step = pl.program_id(1)
    c = (nc - 1 - step) if reverse else step
    has_prev, has_next = c > 0, c < nc - 1
    cw, cb = cw_ref[...], cb_ref[...]
    xc = _conv_silu(x_ref[bb], xp_ref[bb], xn_ref[bb], cw[:, :D_SSM], cb[:, :D_SSM], has_prev, has_next)
    bcv = _conv_silu(bc_ref[bb], bcp_ref[bb], bcn_ref[bb], cw[:, D_SSM:], cb[:, D_SSM:], has_prev, has_next)

    raw = dt_ref[bb] + dtb_ref[...]
    dt = jnp.maximum(raw, 0.0) + jnp.log1p(jnp.exp(-jnp.abs(raw)))
    a = dt * (-jnp.exp(alog_ref[...]))
    li = lax.broadcasted_iota(I32, (CHUNK, CHUNK), 0)
    si = lax.broadcasted_iota(I32, (CHUNK, CHUNK), 1)
    incl = jnp.dot(jnp.where(li >= si, 1.0, 0.0).astype(F32), a, precision=lax.Precision.HIGHEST,
                   preferred_element_type=F32)
    tot = incl[CHUNK - 1:CHUNK, :]
    if reverse:
        pcs = incl - a
        dstate = jnp.exp(pcs)
        yscale = jnp.exp(tot - pcs)
        mask = si >= li
    else:
        pcs = incl
        dstate = jnp.exp(tot - pcs)
        yscale = jnp.exp(pcs)
        mask = li >= si
    pcs_t = pcs.T
    lane0 = SSM_HEADS if reverse else 0
    P = SSM_HEAD_DIM
    gw = HEADS_PER_GROUP * P

    def spread(x, width, pieces):
        k = lax.broadcasted_iota(I32, (LANES, SSM_HEADS * width), 0)
        c = lax.broadcasted_iota(I32, (LANES, SSM_HEADS * width), 1)
        sel = jnp.where(k == lane0 + c // width, 1.0, 0.0).astype(BF16)
        out, rem = None, x
        for _ in range(pieces):
            piece = rem.astype(BF16)
            rem = rem - piece.astype(F32)
            d = jnp.dot(piece, sel, preferred_element_type=F32)
            out = d if out is None else out + d
        return out

    scales = spread(jnp.concatenate([dt, dstate, yscale, jnp.broadcast_to(jnp.exp(tot), (HALO_ROWS, LANES))], axis=0),
                    P, 2)
    dt_e, ds_e, ys_e = scales[:CHUNK], scales[CHUNK:2 * CHUNK], scales[2 * CHUNK:3 * CHUNK]
    cdec_e = scales[3 * CHUNK:3 * CHUNK + 1]
    col_e = spread(pcs, CHUNK, 3)
    xdt = xc * dt_e
    xdt_b = xdt.astype(BF16)
    xs_b = (xdt * ds_e).astype(BF16)
    low = lax.broadcasted_iota(I32, (CHUNK, LANES), 1) < P

    y_groups = []
    for g in range(SSM_GROUPS):
        bg = bcv[:, g * SSM_STATE:(g + 1) * SSM_STATE]
        cg = bcv[:, D_BC + g * SSM_STATE:D_BC + (g + 1) * SSM_STATE].astype(BF16)
        cbm = lax.dot_general(cg, bg.astype(BF16), (((1,), (1,)), ((), ())), preferred_element_type=F32)
        hg = h_ref[bb, g]
        yoff = jnp.dot(cg, hg.astype(BF16), preferred_element_type=F32)
        yd = []
        for jp in range(HEADS_PER_GROUP // 2):
            tile = (g * HEADS_PER_GROUP) // 2 + jp
            xpair = xdt_b[:, tile * LANES:(tile + 1) * LANES]
            halves = []
            for par in range(2):
                j = 2 * tile + par
                ln = lane0 + j
                col = col_e[:, j * CHUNK:(j + 1) * CHUNK]
                row = pcs_t[ln:ln + 1, :]
                seg = (row - col) if reverse else (col - row)
                lm = jnp.exp(jnp.where(mask, seg, -jnp.inf))
                halves.append(jnp.dot((cbm * lm).astype(BF16), xpair, preferred_element_type=F32))
            yd.append(jnp.where(low, halves[0], halves[1]))
        sl = slice(g * gw, (g + 1) * gw)
        y_groups.append(jnp.concatenate(yd, axis=1) + yoff * ys_e[:, sl])
        snew = jnp.dot(bg.T.astype(BF16), xs_b[:, sl], preferred_element_type=F32)
        h_ref[bb, g] = hg * cdec_e[:, sl] + snew
    y = jnp.concatenate(y_groups, axis=1)

    if not reverse:
        o_ref[bb] = y
    else:
        ytot = yf_ref[bb] + y + dskip_ref[...] * xc
        yz = ytot * _silu(z_ref[bb])
        halves = []
        for g in range(SSM_GROUPS):
            seg = yz[:, g * gw:(g + 1) * gw]
            halves.append(seg * lax.rsqrt(jnp.mean(seg * seg, axis=-1, keepdims=True) + EPS))
        o_ref[bb] = (jnp.concatenate(halves, axis=1) * nw_ref[...]).astype(BF16)


def _ssd_pass(proj3, conv_w3, conv_b, dtb, alog, reverse, yf=None, dskip=None, norm_w=None):
    B, S, _ = proj3.shape
    nc = S // CHUNK
    hp = CHUNK // HALO_ROWS
    nh = S // HALO_ROWS
    ch = (lambda s: nc - 1 - s) if reverse else (lambda s: s)
    nbt = SSD_BATCH_BLOCK if B % SSD_BATCH_BLOCK == 0 else 1
    main = lambda w, col: pl.BlockSpec((nbt, CHUNK, w), lambda b, s: (b, ch(s), col // w))
    halo_p = lambda w, col: pl.BlockSpec((nbt, HALO_ROWS, w),
                                         lambda b, s: (b, jnp.maximum(ch(s) * hp - 1, 0), col // w))
    halo_n = lambda w, col: pl.BlockSpec((nbt, HALO_ROWS, w),
                                         lambda b, s: (b, jnp.minimum((ch(s) + 1) * hp, nh - 1), col // w))
    full = lambda a: pl.BlockSpec(a.shape, lambda b, s: (0,) * a.ndim)
    d_bc2 = 2 * D_BC
    in_specs = [main(D_SSM, COL_X), halo_p(D_SSM, COL_X), halo_n(D_SSM, COL_X),
                main(d_bc2, COL_B), halo_p(d_bc2, COL_B), halo_n(d_bc2, COL_B),
                main(LANES, COL_DT), full(conv_w3), full(conv_b), full(dtb), full(alog)]
    args = [proj3, proj3, proj3, proj3, proj3, proj3, proj3, conv_w3, conv_b, dtb, alog]
    if reverse:
        in_specs = [main(D_SSM, COL_Z), pl.BlockSpec((nbt, CHUNK, D_SSM), lambda b, s: (b, ch(s), 0))] + in_specs
        in_specs += [full(dskip), full(norm_w)]
        args = [proj3, yf] + args + [dskip, norm_w]
    return pl.pallas_call(
        functools.partial(_ssd_kernel, reverse=reverse, nc=nc),
        grid=(B // nbt, nc),
        in_specs=in_specs,
        out_specs=pl.BlockSpec((nbt, CHUNK, D_SSM), lambda b, s: (b, ch(s), 0)),
        out_shape=jax.ShapeDtypeStruct((B, S, D_SSM), BF16 if reverse else F32),
        scratch_shapes=[pltpu.VMEM((nbt, SSM_GROUPS, SSM_STATE, HEADS_PER_GROUP * SSM_HEAD_DIM), F32)],
        compiler_params=_cparams(2),
        name="ssd_bwd" if reverse else "ssd_fwd",
    )(*args)


def _outproj_kernel(x_ref, a_ref, s_ref, w_ref, n2_ref, rw2_ref, x1_ref, h_ref, aff_ref):
    tm = x_ref.shape[0]
    x1 = (x_ref[...]
          + jnp.dot(a_ref[...], w_ref[:D_ATTN, :], preferred_element_type=F32)
          + jnp.dot(s_ref[...], w_ref[D_ATTN:, :], preferred_element_type=F32))
    x1_ref[...] = x1
    hn = _rms(x1, n2_ref[...])
    for s in range(TOKEN_ROWS):
        h_ref[pl.ds(s, tm, stride=TOKEN_ROWS), :] = hn[:, s * LANES:(s + 1) * LANES]
    hn_hi = hn.astype(BF16)
    hn_lo = (hn - hn_hi.astype(F32)).astype(BF16)
    l_hi = jnp.dot(hn_hi, rw2_ref[...], preferred_element_type=F32)
    l_lo = jnp.dot(hn_lo, rw2_ref[:, :N_EXPERTS], preferred_element_type=F32)
    logits = l_hi[:, :N_EXPERTS] + l_hi[:, N_EXPERTS:] + l_lo
    e = jnp.exp(logits - jnp.max(logits, axis=-1, keepdims=True))
    aff_ref[...] = e / jnp.sum(e, axis=-1, keepdims=True)


def _outproj(x2, attn2, ssm2, w_out, norm2_w, router_w, tm):
    T = x2.shape[0]
    row = lambda w: pl.BlockSpec((tm, w), lambda i: (i, 0))
    full = lambda a: pl.BlockSpec(a.shape, lambda i: (0,) * a.ndim)
    return pl.pallas_call(
        _outproj_kernel,
        grid=(T // tm,),
        in_specs=[row(D_MODEL), row(D_ATTN), row(D_SSM), full(w_out), full(norm2_w), full(router_w)],
        out_specs=[row(D_MODEL), pl.BlockSpec((tm * TOKEN_ROWS, LANES), lambda i: (i, 0)), row(N_EXPERTS)],
        out_shape=[jax.ShapeDtypeStruct((T, D_MODEL), F32),
                   jax.ShapeDtypeStruct((T * TOKEN_ROWS, LANES), F32),
                   jax.ShapeDtypeStruct((T, N_EXPERTS), F32)],
        compiler_params=_cparams(1, VMEM_LIMIT),
        name="outproj_router",
    )(x2, attn2, ssm2, w_out, norm2_w, router_w)


def _count(m):
    c = jnp.sum(jnp.where(m, 1.0, 0.0), axis=0, keepdims=True)
    return jnp.sum(c, axis=1, keepdims=True)


def _tri(n, m, fn):
    return jnp.where(fn(lax.broadcasted_iota(I32, (n, m), 0), lax.broadcasted_iota(I32, (n, m), 1)),
                     1.0, 0.0).astype(BF16)


def _dot_u16(lhs01, x):
    hi = jnp.floor(x * (1.0 / 256.0))
    lo = x - hi * 256.0
    return (jnp.dot(lhs01, hi.astype(BF16), preferred_element_type=F32) * 256.0
            + jnp.dot(lhs01, lo.astype(BF16), preferred_element_type=F32))


def _cumsum_rowmajor(x):
    R = x.shape[0]
    within = jnp.dot(x.astype(BF16), _tri(LANES, LANES, lambda k, l: k <= l), preferred_element_type=F32)
    rowtot = jnp.broadcast_to(within[:, LANES - 1:LANES], (R, LANES))
    before = _dot_u16(_tri(R, R, lambda i, k: k < i), rowtot)
    return within + before


def _route_kernel(a_ref, idx_ref, dst_ref, g_ref, cs_ref, ce_ref, sel_s, cnt_s, cs_s, rank_s, *, cap):
    s = pl.program_id(0)
    R = a_ref.shape[0]

    @pl.when(s < N_EXPERTS)
    def _select():
        bits = pltpu.bitcast(a_ref[...], I32)
        capf = jnp.float32(cap)

        def body(k, tau):
            cand = tau | lax.shift_left(jnp.int32(1), 30 - k)
            return jnp.where(_count(bits >= cand) >= capf, cand, tau)

        tau = lax.fori_loop(0, 31, body, jnp.zeros((1, 1), I32))
        gt = bits > tau
        eq = bits == tau
        need = capf - _count(gt)
        ties = _cumsum_rowmajor(jnp.where(eq, 1.0, 0.0))
        sel = jnp.where(gt | (eq & (ties <= need)), 1.0, 0.0)
        sel_s[s] = sel

        @pl.when(s == 0)
        def _():
            cnt_s[...] = sel

        @pl.when(s > 0)
        def _():
            cnt_s[...] += sel

    @pl.when(s == N_EXPERTS)
    def _prefix():
        cnt = cnt_s[...]
        ce = _cumsum_rowmajor(cnt)
        cs_s[...] = ce - cnt
        rank_s[...] = jnp.zeros_like(rank_s)
        cs_ref[...] = ce - cnt
        ce_ref[...] = ce

    @pl.when(s >= N_EXPERTS)
    def _invert():
        e = s - N_EXPERTS
        sel = sel_s[e]
        rank = rank_s[...]
        q = cs_s[...] + rank
        rank_s[...] = rank + sel
        a = a_ref[...]
        within = jnp.dot(sel.astype(BF16), _tri(LANES, LANES, lambda k, l: k <= l), preferred_element_type=F32)
        n_b = jnp.broadcast_to(within[:, LANES - 1:LANES], (R, LANES))
        rowend = jnp.dot(_tri(R, R, lambda i, k: k <= i), n_b.astype(BF16), preferred_element_type=F32)
        slot = lax.broadcasted_iota(I32, (R, cap), 1).astype(F32)
        done = jnp.where(rowend[:, 0:1] <= slot, 1.0, 0.0)
        ones = jnp.ones((8, R), BF16)
        row_p = jnp.dot(ones, done.astype(BF16), preferred_element_type=F32)[0:1]
        start_p = jnp.dot(ones, (done * n_b[:, 0:1]).astype(BF16), preferred_element_type=F32)[0:1]
        onehot = jnp.where(lax.broadcasted_iota(I32, (R, cap), 0).astype(F32) == row_p, 1.0, 0.0).astype(BF16)
        w_t = jnp.dot(within.T.astype(BF16), onehot, preferred_element_type=F32)
        k_in_row = lax.broadcasted_iota(I32, (1, cap), 1).astype(F32) - start_p
        lane_p = jnp.sum(jnp.where(w_t <= k_in_row, 1.0, 0.0), axis=0, keepdims=True)
        pick = lax.broadcasted_iota(I32, (LANES, cap), 0).astype(F32) == lane_p

        def take(x):
            v = jnp.dot(x.T.astype(BF16), onehot, preferred_element_type=F32)
            return jnp.sum(jnp.where(pick, v, 0.0), axis=0, keepdims=True)

        q_hi = jnp.floor(q * (1.0 / 256.0))
        a_hi = a.astype(BF16).astype(F32)
        a_mid = (a - a_hi).astype(BF16).astype(F32)
        a_lo = a - a_hi - a_mid
        idx_ref[...] = (row_p * LANES + lane_p).astype(I32)
        dst_ref[...] = (take(q_hi) * 256.0 + take(q - q_hi * 256.0)).astype(I32)
        g_ref[...] = take(a_hi) + take(a_mid) + take(a_lo)


def _route(aff_t3, cap):
    E, R, _ = aff_t3.shape
    assert R <= 256 and E == N_EXPERTS
    slot_spec = pl.BlockSpec((None, 1, cap), lambda s: (jnp.maximum(s - N_EXPERTS, 0), 0, 0))
    tok_spec = pl.BlockSpec((R, LANES), lambda s: (0, 0))
    return pl.pallas_call(
        functools.partial(_route_kernel, cap=cap),
        grid=(2 * E,),
        in_specs=[pl.BlockSpec((None, R, LANES), lambda s: (s % N_EXPERTS, 0, 0))],
        out_specs=[slot_spec, slot_spec, slot_spec, tok_spec, tok_spec],
        out_shape=[jax.ShapeDtypeStruct((E, 1, cap), I32), jax.ShapeDtypeStruct((E, 1, cap), I32),
                   jax.ShapeDtypeStruct((E, 1, cap), F32),
                   jax.ShapeDtypeStruct((R, LANES), F32), jax.ShapeDtypeStruct((R, LANES), F32)],
        scratch_shapes=[pltpu.VMEM((E, R, LANES), F32), pltpu.VMEM((R, LANES), F32),
                        pltpu.VMEM((R, LANES), F32), pltpu.VMEM((R, LANES), F32)],
        compiler_params=_cparams(1, VMEM_LIMIT),
        name="route",
    )(aff_t3)


def _ffn_kernel(idx_ref, dst_ref, h_hbm, gt_ref, wg_ref, wu_ref, wd_ref, c_hbm,
                xs_stage, xsb, hid, o_stage, gsem, ssem, *, tm, cap, nf):
    e, m, f = pl.program_id(0), pl.program_id(1), pl.program_id(2)
    base = e * cap + m * tm
    tr = TOKEN_ROWS
    part = tm // nf
    tf = wg_ref.shape[1]
    tn = wd_ref.shape[1]
    is_first = (e == 0) & (m == 0)
    is_last = (e == pl.num_programs(0) - 1) & (m == pl.num_programs(1) - 1)

    def token_rows(t):
        return pl.ds(pl.multiple_of(t * tr, tr), tr)

    def gather_rows(tile_base, r0, n):
        def issue(r, carry):
            pltpu.make_async_copy(h_hbm.at[token_rows(idx_ref[tile_base + r0 + r]), :],
                                  xs_stage.at[token_rows(r0 + r), :], gsem).start()
            return carry

        lax.fori_loop(0, n, issue, 0, unroll=8)

    def scatter_rows(tile_base, r0, n):
        def issue(r, carry):
            pltpu.make_async_copy(o_stage.at[token_rows(r0 + r), :],
                                  c_hbm.at[token_rows(dst_ref[tile_base + r0 + r]), :], ssem).start()
            return carry

        lax.fori_loop(0, n, issue, 0, unroll=8)

    def scatter_done():
        pltpu.make_async_copy(o_stage, c_hbm.at[pl.ds(0, tm * tr), :], ssem).wait()

    def gather_done():
        pltpu.make_async_copy(h_hbm.at[pl.ds(0, tm * tr), :], xs_stage, gsem).wait()

    @pl.when(f == 0)
    def _stage_in():
        @pl.when(is_first)
        def _():
            gather_rows(base, 0, tm)
            o_stage[...] = jnp.zeros_like(o_stage)

        gather_done()
        for s in range(tr):
            xsb[:, s * LANES:(s + 1) * LANES] = xs_stage[pl.ds(s, tm, stride=tr), :].astype(BF16)

    nxt = jnp.where(is_last, base, base + tm)
    prv = jnp.maximum(base - tm, 0)

    for k in range(nf):
        @pl.when(f == k)
        def _gate_up(k=k):
            xs = xsb[...]
            g = jnp.dot(xs, wg_ref[...].astype(BF16), preferred_element_type=F32)
            u = jnp.dot(xs, wu_ref[...].astype(BF16), preferred_element_type=F32)
            hid[:, k * tf:(k + 1) * tf] = (_silu(g) * u).astype(BF16)
            for r in range(k * part, (k + 1) * part):
                pltpu.make_async_copy(h_hbm.at[token_rows(idx_ref[nxt + r]), :],
                                      xs_stage.at[token_rows(r), :], gsem).start()
                pltpu.make_async_copy(o_stage.at[token_rows(r), :],
                                      c_hbm.at[token_rows(dst_ref[prv + r]), :], ssem).start()

    for n in range(nf):
        @pl.when(f == nf + n)
        def _down(n=n):
            if n == 0:
                scatter_done()

            out = jnp.dot(hid[...], wd_ref[...].astype(BF16), preferred_element_type=F32)
            for j in range(tm // LANES):
                rows = out[j * LANES:(j + 1) * LANES, :] * gt_ref[:, j:j + 1]
                for c in range(tn // LANES):
                    o_stage[pl.ds(j * LANES * tr + n * (tn // LANES) + c, LANES, stride=tr), :] = (
                        rows[:, c * LANES:(c + 1) * LANES])

            if n == nf - 1:
                @pl.when(is_last)
                def _():
                    scatter_rows(base, 0, tm)
                    scatter_done()
                    gather_done()


def _ffn(idx, dst, h_rows, g_t, w_gate, w_up, w_down, tm, tf):
    E, n_m = g_t.shape[0], g_t.shape[1]
    cap = n_m * tm
    n_contrib = E * cap
    nf = D_FF // tf
    tn = D_MODEL // nf
    up = lambda f: jnp.minimum(f, nf - 1)
    down = lambda f: jnp.maximum(f - nf, 0)
    grid_spec = pltpu.PrefetchScalarGridSpec(
        num_scalar_prefetch=2,
        grid=(E, n_m, 2 * nf),
        in_specs=[pl.BlockSpec(memory_space=pl.ANY),
                  pl.BlockSpec((None, None, LANES, tm // LANES), lambda e, m, f, i, d: (e, m, 0, 0)),
                  pl.BlockSpec((None, D_MODEL, tf), lambda e, m, f, i, d: (e, 0, up(f))),
                  pl.BlockSpec((None, D_MODEL, tf), lambda e, m, f, i, d: (e, 0, up(f))),
                  pl.BlockSpec((None, D_FF, tn), lambda e, m, f, i, d: (e, 0, down(f)))],
        out_specs=pl.BlockSpec(memory_space=pl.ANY),
        scratch_shapes=[pltpu.VMEM((tm * TOKEN_ROWS, LANES), F32), pltpu.VMEM((tm, D_MODEL), BF16),
                        pltpu.VMEM((tm, D_FF), BF16), pltpu.VMEM((tm * TOKEN_ROWS, LANES), F32),
                        pltpu.SemaphoreType.DMA, pltpu.SemaphoreType.DMA],
    )
    return pl.pallas_call(
        functools.partial(_ffn_kernel, tm=tm, cap=cap, nf=nf),
        grid_spec=grid_spec,
        out_shape=jax.ShapeDtypeStruct((n_contrib * TOKEN_ROWS, LANES), F32),
        compiler_params=_cparams(3, FFN_VMEM_LIMIT),
        name="expert_ffn",
    )(idx, dst, h_rows, g_t, w_gate, w_up, w_down)


def _combine_kernel(ch_ref, tt_ref, flag_ref, x1_ref, cs_ref, ce_ref, c_ref, nw_ref, out_ref, acc_ref, *, cg, rows):
    k = pl.program_id(0)
    flag = flag_ref[k]

    @pl.when((flag & 2) > 0)
    def _():
        acc_ref[...] = x1_ref[...]

    @pl.when((flag & 1) > 0)
    def _():
        cio = (lax.broadcasted_iota(I32, (cg, LANES), 0) + ch_ref[k] * cg).astype(F32)
        parts = [jnp.where((cs_ref[r:r + 1, :] <= cio) & (cio < ce_ref[r:r + 1, :]), 1.0, 0.0) for r in range(rows)]
        a = jnp.concatenate(parts, axis=1).T.astype(BF16)
        chunk = jnp.concatenate([c_ref[pl.ds(s, cg, stride=TOKEN_ROWS), :].astype(BF16)
                                 for s in range(TOKEN_ROWS)], axis=1)
        acc_ref[...] += jnp.dot(a, chunk, preferred_element_type=F32)

    @pl.when((flag & 4) > 0)
    def _():
        out_ref[...] = _rms(acc_ref[...], nw_ref[...])


def _combine(ch, tt, flag, x1, cs3, ce3, contrib, final_w, tb, cg):
    T = x1.shape[0]
    rows = tb // LANES
    n_pairs = ch.shape[0]
    grid_spec = pltpu.PrefetchScalarGridSpec(
        num_scalar_prefetch=3,
        grid=(n_pairs,),
        in_specs=[pl.BlockSpec((tb, D_MODEL), lambda k, ch, tt, fl: (tt[k], 0)),
                  pl.BlockSpec((None, rows, LANES), lambda k, ch, tt, fl: (tt[k], 0, 0)),
                  pl.BlockSpec((None, rows, LANES), lambda k, ch, tt, fl: (tt[k], 0, 0)),
                  pl.BlockSpec((cg * TOKEN_ROWS, LANES), lambda k, ch, tt, fl: (ch[k], 0)),
                  pl.BlockSpec((1, D_MODEL), lambda k, ch, tt, fl: (0, 0))],
        out_specs=pl.BlockSpec((tb, D_MODEL), lambda k, ch, tt, fl: (tt[k], 0)),
        scratch_shapes=[pltpu.VMEM((tb, D_MODEL), F32)],
    )
    return pl.pallas_call(
        functools.partial(_combine_kernel, cg=cg, rows=rows),
        grid_spec=grid_spec,
        out_shape=jax.ShapeDtypeStruct((T, D_MODEL), F32),
        compiler_params=_cparams(1, VMEM_LIMIT),
        name="combine_final",
    )(ch, tt, flag, x1, cs3, ce3, contrib, final_w)


def _combine_schedule(cs, ce, tb, cg):
    T = cs.size
    ntt = T // tb
    total = CAPACITY_FACTOR * T
    nch = total // cg
    lo = cs.reshape(ntt, tb)[:, 0].astype(I32)
    hi = ce.reshape(ntt, tb)[:, -1].astype(I32)
    first = jnp.minimum(lo, total - 1) // cg
    last = jnp.where(hi > lo, (hi - 1) // cg, first)
    n = last - first + 1
    ends = jnp.cumsum(n)
    offs = ends - n
    n_pairs = ntt + nch
    ks = jnp.arange(n_pairs, dtype=I32)
    tile = jnp.minimum(jnp.searchsorted(ends, ks, side="right").astype(I32), ntt - 1)
    valid = ks < ends[-1]
    chunk = jnp.where(valid, first[tile] + ks - offs[tile], last[-1])
    flag = jnp.where(valid, 1 + 2 * (ks == offs[tile]) + 4 * (ks == ends[tile] - 1), 0).astype(I32)
    return chunk.astype(I32), tile, flag


def _t5_buckets(rel):
    nb = NUM_BUCKETS // 2
    ret = (rel > 0).astype(np.int32) * nb
    n = np.abs(rel)
    max_exact = nb // 2
    large = max_exact + (np.log(np.maximum(n, 1) / max_exact) / np.log(MAX_DISTANCE / max_exact)
                         * (nb - max_exact)).astype(np.int32)
    large = np.minimum(large, nb - 1)
    return ret + np.where(n < max_exact, n, large)


def _tile(n, pref):
    t = min(n, pref)
    assert n % t == 0
    return t


def _trunk(x, w):
    B, S, _ = x.shape
    T = B * S
    cap = CAPACITY_FACTOR * T // N_EXPERTS
    x2 = x.reshape(T, D_MODEL)
    tm = _tile(T, 512)

    qkv, main = _inproj(x2, w["norm1_w"], w["w_qkv"], w["w_main"], tm)
    main3 = main.reshape(B, S, MAIN_COLS)
    attn = _attention(qkv.reshape(B, S, QKV_COLS), w["bias_tab"], w["attn_sink"], w["attn_norm_w"])
    yf = _ssd_pass(main3, w["conv_w"], w["conv_b"], w["dtb"], w["alog"], reverse=False)
    ssm = _ssd_pass(main3, w["conv_w"], w["conv_b"], w["dtb"], w["alog"], reverse=True,
                    yf=yf, dskip=w["dskip"], norm_w=w["ssm_norm_w"])
    x1, h_rows, aff = _outproj(x2, attn.reshape(T, D_ATTN), ssm.reshape(T, D_SSM), w["w_out"], w["norm2_w"],
                          w["router_w2"], tm)

    tb = _tile(T, 512)
    cg = 512
    idx, dst, g, cs, ce = _route(aff.T.reshape(N_EXPERTS, T // LANES, LANES), cap)
    tmf = _tile(cap, 1024)
    g_t = jnp.swapaxes(g.reshape(N_EXPERTS, cap // tmf, tmf // LANES, LANES), 2, 3)
    contrib = _ffn(idx.reshape(-1), dst.reshape(-1), h_rows, g_t, w["w_gate"], w["w_up"], w["w_down"], tmf, 512)
    ch, tt, flag = _combine_schedule(cs, ce, tb, cg)
    tile3 = lambda a: a.reshape(T // tb, tb // LANES, LANES)
    y = _combine(ch, tt, flag, x1, tile3(cs), tile3(ce), contrib, w["final_norm_w"], tb, cg)
    return y.reshape(B, S, D_MODEL)


def _prep_weights(rel_bias, norm1_w, w_in, conv_w, conv_b, dt_bias_fwd, dt_bias_bwd, a_log_fwd, a_log_bwd,
                  d_skip, ssm_norm_w, attn_sink, attn_norm_w, w_out, norm2_w, router_w, w_gate, w_up, w_down,
                  final_norm_w):
    o1 = D_ATTN; o2 = o1 + D_KV; o3 = o2 + D_KV; o4 = o3 + D_SSM; o5 = o4 + D_SSM + 2 * D_BC
    wi = w_in[0]
    scale = 1.0 / math.sqrt(HEAD_DIM)
    perm = np.concatenate([np.arange(h * HEAD_DIM, (h + 1) * HEAD_DIM) for h in ATTN_HEAD_ORDER])
    w_qkv = jnp.concatenate([(wi[:, :o1] * scale)[:, perm], wi[:, o1:o3]], axis=1).astype(BF16)
    w_out_p = jnp.concatenate([w_out[0][:D_ATTN][perm], w_out[0][D_ATTN:]], axis=0).astype(BF16)
    w_main = jnp.concatenate([wi[:, o3:], jnp.zeros((D_MODEL, LANES - 2 * SSM_HEADS), F32)], axis=1).astype(BF16)
    rel = np.arange(3 * BLOCK)[None, :] - BLOCK - np.arange(BLOCK)[:, None]
    onehot = (jnp.asarray(_t5_buckets(rel), I32)[..., None] == jnp.arange(NUM_BUCKETS, dtype=I32)).astype(F32)
    bias_tab = jnp.einsum("qsn,nh->hqs", onehot, rel_bias.astype(F32), precision=lax.Precision.HIGHEST)
    bias_tab = jnp.where(jnp.asarray(np.abs(rel) <= WINDOW)[None], bias_tab, NEG_INF)
    rw = router_w[0].astype(F32)
    rw_hi = rw.astype(BF16)
    router_w2 = jnp.concatenate([rw_hi, (rw - rw_hi.astype(F32)).astype(BF16)], axis=1)
    pad = jnp.zeros((LANES - 2 * SSM_HEADS,), F32)
    row = lambda a: a.reshape(1, -1).astype(F32)
    return dict(
        norm1_w=row(norm1_w[0]), w_qkv=w_qkv, w_main=w_main, bias_tab=bias_tab, attn_sink=attn_sink[0].astype(F32),
        attn_norm_w=row(attn_norm_w[0][perm]), conv_w=conv_w[0].T.astype(F32), conv_b=row(conv_b[0]),
        dtb=row(jnp.concatenate([dt_bias_fwd[0], dt_bias_bwd[0], pad])),
        alog=row(jnp.concatenate([a_log_fwd[0], a_log_bwd[0], pad])),
        dskip=row(jnp.repeat(d_skip[0], SSM_HEAD_DIM)), ssm_norm_w=row(ssm_norm_w[0]),
        w_out=w_out_p, norm2_w=row(norm2_w[0]), router_w2=router_w2,
        w_gate=w_gate[0], w_up=w_up[0], w_down=w_down[0],
        final_norm_w=row(final_norm_w))


def kernel(x_prompt, x_sample, rel_bias, norm1_w, w_in, conv_w, conv_b, dt_bias_fwd, dt_bias_bwd, a_log_fwd, a_log_bwd, d_skip, ssm_norm_w, attn_sink, attn_norm_w, w_out, norm2_w, router_w, w_gate, w_up, w_down, final_norm_w):
    assert norm1_w.shape[0] == 1
    w = _prep_weights(rel_bias, norm1_w, w_in, conv_w, conv_b, dt_bias_fwd, dt_bias_bwd, a_log_fwd, a_log_bwd,
                      d_skip, ssm_norm_w, attn_sink, attn_norm_w, w_out, norm2_w, router_w, w_gate, w_up,
                      w_down, final_norm_w)
    return (_trunk(x_prompt, w), _trunk(x_sample, w))
```

```python
import functools
import math

import numpy as np
import jax
import jax.numpy as jnp
from jax import lax
from jax.experimental import pallas as pl
from jax.experimental.pallas import tpu as pltpu

F32 = jnp.float32
BF16 = jnp.bfloat16
I32 = jnp.int32

D_MODEL = 2048
HEAD_DIM = 64
N_HEADS = 16
N_KV_HEADS = 4
D_ATTN = 1024
D_KV = 256
WINDOW = 128
BLOCK = 128
NUM_BUCKETS = 32
MAX_DISTANCE = 128
SSM_HEAD_DIM = 64
SSM_HEADS = 16
D_SSM = 1024
SSM_STATE = 128
SSM_GROUPS = 2
HEADS_PER_GROUP = SSM_HEADS // SSM_GROUPS
D_BC = 256
CHUNK = 128
N_EXPERTS = 16
CAPACITY_FACTOR = 2
D_FF = 2048
EPS = 1e-6
NEG_INF = -1e30

LANES = 128
HALO_ROWS = 8
TOKEN_ROWS = D_MODEL // LANES
PACKED_ROWS = TOKEN_ROWS // 2

QKV_COLS = D_ATTN + 2 * D_KV
COL_QK, COL_QV = D_ATTN, D_ATTN + D_KV
COL_Z, COL_X, COL_B, COL_DT = 0, 1024, 2048, 2560
MAIN_COLS = COL_DT + LANES

ATTN_HEAD_ORDER = tuple(8 * t + 4 * par + i for t in range(2) for i in range(4) for par in range(2))

SSD_BATCH_BLOCK = 4
ATTN_BATCH_BLOCK = 4

VMEM_LIMIT = 56 * 1024 * 1024
FFN_VMEM_LIMIT = 60 * 1024 * 1024


def _cparams(n_axes, vmem=None):
    return pltpu.CompilerParams(dimension_semantics=("arbitrary",) * n_axes,
                                vmem_limit_bytes=vmem)


def _rms(x, w):
    return x * lax.rsqrt(jnp.mean(x * x, axis=-1, keepdims=True) + EPS) * w


def _silu(x):
    return x * jax.nn.sigmoid(x)


def _inproj_kernel(x_ref, nw_ref, wq_ref, wr_ref, oq_ref, om_ref):
    hn = _rms(x_ref[...], nw_ref[...]).astype(BF16)
    oq_ref[...] = jnp.dot(hn, wq_ref[...], preferred_element_type=F32).astype(BF16)
    om_ref[...] = jnp.dot(hn, wr_ref[...], preferred_element_type=F32)


def _inproj(x2, norm_w, w_qkv, w_main, tm):
    T = x2.shape[0]
    resident = lambda a: pl.BlockSpec(a.shape, lambda i: (0, 0), pipeline_mode=pl.Buffered(1))
    return pl.pallas_call(
        _inproj_kernel,
        grid=(T // tm,),
        in_specs=[pl.BlockSpec((tm, D_MODEL), lambda i: (i, 0)), resident(norm_w), resident(w_qkv),
                  resident(w_main)],
        out_specs=[pl.BlockSpec((tm, QKV_COLS), lambda i: (i, 0)),
                   pl.BlockSpec((tm, MAIN_COLS), lambda i: (i, 0))],
        out_shape=[jax.ShapeDtypeStruct((T, QKV_COLS), BF16), jax.ShapeDtypeStruct((T, MAIN_COLS), F32)],
        compiler_params=_cparams(1, VMEM_LIMIT),
        name="inproj",
    )(x2, norm_w, w_qkv, w_main)


def _attn_kernel(q_ref, kp_ref, ko_ref, kn_ref, vp_ref, vo_ref, vn_ref, bias_ref, sink_ref, nw_ref,
                 o_ref, *, nb):
    i = pl.program_id(1)
    lane = lax.broadcasted_iota(I32, (3 * BLOCK, LANES), 1)
    m_lo = jnp.where(lane < HEAD_DIM, 1.0, 0.0).astype(BF16)
    m_up = jnp.where(lane < HEAD_DIM, 0.0, 1.0).astype(BF16)
    low_q = lax.broadcasted_iota(I32, (BLOCK, LANES), 1) < HEAD_DIM
    top = lax.broadcasted_iota(I32, (2 * BLOCK, 1), 0) < BLOCK
    tiles_per_pair = D_ATTN // LANES // (D_KV // LANES)

    def body(bb, edge):
        tiles = []
        for t in range(D_KV // LANES):
            sl = slice(t * LANES, (t + 1) * LANES)
            kt = jnp.concatenate([kp_ref[bb, :, sl], ko_ref[bb, :, sl], kn_ref[bb, :, sl]], axis=0)
            vt = jnp.concatenate([vp_ref[bb, :, sl], vo_ref[bb, :, sl], vn_ref[bb, :, sl]], axis=0)
            rhs = [(kt * m_lo, vt * m_lo + m_up), (kt * m_up, vt * m_up + m_lo)]
            for j in range(0, tiles_per_pair, 2):
                qa, qb = t * tiles_per_pair + j, t * tiles_per_pair + j + 1
                q2 = jnp.concatenate([q_ref[bb, :, qa * LANES:(qa + 1) * LANES],
                                      q_ref[bb, :, qb * LANES:(qb + 1) * LANES]], axis=0)
                res = []
                for par in range(2):
                    kz, vz = rhs[par]
                    ha, hb = ATTN_HEAD_ORDER[2 * qa + par], ATTN_HEAD_ORDER[2 * qb + par]
                    s = lax.dot_general(q2, kz, (((1,), (1,)), ((), ())), preferred_element_type=F32)
                    s = s + jnp.concatenate([bias_ref[ha], bias_ref[hb]], axis=0) + edge
                    sk = jnp.where(top, sink_ref[ha], sink_ref[hb])
                    m = jnp.maximum(jnp.max(s, axis=-1, keepdims=True), sk)
                    p = jnp.exp(s - m).astype(BF16)
                    pv = jnp.dot(p, vz, preferred_element_type=F32)
                    den = (pv[:, LANES - 1:LANES] if par == 0 else pv[:, 0:1]) + jnp.exp(sk - m)
                    res.append(pv * (1.0 / den))
                tiles.append(jnp.where(low_q, res[0][:BLOCK], res[1][:BLOCK]))
                tiles.append(jnp.where(low_q, res[0][BLOCK:], res[1][BLOCK:]))
        o = jnp.concatenate(tiles, axis=-1)
        o_ref[bb] = _rms(o, nw_ref[...]).astype(BF16)

    kcol = lax.broadcasted_iota(I32, (1, 3 * BLOCK), 1)
    edge = (jnp.where((kcol < BLOCK) & (i == 0), NEG_INF, 0.0)
            + jnp.where((kcol >= 2 * BLOCK) & (i == nb - 1), NEG_INF, 0.0))
    for bb in range(q_ref.shape[0]):
        body(bb, edge)


def _attention(qkv3, bias_tab, sink, norm_w):
    B, S, _ = qkv3.shape
    nb = S // BLOCK
    kcol, vcol = COL_QK // D_KV, COL_QV // D_KV
    prev = lambda i: jnp.maximum(i - 1, 0)
    nxt = lambda i: jnp.minimum(i + 1, nb - 1)
    nbt = ATTN_BATCH_BLOCK if B % ATTN_BATCH_BLOCK == 0 else 1
    kv = lambda col, f: pl.BlockSpec((nbt, BLOCK, D_KV), lambda b, i: (b, f(i), col))
    same = lambda i: i
    return pl.pallas_call(
        functools.partial(_attn_kernel, nb=nb),
        grid=(B // nbt, nb),
        in_specs=[pl.BlockSpec((nbt, BLOCK, D_ATTN), lambda b, i: (b, i, 0)),
                  kv(kcol, prev), kv(kcol, same), kv(kcol, nxt),
                  kv(vcol, prev), kv(vcol, same), kv(vcol, nxt),
                  pl.BlockSpec((N_HEADS, BLOCK, 3 * BLOCK), lambda b, i: (0, 0, 0)),
                  pl.BlockSpec(memory_space=pltpu.SMEM),
                  pl.BlockSpec((1, D_ATTN), lambda b, i: (0, 0))],
        out_specs=pl.BlockSpec((nbt, BLOCK, D_ATTN), lambda b, i: (b, i, 0)),
        out_shape=jax.ShapeDtypeStruct((B, S, D_ATTN), BF16),
        compiler_params=_cparams(2),
        name="attention",
    )(qkv3, qkv3, qkv3, qkv3, qkv3, qkv3, qkv3, bias_tab, sink, norm_w)


def _conv_silu(u, prev8, next8, w, b, has_prev, has_next):
    prev_row = jnp.where(has_prev, prev8[HALO_ROWS - 1:HALO_ROWS, :], 0.0)
    next_row = jnp.where(has_next, next8[0:1, :], 0.0)
    rid = lax.broadcasted_iota(I32, u.shape, 0)
    um = jnp.where(rid == 0, prev_row, pltpu.roll(u, 1, 0))
    up = jnp.where(rid == CHUNK - 1, next_row, pltpu.roll(u, CHUNK - 1, 0))
    y = b + um * w[0:1, :] + u * w[1:2, :] + up * w[2:3, :]
    return _silu(y)


def _ssd_kernel(*refs, reverse, nc):
    h_ref = refs[-1]

    @pl.when(pl.program_id(1) == 0)
    def _():
        h_ref[...] = jnp.zeros_like(h_ref)

    for bb in range(h_ref.shape[0]):
        _ssd_chunk(bb, refs, reverse, nc)


def _ssd_chunk(bb, refs, reverse, nc):
    if reverse:
        (z_ref, yf_ref, x_ref, xp_ref, xn_ref, bc_ref, bcp_ref, bcn_ref, dt_ref, cw_ref, cb_ref, dtb_ref,
         alog_ref, dskip_ref, nw_ref, o_ref, h_ref) = refs
    else:
        (x_ref, xp_ref, xn_ref, bc_ref, bcp_ref, bcn_ref, dt_ref, cw_ref, cb_ref, dtb_ref,
         alog_ref, o_ref, h_ref) = refs
    step = pl.program_id(1)
    c = (nc - 1 - step) if reverse else step
    has_prev, has_next = c > 0, c < nc - 1
    cw, cb = cw_ref[...], cb_ref[...]
    xc = _conv_silu(x_ref[bb], xp_ref[bb], xn_ref[bb], cw[:, :D_SSM], cb[:, :D_SSM], has_prev, has_next)
    bcv = _conv_silu(bc_ref[bb], bcp_ref[bb], bcn_ref[bb], cw[:, D_SSM:], cb[:, D_SSM:], has_prev, has_next)

    raw = dt_ref[bb] + dtb_ref[...]
    dt = jnp.maximum(raw, 0.0) + jnp.log1p(jnp.exp(-jnp.abs(raw)))
    a = dt * (-jnp.exp(alog_ref[...]))
    li = lax.broadcasted_iota(I32, (CHUNK, CHUNK), 0)
    si = lax.broadcasted_iota(I32, (CHUNK, CHUNK), 1)
    incl = jnp.dot(jnp.where(li >= si, 1.0, 0.0).astype(F32), a, precision=lax.Precision.HIGHEST,
                   preferred_element_type=F32)
    tot = incl[CHUNK - 1:CHUNK, :]
    if reverse:
        pcs = incl - a
        dstate = jnp.exp(pcs)
        yscale = jnp.exp(tot - pcs)
        mask = si >= li
    else:
        pcs = incl
        dstate = jnp.exp(tot - pcs)
        yscale = jnp.exp(pcs)
        mask = li >= si
    pcs_t = pcs.T
    lane0 = SSM_HEADS if reverse else 0
    P = SSM_HEAD_DIM
    gw = HEADS_PER_GROUP * P

    def spread(x, width, pieces):
        k = lax.broadcasted_iota(I32, (LANES, SSM_HEADS * width), 0)
        c = lax.broadcasted_iota(I32, (LANES, SSM_HEADS * width), 1)
        sel = jnp.where(k == lane0 + c // width, 1.0, 0.0).astype(BF16)
        out, rem = None, x
        for _ in range(pieces):
            piece = rem.astype(BF16)
            rem = rem - piece.astype(F32)
            d = jnp.dot(piece, sel, preferred_element_type=F32)
            out = d if out is None else out + d
        return out

    scales = spread(jnp.concatenate([dt, dstate, yscale, jnp.broadcast_to(jnp.exp(tot), (HALO_ROWS, LANES))], axis=0),
                    P, 2)
    dt_e, ds_e, ys_e = scales[:CHUNK], scales[CHUNK:2 * CHUNK], scales[2 * CHUNK:3 * CHUNK]
    cdec_e = scales[3 * CHUNK:3 * CHUNK + 1]
    col_e = spread(pcs, CHUNK, 3)
    xdt = xc * dt_e
    xdt_b = xdt.astype(BF16)
    xs_b = (xdt * ds_e).astype(BF16)
    low = lax.broadcasted_iota(I32, (CHUNK, LANES), 1) < P

    y_groups = []
    for g in range(SSM_GROUPS):
        bg = bcv[:, g * SSM_STATE:(g + 1) * SSM_STATE]
        cg = bcv[:, D_BC + g * SSM_STATE:D_BC + (g + 1) * SSM_STATE].astype(BF16)
        cbm = lax.dot_general(cg, bg.astype(BF16), (((1,), (1,)), ((), ())), preferred_element_type=F32)
        hg = h_ref[bb, g]
        yoff = jnp.dot(cg, hg.astype(BF16), preferred_element_type=F32)
        yd = []
        for jp in range(HEADS_PER_GROUP // 2):
            tile = (g * HEADS_PER_GROUP) // 2 + jp
            xpair = xdt_b[:, tile * LANES:(tile + 1) * LANES]
            halves = []
            for par in range(2):
                j = 2 * tile + par
                ln = lane0 + j
                col = col_e[:, j * CHUNK:(j + 1) * CHUNK]
                row = pcs_t[ln:ln + 1, :]
                seg = (row - col) if reverse else (col - row)
                lm = jnp.exp(jnp.where(mask, seg, -jnp.inf))
                halves.append(jnp.dot((cbm * lm).astype(BF16), xpair, preferred_element_type=F32))
            yd.append(jnp.where(low, halves[0], halves[1]))
        sl = slice(g * gw, (g + 1) * gw)
        y_groups.append(jnp.concatenate(yd, axis=1) + yoff * ys_e[:, sl])
        snew = jnp.dot(bg.T.astype(BF16), xs_b[:, sl], preferred_element_type=F32)
        h_ref[bb, g] = hg * cdec_e[:, sl] + snew
    y = jnp.concatenate(y_groups, axis=1)

    if not reverse:
        o_ref[bb] = y
    else:
        ytot = yf_ref[bb] + y + dskip_ref[...] * xc
        yz = ytot * _silu(z_ref[bb])
        halves = []
        for g in range(SSM_GROUPS):
            seg = yz[:, g * gw:(g + 1) * gw]
            halves.append(seg * lax.rsqrt(jnp.mean(seg * seg, axis=-1, keepdims=True) + EPS))
        o_ref[bb] = (jnp.concatenate(halves, axis=1) * nw_ref[...]).astype(BF16)


def _ssd_pass(proj3, conv_w3, conv_b, dtb, alog, reverse, yf=None, dskip=None, norm_w=None):
    B, S, _ = proj3.shape
    nc = S // CHUNK
    hp = CHUNK // HALO_ROWS
    nh = S // HALO_ROWS
    ch = (lambda s: nc - 1 - s) if reverse else (lambda s: s)
    nbt = SSD_BATCH_BLOCK if B % SSD_BATCH_BLOCK == 0 else 1
    main = lambda w, col: pl.BlockSpec((nbt, CHUNK, w), lambda b, s: (b, ch(s), col // w))
    halo_p = lambda w, col: pl.BlockSpec((nbt, HALO_ROWS, w),
                                         lambda b, s: (b, jnp.maximum(ch(s) * hp - 1, 0), col // w))
    halo_n = lambda w, col: pl.BlockSpec((nbt, HALO_ROWS, w),
                                         lambda b, s: (b, jnp.minimum((ch(s) + 1) * hp, nh - 1), col // w))
    full = lambda a: pl.BlockSpec(a.shape, lambda b, s: (0,) * a.ndim)
    d_bc2 = 2 * D_BC
    in_specs = [main(D_SSM, COL_X), halo_p(D_SSM, COL_X), halo_n(D_SSM, COL_X),
                main(d_bc2, COL_B), halo_p(d_bc2, COL_B), halo_n(d_bc2, COL_B),
                main(LANES, COL_DT), full(conv_w3), full(conv_b), full(dtb), full(alog)]
    args = [proj3, proj3, proj3, proj3, proj3, proj3, proj3, conv_w3, conv_b, dtb, alog]
    if reverse:
        in_specs = [main(D_SSM, COL_Z), pl.BlockSpec((nbt, CHUNK, D_SSM), lambda b, s: (b, ch(s), 0))] + in_specs
        in_specs += [full(dskip), full(norm_w)]
        args = [proj3, yf] + args + [dskip, norm_w]
    return pl.pallas_call(
        functools.partial(_ssd_kernel, reverse=reverse, nc=nc),
        grid=(B // nbt, nc),
        in_specs=in_specs,
        out_specs=pl.BlockSpec((nbt, CHUNK, D_SSM), lambda b, s: (b, ch(s), 0)),
        out_shape=jax.ShapeDtypeStruct((B, S, D_SSM), BF16 if reverse else F32),
        scratch_shapes=[pltpu.VMEM((nbt, SSM_GROUPS, SSM_STATE, HEADS_PER_GROUP * SSM_HEAD_DIM), F32)],
        compiler_params=_cparams(2),
        name="ssd_bwd" if reverse else "ssd_fwd",
    )(*args)


def _outproj_kernel(x_ref, a_ref, s_ref, w_ref, n2_ref, rw2_ref, x1_ref, h_ref, aff_ref):
    tm = x_ref.shape[0]
    x1 = (x_ref[...]
          + jnp.dot(a_ref[...], w_ref[:D_ATTN, :], preferred_element_type=F32)
          + jnp.dot(s_ref[...], w_ref[D_ATTN:, :], preferred_element_type=F32))
    x1_ref[...] = x1
    hn = _rms(x1, n2_ref[...])
    hn_hi = hn.astype(BF16)
    bits = pltpu.bitcast(hn_hi.astype(F32), I32)
    packed = lax.shift_right_logical(bits[:, :D_MODEL // 2], 16) | bits[:, D_MODEL // 2:]
    for s in range(PACKED_ROWS):
        h_ref[pl.ds(s, tm, stride=PACKED_ROWS), :] = packed[:, s * LANES:(s + 1) * LANES]
    hn_lo = (hn - hn_hi.astype(F32)).astype(BF16)
    l_hi = jnp.dot(hn_hi, rw2_ref[...], preferred_element_type=F32)
    l_lo = jnp.dot(hn_lo, rw2_ref[:, :N_EXPERTS], preferred_element_type=F32)
    logits = l_hi[:, :N_EXPERTS] + l_hi[:, N_EXPERTS:] + l_lo
    e = jnp.exp(logits - jnp.max(logits, axis=-1, keepdims=True))
    aff_ref[...] = e / jnp.sum(e, axis=-1, keepdims=True)


def _outproj(x2, attn2, ssm2, w_out, norm2_w, router_w, tm):
    T = x2.shape[0]
    row = lambda w: pl.BlockSpec((tm, w), lambda i: (i, 0))
    full = lambda a: pl.BlockSpec(a.shape, lambda i: (0,) * a.ndim)
    return pl.pallas_call(
        _outproj_kernel,
        grid=(T // tm,),
        in_specs=[row(D_MODEL), row(D_ATTN), row(D_SSM), full(w_out), full(norm2_w), full(router_w)],
        out_specs=[row(D_MODEL), pl.BlockSpec((tm * PACKED_ROWS, LANES), lambda i: (i, 0)), row(N_EXPERTS)],
        out_shape=[jax.ShapeDtypeStruct((T, D_MODEL), F32),
                   jax.ShapeDtypeStruct((T * PACKED_ROWS, LANES), I32),
                   jax.ShapeDtypeStruct((T, N_EXPERTS), F32)],
        compiler_params=_cparams(1, VMEM_LIMIT),
        name="outproj_router",
    )(x2, attn2, ssm2, w_out, norm2_w, router_w)


def _count(m):
    c = jnp.sum(jnp.where(m, 1.0, 0.0), axis=0, keepdims=True)
    return jnp.sum(c, axis=1, keepdims=True)


def _tri(n, m, fn):
    return jnp.where(fn(lax.broadcasted_iota(I32, (n, m), 0), lax.broadcasted_iota(I32, (n, m), 1)),
                     1.0, 0.0).astype(BF16)


def _dot_u16(lhs01, x):
    hi = jnp.floor(x * (1.0 / 256.0))
    lo = x - hi * 256.0
    return (jnp.dot(lhs01, hi.astype(BF16), preferred_element_type=F32) * 256.0
            + jnp.dot(lhs01, lo.astype(BF16), preferred_element_type=F32))


def _cumsum_rowmajor(x):
    R = x.shape[0]
    within = jnp.dot(x.astype(BF16), _tri(LANES, LANES, lambda k, l: k <= l), preferred_element_type=F32)
    rowtot = jnp.broadcast_to(within[:, LANES - 1:LANES], (R, LANES))
    before = _dot_u16(_tri(R, R, lambda i, k: k < i), rowtot)
    return within + before


def _route_kernel(a_ref, idx_ref, dst_ref, g_ref, cs_ref, ce_ref, sel_s, cnt_s, cs_s, rank_s, *, cap):
    s = pl.program_id(0)
    R = a_ref.shape[0]

    @pl.when(s < N_EXPERTS)
    def _select():
        bits = pltpu.bitcast(a_ref[...], I32)
        capf = jnp.float32(cap)

        def body(k, tau):
            cand = tau | lax.shift_left(jnp.int32(1), 30 - k)
            return jnp.where(_count(bits >= cand) >= capf, cand, tau)

        tau = lax.fori_loop(0, 31, body, jnp.zeros((1, 1), I32))
        gt = bits > tau
        eq = bits == tau
        need = capf - _count(gt)
        ties = _cumsum_rowmajor(jnp.where(eq, 1.0, 0.0))
        sel = jnp.where(gt | (eq & (ties <= need)), 1.0, 0.0)
        sel_s[s] = sel

        @pl.when(s == 0)
        def _():
            cnt_s[...] = sel

        @pl.when(s > 0)
        def _():
            cnt_s[...] += sel

    @pl.when(s == N_EXPERTS)
    def _prefix():
        cnt = cnt_s[...]
        ce = _cumsum_rowmajor(cnt)
        cs_s[...] = ce - cnt
        rank_s[...] = jnp.zeros_like(rank_s)
        cs_ref[...] = ce - cnt
        ce_ref[...] = ce

    @pl.when(s >= N_EXPERTS)
    def _invert():
        e = s - N_EXPERTS
        sel = sel_s[e]
        rank = rank_s[...]
        q = cs_s[...] + rank
        rank_s[...] = rank + sel
        a = a_ref[...]
        within = jnp.dot(sel.astype(BF16), _tri(LANES, LANES, lambda k, l: k <= l), preferred_element_type=F32)
        n_b = jnp.broadcast_to(within[:, LANES - 1:LANES], (R, LANES))
        rowend = jnp.dot(_tri(R, R, lambda i, k: k <= i), n_b.astype(BF16), preferred_element_type=F32)
        slot = lax.broadcasted_iota(I32, (R, cap), 1).astype(F32)
        done = jnp.where(rowend[:, 0:1] <= slot, 1.0, 0.0)
        ones = jnp.ones((8, R), BF16)
        row_p = jnp.dot(ones, done.astype(BF16), preferred_element_type=F32)[0:1]
        start_p = jnp.dot(ones, (done * n_b[:, 0:1]).astype(BF16), preferred_element_type=F32)[0:1]
        onehot = jnp.where(lax.broadcasted_iota(I32, (R, cap), 0).astype(F32) == row_p, 1.0, 0.0).astype(BF16)
        w_t = jnp.dot(within.T.astype(BF16), onehot, preferred_element_type=F32)
        k_in_row = lax.broadcasted_iota(I32, (1, cap), 1).astype(F32) - start_p
        lane_p = jnp.sum(jnp.where(w_t <= k_in_row, 1.0, 0.0), axis=0, keepdims=True)
        pick = lax.broadcasted_iota(I32, (LANES, cap), 0).astype(F32) == lane_p

        def take(x):
            v = jnp.dot(x.T.astype(BF16), onehot, preferred_element_type=F32)
            return jnp.sum(jnp.where(pick, v, 0.0), axis=0, keepdims=True)

        q_hi = jnp.floor(q * (1.0 / 256.0))
        a_hi = a.astype(BF16).astype(F32)
        a_mid = (a - a_hi).astype(BF16).astype(F32)
        a_lo = a - a_hi - a_mid
        idx_ref[...] = (row_p * LANES + lane_p).astype(I32)
        dst_ref[...] = (take(q_hi) * 256.0 + take(q - q_hi * 256.0)).astype(I32)
        g_ref[...] = take(a_hi) + take(a_mid) + take(a_lo)


def _route(aff_t3, cap):
    E, R, _ = aff_t3.shape
    assert R <= 256 and E == N_EXPERTS
    slot_spec = pl.BlockSpec((None, 1, cap), lambda s: (jnp.maximum(s - N_EXPERTS, 0), 0, 0))
    tok_spec = pl.BlockSpec((R, LANES), lambda s: (0, 0))
    return pl.pallas_call(
        functools.partial(_route_kernel, cap=cap),
        grid=(2 * E,),
        in_specs=[pl.BlockSpec((None, R, LANES), lambda s: (s % N_EXPERTS, 0, 0))],
        out_specs=[slot_spec, slot_spec, slot_spec, tok_spec, tok_spec],
        out_shape=[jax.ShapeDtypeStruct((E, 1, cap), I32), jax.ShapeDtypeStruct((E, 1, cap), I32),
                   jax.ShapeDtypeStruct((E, 1, cap), F32),
                   jax.ShapeDtypeStruct((R, LANES), F32), jax.ShapeDtypeStruct((R, LANES), F32)],
        scratch_shapes=[pltpu.VMEM((E, R, LANES), F32), pltpu.VMEM((R, LANES), F32),
                        pltpu.VMEM((R, LANES), F32), pltpu.VMEM((R, LANES), F32)],
        compiler_params=_cparams(1, VMEM_LIMIT),
        name="route",
    )(aff_t3)


def _ffn_kernel(idx_ref, dst_ref, h_hbm, gt_ref, wg_ref, wu_ref, wd_ref, c_hbm,
                xs_stage, xsb, hid, o_stage, gsem, ssem, *, tm, cap, nf):
    e, m, f = pl.program_id(0), pl.program_id(1), pl.program_id(2)
    base = e * cap + m * tm
    tr = TOKEN_ROWS
    part = tm // nf
    tf = wg_ref.shape[1]
    tn = wd_ref.shape[1]
    is_first = (e == 0) & (m == 0)
    is_last = (e == pl.num_programs(0) - 1) & (m == pl.num_programs(1) - 1)

    def token_rows(t):
        return pl.ds(pl.multiple_of(t * tr, tr), tr)

    def packed_rows(t):
        return pl.ds(pl.multiple_of(t * PACKED_ROWS, PACKED_ROWS), PACKED_ROWS)

    def gather_rows(tile_base, r0, n):
        def issue(r, carry):
            pltpu.make_async_copy(h_hbm.at[packed_rows(idx_ref[tile_base + r0 + r]), :],
                                  xs_stage.at[packed_rows(r0 + r), :], gsem).start()
            return carry

        lax.fori_loop(0, n, issue, 0, unroll=8)

    def scatter_rows(tile_base, r0, n):
        def issue(r, carry):
            pltpu.make_async_copy(o_stage.at[token_rows(r0 + r), :],
                                  c_hbm.at[token_rows(dst_ref[tile_base + r0 + r]), :], ssem).start()
            return carry

        lax.fori_loop(0, n, issue, 0, unroll=8)

    def scatter_done():
        pltpu.make_async_copy(o_stage, c_hbm.at[pl.ds(0, tm * tr), :], ssem).wait()

    def gather_done():
        pltpu.make_async_copy(h_hbm.at[pl.ds(0, tm * PACKED_ROWS), :], xs_stage, gsem).wait()

    @pl.when(f == 0)
    def _stage_in():
        @pl.when(is_first)
        def _():
            gather_rows(base, 0, tm)
            o_stage[...] = jnp.zeros_like(o_stage)

        gather_done()
        half = D_MODEL // 2
        for s in range(PACKED_ROWS):
            u = xs_stage[pl.ds(s, tm, stride=PACKED_ROWS), :]
            lo = pltpu.bitcast(lax.shift_left(u, 16), F32)
            hi = pltpu.bitcast(u & jnp.int32(-65536), F32)
            xsb[:, s * LANES:(s + 1) * LANES] = lo.astype(BF16)
            xsb[:, half + s * LANES:half + (s + 1) * LANES] = hi.astype(BF16)

    nxt = jnp.where(is_last, base, base + tm)
    prv = jnp.maximum(base - tm, 0)

    for k in range(nf):
        @pl.when(f == k)
        def _gate_up(k=k):
            xs = xsb[...]
            g = jnp.dot(xs, wg_ref[...].astype(BF16), preferred_element_type=F32)
            u = jnp.dot(xs, wu_ref[...].astype(BF16), preferred_element_type=F32)
            hid[:, k * tf:(k + 1) * tf] = (_silu(g) * u).astype(BF16)
            for r in range(k * part, (k + 1) * part):
                pltpu.make_async_copy(h_hbm.at[packed_rows(idx_ref[nxt + r]), :],
                                      xs_stage.at[packed_rows(r), :], gsem).start()
                pltpu.make_async_copy(o_stage.at[token_rows(r), :],
                                      c_hbm.at[token_rows(dst_ref[prv + r]), :], ssem).start()

    for n in range(nf):
        @pl.when(f == nf + n)
        def _down(n=n):
            if n == 0:
                scatter_done()

            out = jnp.dot(hid[...], wd_ref[...].astype(BF16), preferred_element_type=F32)
            for j in range(tm // LANES):
                rows = out[j * LANES:(j + 1) * LANES, :] * gt_ref[:, j:j + 1]
                for c in range(tn // LANES):
                    o_stage[pl.ds(j * LANES * tr + n * (tn // LANES) + c, LANES, stride=tr), :] = (
                        rows[:, c * LANES:(c + 1) * LANES])

            if n == nf - 1:
                @pl.when(is_last)
                def _():
                    scatter_rows(base, 0, tm)
                    scatter_done()
                    gather_done()


def _ffn(idx, dst, h_rows, g_t, w_gate, w_up, w_down, tm, tf):
    E, n_m = g_t.shape[0], g_t.shape[1]
    cap = n_m * tm
    n_contrib = E * cap
    nf = D_FF // tf
    tn = D_MODEL // nf
    up = lambda f: jnp.minimum(f, nf - 1)
    down = lambda f: jnp.maximum(f - nf, 0)
    grid_spec = pltpu.PrefetchScalarGridSpec(
        num_scalar_prefetch=2,
        grid=(E, n_m, 2 * nf),
        in_specs=[pl.BlockSpec(memory_space=pl.ANY),
                  pl.BlockSpec((None, None, LANES, tm // LANES), lambda e, m, f, i, d: (e, m, 0, 0)),
                  pl.BlockSpec((None, D_MODEL, tf), lambda e, m, f, i, d: (e, 0, up(f))),
                  pl.BlockSpec((None, D_MODEL, tf), lambda e, m, f, i, d: (e, 0, up(f))),
                  pl.BlockSpec((None, D_FF, tn), lambda e, m, f, i, d: (e, 0, down(f)))],
        out_specs=pl.BlockSpec(memory_space=pl.ANY),
        scratch_shapes=[pltpu.VMEM((tm * PACKED_ROWS, LANES), I32), pltpu.VMEM((tm, D_MODEL), BF16),
                        pltpu.VMEM((tm, D_FF), BF16), pltpu.VMEM((tm * TOKEN_ROWS, LANES), F32),
                        pltpu.SemaphoreType.DMA, pltpu.SemaphoreType.DMA],
    )
    return pl.pallas_call(
        functools.partial(_ffn_kernel, tm=tm, cap=cap, nf=nf),
        grid_spec=grid_spec,
        out_shape=jax.ShapeDtypeStruct((n_contrib * TOKEN_ROWS, LANES), F32),
        compiler_params=_cparams(3, FFN_VMEM_LIMIT),
        name="expert_ffn",
    )(idx, dst, h_rows, g_t, w_gate, w_up, w_down)


def _combine_kernel(ch_ref, tt_ref, flag_ref, x1_ref, cs_ref, ce_ref, c_ref, nw_ref, out_ref, acc_ref, *, cg, rows):
    k = pl.program_id(0)
    flag = flag_ref[k]

    @pl.when((flag & 2) > 0)
    def _():
        acc_ref[...] = x1_ref[...]

    @pl.when((flag & 1) > 0)
    def _():
        cio = (lax.broadcasted_iota(I32, (cg, LANES), 0) + ch_ref[k] * cg).astype(F32)
        parts = [jnp.where((cs_ref[r:r + 1, :] <= cio) & (cio < ce_ref[r:r + 1, :]), 1.0, 0.0) for r in range(rows)]
        a = jnp.concatenate(parts, axis=1).T.astype(BF16)
        chunk = jnp.concatenate([c_ref[pl.ds(s, cg, stride=TOKEN_ROWS), :].astype(BF16)
                                 for s in range(TOKEN_ROWS)], axis=1)
        acc_ref[...] += jnp.dot(a, chunk, preferred_element_type=F32)

    @pl.when((flag & 4) > 0)
    def _():
        out_ref[...] = _rms(acc_ref[...], nw_ref[...])


def _combine(ch, tt, flag, x1, cs3, ce3, contrib, final_w, tb, cg):
    T = x1.shape[0]
    rows = tb // LANES
    n_pairs = ch.shape[0]
    grid_spec = pltpu.PrefetchScalarGridSpec(
        num_scalar_prefetch=3,
        grid=(n_pairs,),
        in_specs=[pl.BlockSpec((tb, D_MODEL), lambda k, ch, tt, fl: (tt[k], 0)),
                  pl.BlockSpec((None, rows, LANES), lambda k, ch, tt, fl: (tt[k], 0, 0)),
                  pl.BlockSpec((None, rows, LANES), lambda k, ch, tt, fl: (tt[k], 0, 0)),
                  pl.BlockSpec((cg * TOKEN_ROWS, LANES), lambda k, ch, tt, fl: (ch[k], 0)),
                  pl.BlockSpec((1, D_MODEL), lambda k, ch, tt, fl: (0, 0))],
        out_specs=pl.BlockSpec((tb, D_MODEL), lambda k, ch, tt, fl: (tt[k], 0)),
        scratch_shapes=[pltpu.VMEM((tb, D_MODEL), F32)],
    )
    return pl.pallas_call(
        functools.partial(_combine_kernel, cg=cg, rows=rows),
        grid_spec=grid_spec,
        out_shape=jax.ShapeDtypeStruct((T, D_MODEL), F32),
        compiler_params=_cparams(1, VMEM_LIMIT),
        name="combine_final",
    )(ch, tt, flag, x1, cs3, ce3, contrib, final_w)


def _combine_schedule(cs, ce, tb, cg):
    T = cs.size
    ntt = T // tb
    total = CAPACITY_FACTOR * T
    nch = total // cg
    lo = cs.reshape(ntt, tb)[:, 0].astype(I32)
    hi = ce.reshape(ntt, tb)[:, -1].astype(I32)
    first = jnp.minimum(lo, total - 1) // cg
    last = jnp.where(hi > lo, (hi - 1) // cg, first)
    n = last - first + 1
    ends = jnp.cumsum(n)
    offs = ends - n
    n_pairs = ntt + nch
    ks = jnp.arange(n_pairs, dtype=I32)
    tile = jnp.minimum(jnp.searchsorted(ends, ks, side="right").astype(I32), ntt - 1)
    valid = ks < ends[-1]
    chunk = jnp.where(valid, first[tile] + ks - offs[tile], last[-1])
    flag = jnp.where(valid, 1 + 2 * (ks == offs[tile]) + 4 * (ks == ends[tile] - 1), 0).astype(I32)
    return chunk.astype(I32), tile, flag


def _t5_buckets(rel):
    nb = NUM_BUCKETS // 2
    ret = (rel > 0).astype(np.int32) * nb
    n = np.abs(rel)
    max_exact = nb // 2
    large = max_exact + (np.log(np.maximum(n, 1) / max_exact) / np.log(MAX_DISTANCE / max_exact)
                         * (nb - max_exact)).astype(np.int32)
    large = np.minimum(large, nb - 1)
    return ret + np.where(n < max_exact, n, large)


def _tile(n, pref):
    t = min(n, pref)
    assert n % t == 0
    return t


def _trunk(x, w):
    B, S, _ = x.shape
    T = B * S
    cap = CAPACITY_FACTOR * T // N_EXPERTS
    x2 = x.reshape(T, D_MODEL)
    tm = _tile(T, 512)

    qkv, main = _inproj(x2, w["norm1_w"], w["w_qkv"], w["w_main"], tm)
    main3 = main.reshape(B, S, MAIN_COLS)
    attn = _attention(qkv.reshape(B, S, QKV_COLS), w["bias_tab"], w["attn_sink"], w["attn_norm_w"])
    yf = _ssd_pass(main3, w["conv_w"], w["conv_b"], w["dtb"], w["alog"], reverse=False)
    ssm = _ssd_pass(main3, w["conv_w"], w["conv_b"], w["dtb"], w["alog"], reverse=True,
                    yf=yf, dskip=w["dskip"], norm_w=w["ssm_norm_w"])
    x1, h_rows, aff = _outproj(x2, attn.reshape(T, D_ATTN), ssm.reshape(T, D_SSM), w["w_out"], w["norm2_w"],
                          w["router_w2"], tm)

    tb = _tile(T, 512)
    cg = 512
    idx, dst, g, cs, ce = _route(aff.T.reshape(N_EXPERTS, T // LANES, LANES), cap)
    tmf = _tile(cap, 1024)
    g_t = jnp.swapaxes(g.reshape(N_EXPERTS, cap // tmf, tmf // LANES, LANES), 2, 3)
    contrib = _ffn(idx.reshape(-1), dst.reshape(-1), h_rows, g_t, w["w_gate"], w["w_up"], w["w_down"], tmf, 512)
    ch, tt, flag = _combine_schedule(cs, ce, tb, cg)
    tile3 = lambda a: a.reshape(T // tb, tb // LANES, LANES)
    y = _combine(ch, tt, flag, x1, tile3(cs), tile3(ce), contrib, w["final_norm_w"], tb, cg)
    return y.reshape(B, S, D_MODEL)


def _prep_weights(rel_bias, norm1_w, w_in, conv_w, conv_b, dt_bias_fwd, dt_bias_bwd, a_log_fwd, a_log_bwd,
                  d_skip, ssm_norm_w, attn_sink, attn_norm_w, w_out, norm2_w, router_w, w_gate, w_up, w_down,
                  final_norm_w):
    o1 = D_ATTN; o2 = o1 + D_KV; o3 = o2 + D_KV; o4 = o3 + D_SSM; o5 = o4 + D_SSM + 2 * D_BC
    wi = w_in[0]
    scale = 1.0 / math.sqrt(HEAD_DIM)
    perm = np.concatenate([np.arange(h * HEAD_DIM, (h + 1) * HEAD_DIM) for h in ATTN_HEAD_ORDER])
    w_qkv = jnp.concatenate([(wi[:, :o1] * scale)[:, perm], wi[:, o1:o3]], axis=1).astype(BF16)
    w_out_p = jnp.concatenate([w_out[0][:D_ATTN][perm], w_out[0][D_ATTN:]], axis=0).astype(BF16)
    w_main = jnp.concatenate([wi[:, o3:], jnp.zeros((D_MODEL, LANES - 2 * SSM_HEADS), F32)], axis=1).astype(BF16)
    rel = np.arange(3 * BLOCK)[None, :] - BLOCK - np.arange(BLOCK)[:, None]
    onehot = (jnp.asarray(_t5_buckets(rel), I32)[..., None] == jnp.arange(NUM_BUCKETS, dtype=I32)).astype(F32)
    bias_tab = jnp.einsum("qsn,nh->hqs", onehot, rel_bias.astype(F32), precision=lax.Precision.HIGHEST)
    bias_tab = jnp.where(jnp.asarray(np.abs(rel) <= WINDOW)[None], bias_tab, NEG_INF)
    rw = router_w[0].astype(F32)
    rw_hi = rw.astype(BF16)
    router_w2 = jnp.concatenate([rw_hi, (rw - rw_hi.astype(F32)).astype(BF16)], axis=1)
    pad = jnp.zeros((LANES - 2 * SSM_HEADS,), F32)
    row = lambda a: a.reshape(1, -1).astype(F32)
    return dict(
        norm1_w=row(norm1_w[0]), w_qkv=w_qkv, w_main=w_main, bias_tab=bias_tab, attn_sink=attn_sink[0].astype(F32),
        attn_norm_w=row(attn_norm_w[0][perm]), conv_w=conv_w[0].T.astype(F32), conv_b=row(conv_b[0]),
        dtb=row(jnp.concatenate([dt_bias_fwd[0], dt_bias_bwd[0], pad])),
        alog=row(jnp.concatenate([a_log_fwd[0], a_log_bwd[0], pad])),
        dskip=row(jnp.repeat(d_skip[0], SSM_HEAD_DIM)), ssm_norm_w=row(ssm_norm_w[0]),
        w_out=w_out_p, norm2_w=row(norm2_w[0]), router_w2=router_w2,
        w_gate=w_gate[0], w_up=w_up[0], w_down=w_down[0],
        final_norm_w=row(final_norm_w))


def kernel(x_prompt, x_sample, rel_bias, norm1_w, w_in, conv_w, conv_b, dt_bias_fwd, dt_bias_bwd, a_log_fwd, a_log_bwd, d_skip, ssm_norm_w, attn_sink, attn_norm_w, w_out, norm2_w, router_w, w_gate, w_up, w_down, final_norm_w):
    assert norm1_w.shape[0] == 1
    w = _prep_weights(rel_bias, norm1_w, w_in, conv_w, conv_b, dt_bias_fwd, dt_bias_bwd, a_log_fwd, a_log_bwd,
                      d_skip, ssm_norm_w, attn_sink, attn_norm_w, w_out, norm2_w, router_w, w_gate, w_up,
                      w_down, final_norm_w)
    return (_trunk(x_prompt, w), _trunk(x_sample, w))
```

```python
import functools
import math

import numpy as np
import jax
import jax.numpy as jnp
from jax import lax
from jax.experimental import pallas as pl
from jax.experimental.pallas import tpu as pltpu

F32 = jnp.float32
BF16 = jnp.bfloat16
I32 = jnp.int32

D_MODEL = 2048
HEAD_DIM = 64
N_HEADS = 16
N_KV_HEADS = 4
D_ATTN = 1024
D_KV = 256
WINDOW = 128
BLOCK = 128
NUM_BUCKETS = 32
MAX_DISTANCE = 128
SSM_HEAD_DIM = 64
SSM_HEADS = 16
D_SSM = 1024
SSM_STATE = 128
SSM_GROUPS = 2
HEADS_PER_GROUP = SSM_HEADS // SSM_GROUPS
D_BC = 256
CHUNK = 128
N_EXPERTS = 16
CAPACITY_FACTOR = 2
D_FF = 2048
EPS = 1e-6
NEG_INF = -1e30

LANES = 128
HALO_ROWS = 8
TOKEN_ROWS = D_MODEL // LANES
PACKED_ROWS = TOKEN_ROWS // 2
CONTRIB_TN = 512

QKV_COLS = D_ATTN + 2 * D_KV
COL_QK, COL_QV = D_ATTN, D_ATTN + D_KV
COL_Z, COL_X, COL_B, COL_DT = 0, 1024, 2048, 2560
MAIN_COLS = COL_DT + LANES

ATTN_HEAD_ORDER = tuple(8 * t + 4 * par + i for t in range(2) for i in range(4) for par in range(2))

SSD_BATCH_BLOCK = 4
ATTN_BATCH_BLOCK = 4

VMEM_LIMIT = 56 * 1024 * 1024
FFN_VMEM_LIMIT = 60 * 1024 * 1024


def _cparams(n_axes, vmem=None):
    return pltpu.CompilerParams(dimension_semantics=("arbitrary",) * n_axes,
                                vmem_limit_bytes=vmem)


def _rms(x, w):
    return x * lax.rsqrt(jnp.mean(x * x, axis=-1, keepdims=True) + EPS) * w


def _silu(x):
    return x * jax.nn.sigmoid(x)


def _inproj_kernel(x_ref, nw_ref, wq_ref, wr_ref, oq_ref, om_ref):
    hn = _rms(x_ref[...], nw_ref[...]).astype(BF16)
    oq_ref[...] = jnp.dot(hn, wq_ref[...], preferred_element_type=F32).astype(BF16)
    om_ref[...] = jnp.dot(hn, wr_ref[...], preferred_element_type=F32)


def _inproj(x2, norm_w, w_qkv, w_main, tm):
    T = x2.shape[0]
    resident = lambda a: pl.BlockSpec(a.shape, lambda i: (0, 0), pipeline_mode=pl.Buffered(1))
    return pl.pallas_call(
        _inproj_kernel,
        grid=(T // tm,),
        in_specs=[pl.BlockSpec((tm, D_MODEL), lambda i: (i, 0)), resident(norm_w), resident(w_qkv),
                  resident(w_main)],
        out_specs=[pl.BlockSpec((tm, QKV_COLS), lambda i: (i, 0)),
                   pl.BlockSpec((tm, MAIN_COLS), lambda i: (i, 0))],
        out_shape=[jax.ShapeDtypeStruct((T, QKV_COLS), BF16), jax.ShapeDtypeStruct((T, MAIN_COLS), F32)],
        compiler_params=_cparams(1, VMEM_LIMIT),
        name="inproj",
    )(x2, norm_w, w_qkv, w_main)


def _attn_kernel(q_ref, kp_ref, ko_ref, kn_ref, vp_ref, vo_ref, vn_ref, bias_ref, sink_ref, nw_ref,
                 o_ref, *, nb):
    i = pl.program_id(1)
    lane = lax.broadcasted_iota(I32, (3 * BLOCK, LANES), 1)
    m_lo = jnp.where(lane < HEAD_DIM, 1.0, 0.0).astype(BF16)
    m_up = jnp.where(lane < HEAD_DIM, 0.0, 1.0).astype(BF16)
    low_q = lax.broadcasted_iota(I32, (BLOCK, LANES), 1) < HEAD_DIM
    top = lax.broadcasted_iota(I32, (2 * BLOCK, 1), 0) < BLOCK
    tiles_per_pair = D_ATTN // LANES // (D_KV // LANES)

    def body(bb, edge):
        tiles = []
        for t in range(D_KV // LANES):
            sl = slice(t * LANES, (t + 1) * LANES)
            kt = jnp.concatenate([kp_ref[bb, :, sl], ko_ref[bb, :, sl], kn_ref[bb, :, sl]], axis=0)
            vt = jnp.concatenate([vp_ref[bb, :, sl], vo_ref[bb, :, sl], vn_ref[bb, :, sl]], axis=0)
            rhs = [(kt * m_lo, vt * m_lo + m_up), (kt * m_up, vt * m_up + m_lo)]
            for j in range(0, tiles_per_pair, 2):
                qa, qb = t * tiles_per_pair + j, t * tiles_per_pair + j + 1
                q2 = jnp.concatenate([q_ref[bb, :, qa * LANES:(qa + 1) * LANES],
                                      q_ref[bb, :, qb * LANES:(qb + 1) * LANES]], axis=0)
                res = []
                for par in range(2):
                    kz, vz = rhs[par]
                    ha, hb = ATTN_HEAD_ORDER[2 * qa + par], ATTN_HEAD_ORDER[2 * qb + par]
                    s = lax.dot_general(q2, kz, (((1,), (1,)), ((), ())), preferred_element_type=F32)
                    s = s + jnp.concatenate([bias_ref[ha], bias_ref[hb]], axis=0) + edge
                    sk = jnp.where(top, sink_ref[ha], sink_ref[hb])
                    m = jnp.maximum(jnp.max(s, axis=-1, keepdims=True), sk)
                    p = jnp.exp(s - m).astype(BF16)
                    pv = jnp.dot(p, vz, preferred_element_type=F32)
                    den = (pv[:, LANES - 1:LANES] if par == 0 else pv[:, 0:1]) + jnp.exp(sk - m)
                    res.append(pv * (1.0 / den))
                tiles.append(jnp.where(low_q, res[0][:BLOCK], res[1][:BLOCK]))
                tiles.append(jnp.where(low_q, res[0][BLOCK:], res[1][BLOCK:]))
        o = jnp.concatenate(tiles, axis=-1)
        o_ref[bb] = _rms(o, nw_ref[...]).astype(BF16)

    kcol = lax.broadcasted_iota(I32, (1, 3 * BLOCK), 1)
    edge = (jnp.where((kcol < BLOCK) & (i == 0), NEG_INF, 0.0)
            + jnp.where((kcol >= 2 * BLOCK) & (i == nb - 1), NEG_INF, 0.0))
    for bb in range(q_ref.shape[0]):
        body(bb, edge)


def _attention(qkv3, bias_tab, sink, norm_w):
    B, S, _ = qkv3.shape
    nb = S // BLOCK
    kcol, vcol = COL_QK // D_KV, COL_QV // D_KV
    prev = lambda i: jnp.maximum(i - 1, 0)
    nxt = lambda i: jnp.minimum(i + 1, nb - 1)
    nbt = ATTN_BATCH_BLOCK if B % ATTN_BATCH_BLOCK == 0 else 1
    kv = lambda col, f: pl.BlockSpec((nbt, BLOCK, D_KV), lambda b, i: (b, f(i), col))
    same = lambda i: i
    return pl.pallas_call(
        functools.partial(_attn_kernel, nb=nb),
        grid=(B // nbt, nb),
        in_specs=[pl.BlockSpec((nbt, BLOCK, D_ATTN), lambda b, i: (b, i, 0)),
                  kv(kcol, prev), kv(kcol, same), kv(kcol, nxt),
                  kv(vcol, prev), kv(vcol, same), kv(vcol, nxt),
                  pl.BlockSpec((N_HEADS, BLOCK, 3 * BLOCK), lambda b, i: (0, 0, 0)),
                  pl.BlockSpec(memory_space=pltpu.SMEM),
                  pl.BlockSpec((1, D_ATTN), lambda b, i: (0, 0))],
        out_specs=pl.BlockSpec((nbt, BLOCK, D_ATTN), lambda b, i: (b, i, 0)),
        out_shape=jax.ShapeDtypeStruct((B, S, D_ATTN), BF16),
        compiler_params=_cparams(2),
        name="attention",
    )(qkv3, qkv3, qkv3, qkv3, qkv3, qkv3, qkv3, bias_tab, sink, norm_w)


def _conv_silu(u, prev8, next8, w, b, has_prev, has_next):
    prev_row = jnp.where(has_prev, prev8[HALO_ROWS - 1:HALO_ROWS, :], 0.0)
    next_row = jnp.where(has_next, next8[0:1, :], 0.0)
    rid = lax.broadcasted_iota(I32, u.shape, 0)
    um = jnp.where(rid == 0, prev_row, pltpu.roll(u, 1, 0))
    up = jnp.where(rid == CHUNK - 1, next_row, pltpu.roll(u, CHUNK - 1, 0))
    y = b + um * w[0:1, :] + u * w[1:2, :] + up * w[2:3, :]
    return _silu(y)


def _ssd_kernel(*refs, reverse, nc):
    h_ref = refs[-1]

    @pl.when(pl.program_id(1) == 0)
    def _():
        h_ref[...] = jnp.zeros_like(h_ref)

    for bb in range(h_ref.shape[0]):
        _ssd_chunk(bb, refs, reverse, nc)


def _ssd_chunk(bb, refs, reverse, nc):
    if reverse:
        (z_ref, yf_ref, x_ref, xp_ref, xn_ref, bc_ref, bcp_ref, bcn_ref, dt_ref, cw_ref, cb_ref, dtb_ref,
         alog_ref, dskip_ref, nw_ref, o_ref, h_ref) = refs
    else:
        (x_ref, xp_ref, xn_ref, bc_ref, bcp_ref, bcn_ref, dt_ref, cw_ref, cb_ref, dtb_ref,
         alog_ref, o_ref, h_ref) = refs
    step = pl.program_id(1)
    c = (nc - 1 - step) if reverse else step
    has_prev, has_next = c > 0, c < nc - 1
    cw, cb = cw_ref[...], cb_ref[...]
    xc = _conv_silu(x_ref[bb], xp_ref[bb], xn_ref[bb], cw[:, :D_SSM], cb[:, :D_SSM], has_prev, has_next)
    bcv = _conv_silu(bc_ref[bb], bcp_ref[bb], bcn_ref[bb], cw[:, D_SSM:], cb[:, D_SSM:], has_prev, has_next)

    raw = dt_ref[bb] + dtb_ref[...]
    dt = jnp.maximum(raw, 0.0) + jnp.log1p(jnp.exp(-jnp.abs(raw)))
    a = dt * (-jnp.exp(alog_ref[...]))
    li = lax.broadcasted_iota(I32, (CHUNK, CHUNK), 0)
    si = lax.broadcasted_iota(I32, (CHUNK, CHUNK), 1)
    incl = jnp.dot(jnp.where(li >= si, 1.0, 0.0).astype(F32), a, precision=lax.Precision.HIGHEST,
                   preferred_element_type=F32)
    tot = incl[CHUNK - 1:CHUNK, :]
    if reverse:
        pcs = incl - a
        dstate = jnp.exp(pcs)
        yscale = jnp.exp(tot - pcs)
        mask = si >= li
    else:
        pcs = incl
        dstate = jnp.exp(tot - pcs)
        yscale = jnp.exp(pcs)
        mask = li >= si
    pcs_t = pcs.T
    lane0 = SSM_HEADS if reverse else 0
    P = SSM_HEAD_DIM
    gw = HEADS_PER_GROUP * P

    def spread(x, width, pieces):
        k = lax.broadcasted_iota(I32, (LANES, SSM_HEADS * width), 0)
        c = lax.broadcasted_iota(I32, (LANES, SSM_HEADS * width), 1)
        sel = jnp.where(k == lane0 + c // width, 1.0, 0.0).astype(BF16)
        out, rem = None, x
        for _ in range(pieces):
            piece = rem.astype(BF16)
            rem = rem - piece.astype(F32)
            d = jnp.dot(piece, sel, preferred_element_type=F32)
            out = d if out is None else out + d
        return out

    scales = spread(jnp.concatenate([dt, dstate, yscale, jnp.broadcast_to(jnp.exp(tot), (HALO_ROWS, LANES))], axis=0),
                    P, 2)
    dt_e, ds_e, ys_e = scales[:CHUNK], scales[CHUNK:2 * CHUNK], scales[2 * CHUNK:3 * CHUNK]
    cdec_e = scales[3 * CHUNK:3 * CHUNK + 1]
    col_e = spread(pcs, CHUNK, 3)
    xdt = xc * dt_e
    xdt_b = xdt.astype(BF16)
    xs_b = (xdt * ds_e).astype(BF16)
    low = lax.broadcasted_iota(I32, (CHUNK, LANES), 1) < P

    y_groups = []
    for g in range(SSM_GROUPS):
        bg = bcv[:, g * SSM_STATE:(g + 1) * SSM_STATE]
        cg = bcv[:, D_BC + g * SSM_STATE:D_BC + (g + 1) * SSM_STATE].astype(BF16)
        cbm = lax.dot_general(cg, bg.astype(BF16), (((1,), (1,)), ((), ())), preferred_element_type=F32)
        hg = h_ref[bb, g]
        yoff = jnp.dot(cg, hg.astype(BF16), preferred_element_type=F32)
        yd = []
        for jp in range(HEADS_PER_GROUP // 2):
            tile = (g * HEADS_PER_GROUP) // 2 + jp
            xpair = xdt_b[:, tile * LANES:(tile + 1) * LANES]
            halves = []
            for par in range(2):
                j = 2 * tile + par
                ln = lane0 + j
                col = col_e[:, j * CHUNK:(j + 1) * CHUNK]
                row = pcs_t[ln:ln + 1, :]
                seg = (row - col) if reverse else (col - row)
                lm = jnp.exp(jnp.where(mask, seg, -jnp.inf))
                halves.append(jnp.dot((cbm * lm).astype(BF16), xpair, preferred_element_type=F32))
            yd.append(jnp.where(low, halves[0], halves[1]))
        sl = slice(g * gw, (g + 1) * gw)
        y_groups.append(jnp.concatenate(yd, axis=1) + yoff * ys_e[:, sl])
        snew = jnp.dot(bg.T.astype(BF16), xs_b[:, sl], preferred_element_type=F32)
        h_ref[bb, g] = hg * cdec_e[:, sl] + snew
    y = jnp.concatenate(y_groups, axis=1)

    if not reverse:
        o_ref[bb] = y
    else:
        ytot = yf_ref[bb] + y + dskip_ref[...] * xc
        yz = ytot * _silu(z_ref[bb])
        halves = []
        for g in range(SSM_GROUPS):
            seg = yz[:, g * gw:(g + 1) * gw]
            halves.append(seg * lax.rsqrt(jnp.mean(seg * seg, axis=-1, keepdims=True) + EPS))
        o_ref[bb] = (jnp.concatenate(halves, axis=1) * nw_ref[...]).astype(BF16)


def _ssd_pass(proj3, conv_w3, conv_b, dtb, alog, reverse, yf=None, dskip=None, norm_w=None):
    B, S, _ = proj3.shape
    nc = S // CHUNK
    hp = CHUNK // HALO_ROWS
    nh = S // HALO_ROWS
    ch = (lambda s: nc - 1 - s) if reverse else (lambda s: s)
    nbt = SSD_BATCH_BLOCK if B % SSD_BATCH_BLOCK == 0 else 1
    main = lambda w, col: pl.BlockSpec((nbt, CHUNK, w), lambda b, s: (b, ch(s), col // w))
    halo_p = lambda w, col: pl.BlockSpec((nbt, HALO_ROWS, w),
                                         lambda b, s: (b, jnp.maximum(ch(s) * hp - 1, 0), col // w))
    halo_n = lambda w, col: pl.BlockSpec((nbt, HALO_ROWS, w),
                                         lambda b, s: (b, jnp.minimum((ch(s) + 1) * hp, nh - 1), col // w))
    full = lambda a: pl.BlockSpec(a.shape, lambda b, s: (0,) * a.ndim)
    d_bc2 = 2 * D_BC
    in_specs = [main(D_SSM, COL_X), halo_p(D_SSM, COL_X), halo_n(D_SSM, COL_X),
                main(d_bc2, COL_B), halo_p(d_bc2, COL_B), halo_n(d_bc2, COL_B),
                main(LANES, COL_DT), full(conv_w3), full(conv_b), full(dtb), full(alog)]
    args = [proj3, proj3, proj3, proj3, proj3, proj3, proj3, conv_w3, conv_b, dtb, alog]
    if reverse:
        in_specs = [main(D_SSM, COL_Z), pl.BlockSpec((nbt, CHUNK, D_SSM), lambda b, s: (b, ch(s), 0))] + in_specs
        in_specs += [full(dskip), full(norm_w)]
        args = [proj3, yf] + args + [dskip, norm_w]
    return pl.pallas_call(
        functools.partial(_ssd_kernel, reverse=reverse, nc=nc),
        grid=(B // nbt, nc),
        in_specs=in_specs,
        out_specs=pl.BlockSpec((nbt, CHUNK, D_SSM), lambda b, s: (b, ch(s), 0)),
        out_shape=jax.ShapeDtypeStruct((B, S, D_SSM), BF16 if reverse else F32),
        scratch_shapes=[pltpu.VMEM((nbt, SSM_GROUPS, SSM_STATE, HEADS_PER_GROUP * SSM_HEAD_DIM), F32)],
        compiler_params=_cparams(2),
        name="ssd_bwd" if reverse else "ssd_fwd",
    )(*args)


def _outproj_kernel(x_ref, a_ref, s_ref, w_ref, n2_ref, rw2_ref, x1_ref, h_ref, aff_ref):
    tm = x_ref.shape[0]
    x1 = (x_ref[...]
          + jnp.dot(a_ref[...], w_ref[:D_ATTN, :], preferred_element_type=F32)
          + jnp.dot(s_ref[...], w_ref[D_ATTN:, :], preferred_element_type=F32))
    x1_ref[...] = x1
    hn = _rms(x1, n2_ref[...])
    hn_hi = hn.astype(BF16)
    bits = pltpu.bitcast(hn_hi.astype(F32), I32)
    packed = lax.shift_right_logical(bits[:, :D_MODEL // 2], 16) | bits[:, D_MODEL // 2:]
    for s in range(PACKED_ROWS):
        h_ref[pl.ds(s, tm, stride=PACKED_ROWS), :] = packed[:, s * LANES:(s + 1) * LANES]
    hn_lo = (hn - hn_hi.astype(F32)).astype(BF16)
    l_hi = jnp.dot(hn_hi, rw2_ref[...], preferred_element_type=F32)
    l_lo = jnp.dot(hn_lo, rw2_ref[:, :N_EXPERTS], preferred_element_type=F32)
    logits = l_hi[:, :N_EXPERTS] + l_hi[:, N_EXPERTS:] + l_lo
    e = jnp.exp(logits - jnp.max(logits, axis=-1, keepdims=True))
    aff_ref[...] = e / jnp.sum(e, axis=-1, keepdims=True)


def _outproj(x2, attn2, ssm2, w_out, norm2_w, router_w, tm):
    T = x2.shape[0]
    row = lambda w: pl.BlockSpec((tm, w), lambda i: (i, 0))
    full = lambda a: pl.BlockSpec(a.shape, lambda i: (0,) * a.ndim)
    return pl.pallas_call(
        _outproj_kernel,
        grid=(T // tm,),
        in_specs=[row(D_MODEL), row(D_ATTN), row(D_SSM), full(w_out), full(norm2_w), full(router_w)],
        out_specs=[row(D_MODEL), pl.BlockSpec((tm * PACKED_ROWS, LANES), lambda i: (i, 0)), row(N_EXPERTS)],
        out_shape=[jax.ShapeDtypeStruct((T, D_MODEL), F32),
                   jax.ShapeDtypeStruct((T * PACKED_ROWS, LANES), I32),
                   jax.ShapeDtypeStruct((T, N_EXPERTS), F32)],
        compiler_params=_cparams(1, VMEM_LIMIT),
        name="outproj_router",
    )(x2, attn2, ssm2, w_out, norm2_w, router_w)


def _count(m):
    c = jnp.sum(jnp.where(m, 1.0, 0.0), axis=0, keepdims=True)
    return jnp.sum(c, axis=1, keepdims=True)


def _tri(n, m, fn):
    return jnp.where(fn(lax.broadcasted_iota(I32, (n, m), 0), lax.broadcasted_iota(I32, (n, m), 1)),
                     1.0, 0.0).astype(BF16)


def _dot_u16(lhs01, x):
    hi = jnp.floor(x * (1.0 / 256.0))
    lo = x - hi * 256.0
    return (jnp.dot(lhs01, hi.astype(BF16), preferred_element_type=F32) * 256.0
            + jnp.dot(lhs01, lo.astype(BF16), preferred_element_type=F32))


def _cumsum_rowmajor(x):
    R = x.shape[0]
    within = jnp.dot(x.astype(BF16), _tri(LANES, LANES, lambda k, l: k <= l), preferred_element_type=F32)
    rowtot = jnp.broadcast_to(within[:, LANES - 1:LANES], (R, LANES))
    before = _dot_u16(_tri(R, R, lambda i, k: k < i), rowtot)
    return within + before


def _route_kernel(a_ref, idx_ref, dst_ref, g_ref, cs_ref, ce_ref, sel_s, cnt_s, cs_s, rank_s, *, cap):
    s = pl.program_id(0)
    R = a_ref.shape[0]

    @pl.when(s < N_EXPERTS)
    def _select():
        bits = pltpu.bitcast(a_ref[...], I32)
        capf = jnp.float32(cap)

        def body(k, tau):
            cand = tau | lax.shift_left(jnp.int32(1), 30 - k)
            return jnp.where(_count(bits >= cand) >= capf, cand, tau)

        tau = lax.fori_loop(0, 31, body, jnp.zeros((1, 1), I32))
        gt = bits > tau
        eq = bits == tau
        need = capf - _count(gt)
        ties = _cumsum_rowmajor(jnp.where(eq, 1.0, 0.0))
        sel = jnp.where(gt | (eq & (ties <= need)), 1.0, 0.0)
        sel_s[s] = sel

        @pl.when(s == 0)
        def _():
            cnt_s[...] = sel

        @pl.when(s > 0)
        def _():
            cnt_s[...] += sel

    @pl.when(s == N_EXPERTS)
    def _prefix():
        cnt = cnt_s[...]
        ce = _cumsum_rowmajor(cnt)
        cs_s[...] = ce - cnt
        rank_s[...] = jnp.zeros_like(rank_s)
        cs_ref[...] = ce - cnt
        ce_ref[...] = ce

    @pl.when(s >= N_EXPERTS)
    def _invert():
        e = s - N_EXPERTS
        sel = sel_s[e]
        rank = rank_s[...]
        q = cs_s[...] + rank
        rank_s[...] = rank + sel
        a = a_ref[...]
        within = jnp.dot(sel.astype(BF16), _tri(LANES, LANES, lambda k, l: k <= l), preferred_element_type=F32)
        n_b = jnp.broadcast_to(within[:, LANES - 1:LANES], (R, LANES))
        rowend = jnp.dot(_tri(R, R, lambda i, k: k <= i), n_b.astype(BF16), preferred_element_type=F32)
        slot = lax.broadcasted_iota(I32, (R, cap), 1).astype(F32)
        done = jnp.where(rowend[:, 0:1] <= slot, 1.0, 0.0)
        ones = jnp.ones((8, R), BF16)
        row_p = jnp.dot(ones, done.astype(BF16), preferred_element_type=F32)[0:1]
        start_p = jnp.dot(ones, (done * n_b[:, 0:1]).astype(BF16), preferred_element_type=F32)[0:1]
        onehot = jnp.where(lax.broadcasted_iota(I32, (R, cap), 0).astype(F32) == row_p, 1.0, 0.0).astype(BF16)
        w_t = jnp.dot(within.T.astype(BF16), onehot, preferred_element_type=F32)
        k_in_row = lax.broadcasted_iota(I32, (1, cap), 1).astype(F32) - start_p
        lane_p = jnp.sum(jnp.where(w_t <= k_in_row, 1.0, 0.0), axis=0, keepdims=True)
        pick = lax.broadcasted_iota(I32, (LANES, cap), 0).astype(F32) == lane_p

        def take(x):
            v = jnp.dot(x.T.astype(BF16), onehot, preferred_element_type=F32)
            return jnp.sum(jnp.where(pick, v, 0.0), axis=0, keepdims=True)

        q_hi = jnp.floor(q * (1.0 / 256.0))
        a_hi = a.astype(BF16).astype(F32)
        a_mid = (a - a_hi).astype(BF16).astype(F32)
        a_lo = a - a_hi - a_mid
        idx_ref[...] = (row_p * LANES + lane_p).astype(I32)
        dst_ref[...] = (take(q_hi) * 256.0 + take(q - q_hi * 256.0)).astype(I32)
        g_ref[...] = take(a_hi) + take(a_mid) + take(a_lo)


def _route(aff_t3, cap):
    E, R, _ = aff_t3.shape
    assert R <= 256 and E == N_EXPERTS
    slot_spec = pl.BlockSpec((None, 1, cap), lambda s: (jnp.maximum(s - N_EXPERTS, 0), 0, 0))
    tok_spec = pl.BlockSpec((R, LANES), lambda s: (0, 0))
    return pl.pallas_call(
        functools.partial(_route_kernel, cap=cap),
        grid=(2 * E,),
        in_specs=[pl.BlockSpec((None, R, LANES), lambda s: (s % N_EXPERTS, 0, 0))],
        out_specs=[slot_spec, slot_spec, slot_spec, tok_spec, tok_spec],
        out_shape=[jax.ShapeDtypeStruct((E, 1, cap), I32), jax.ShapeDtypeStruct((E, 1, cap), I32),
                   jax.ShapeDtypeStruct((E, 1, cap), F32),
                   jax.ShapeDtypeStruct((R, LANES), F32), jax.ShapeDtypeStruct((R, LANES), F32)],
        scratch_shapes=[pltpu.VMEM((E, R, LANES), F32), pltpu.VMEM((R, LANES), F32),
                        pltpu.VMEM((R, LANES), F32), pltpu.VMEM((R, LANES), F32)],
        compiler_params=_cparams(1, VMEM_LIMIT),
        name="route",
    )(aff_t3)


def _ffn_kernel(idx_ref, dst_ref, h_hbm, gt_ref, wg_ref, wu_ref, wd_ref, c_hbm,
                xs_stage, xsb, hid, o_stage, gsem, ssem, *, tm, cap, nf):
    e, m, f = pl.program_id(0), pl.program_id(1), pl.program_id(2)
    base = e * cap + m * tm
    part = tm // nf
    tf = wg_ref.shape[1]
    tn = wd_ref.shape[1]
    is_first = (e == 0) & (m == 0)
    is_last = (e == pl.num_programs(0) - 1) & (m == pl.num_programs(1) - 1)

    def packed_rows(t):
        return pl.ds(pl.multiple_of(t * PACKED_ROWS, PACKED_ROWS), PACKED_ROWS)

    def gather_rows(tile_base, r0, n):
        def issue(r, carry):
            pltpu.make_async_copy(h_hbm.at[packed_rows(idx_ref[tile_base + r0 + r]), :],
                                  xs_stage.at[packed_rows(r0 + r), :], gsem).start()
            return carry

        lax.fori_loop(0, n, issue, 0, unroll=8)

    def scatter_rows(tile_base, r0, n):
        def issue(r, carry):
            pltpu.make_async_copy(o_stage.at[packed_rows(r0 + r), :],
                                  c_hbm.at[packed_rows(dst_ref[tile_base + r0 + r]), :], ssem).start()
            return carry

        lax.fori_loop(0, n, issue, 0, unroll=8)

    def scatter_done():
        pltpu.make_async_copy(o_stage, c_hbm.at[pl.ds(0, tm * PACKED_ROWS), :], ssem).wait()

    def gather_done():
        pltpu.make_async_copy(h_hbm.at[pl.ds(0, tm * PACKED_ROWS), :], xs_stage, gsem).wait()

    @pl.when(f == 0)
    def _stage_in():
        @pl.when(is_first)
        def _():
            gather_rows(base, 0, tm)
            o_stage[...] = jnp.zeros_like(o_stage)

        gather_done()
        half = D_MODEL // 2
        for s in range(PACKED_ROWS):
            u = xs_stage[pl.ds(s, tm, stride=PACKED_ROWS), :]
            lo = pltpu.bitcast(lax.shift_left(u, 16), F32)
            hi = pltpu.bitcast(u & jnp.int32(-65536), F32)
            xsb[:, s * LANES:(s + 1) * LANES] = lo.astype(BF16)
            xsb[:, half + s * LANES:half + (s + 1) * LANES] = hi.astype(BF16)

    nxt = jnp.where(is_last, base, base + tm)
    prv = jnp.maximum(base - tm, 0)

    for k in range(nf):
        @pl.when(f == k)
        def _gate_up(k=k):
            xs = xsb[...]
            g = jnp.dot(xs, wg_ref[...].astype(BF16), preferred_element_type=F32)
            u = jnp.dot(xs, wu_ref[...].astype(BF16), preferred_element_type=F32)
            hid[:, k * tf:(k + 1) * tf] = (_silu(g) * u).astype(BF16)
            for r in range(k * part, (k + 1) * part):
                pltpu.make_async_copy(h_hbm.at[packed_rows(idx_ref[nxt + r]), :],
                                      xs_stage.at[packed_rows(r), :], gsem).start()
                pltpu.make_async_copy(o_stage.at[packed_rows(r), :],
                                      c_hbm.at[packed_rows(dst_ref[prv + r]), :], ssem).start()

    for n in range(nf):
        @pl.when(f == nf + n)
        def _down(n=n):
            if n == 0:
                scatter_done()

            out = jnp.dot(hid[...], wd_ref[...].astype(BF16), preferred_element_type=F32)
            for j in range(tm // LANES):
                rows = out[j * LANES:(j + 1) * LANES, :] * gt_ref[:, j:j + 1]
                bits = pltpu.bitcast(rows.astype(BF16).astype(F32), I32)
                packed = lax.shift_right_logical(bits[:, :tn // 2], 16) | bits[:, tn // 2:]
                for c in range(tn // 2 // LANES):
                    o_stage[pl.ds(j * LANES * PACKED_ROWS + n * (tn // 2 // LANES) + c, LANES,
                                  stride=PACKED_ROWS), :] = packed[:, c * LANES:(c + 1) * LANES]

            if n == nf - 1:
                @pl.when(is_last)
                def _():
                    scatter_rows(base, 0, tm)
                    scatter_done()
                    gather_done()


def _ffn(idx, dst, h_rows, g_t, w_gate, w_up, w_down, tm, tf):
    E, n_m = g_t.shape[0], g_t.shape[1]
    cap = n_m * tm
    n_contrib = E * cap
    nf = D_FF // tf
    tn = D_MODEL // nf
    assert tn == CONTRIB_TN
    up = lambda f: jnp.minimum(f, nf - 1)
    down = lambda f: jnp.maximum(f - nf, 0)
    grid_spec = pltpu.PrefetchScalarGridSpec(
        num_scalar_prefetch=2,
        grid=(E, n_m, 2 * nf),
        in_specs=[pl.BlockSpec(memory_space=pl.ANY),
                  pl.BlockSpec((None, None, LANES, tm // LANES), lambda e, m, f, i, d: (e, m, 0, 0)),
                  pl.BlockSpec((None, D_MODEL, tf), lambda e, m, f, i, d: (e, 0, up(f))),
                  pl.BlockSpec((None, D_MODEL, tf), lambda e, m, f, i, d: (e, 0, up(f))),
                  pl.BlockSpec((None, D_FF, tn), lambda e, m, f, i, d: (e, 0, down(f)))],
        out_specs=pl.BlockSpec(memory_space=pl.ANY),
        scratch_shapes=[pltpu.VMEM((tm * PACKED_ROWS, LANES), I32), pltpu.VMEM((tm, D_MODEL), BF16),
                        pltpu.VMEM((tm, D_FF), BF16), pltpu.VMEM((tm * PACKED_ROWS, LANES), I32),
                        pltpu.SemaphoreType.DMA, pltpu.SemaphoreType.DMA],
    )
    return pl.pallas_call(
        functools.partial(_ffn_kernel, tm=tm, cap=cap, nf=nf),
        grid_spec=grid_spec,
        out_shape=jax.ShapeDtypeStruct((n_contrib * PACKED_ROWS, LANES), I32),
        compiler_params=_cparams(3, FFN_VMEM_LIMIT),
        name="expert_ffn",
    )(idx, dst, h_rows, g_t, w_gate, w_up, w_down)


def _combine_kernel(ch_ref, tt_ref, flag_ref, x1_ref, cs_ref, ce_ref, c_ref, nw_ref, out_ref, acc_ref, *, cg, rows):
    k = pl.program_id(0)
    flag = flag_ref[k]

    @pl.when((flag & 2) > 0)
    def _():
        acc_ref[...] = x1_ref[...]

    @pl.when((flag & 1) > 0)
    def _():
        cio = (lax.broadcasted_iota(I32, (cg, LANES), 0) + ch_ref[k] * cg).astype(F32)
        parts = [jnp.where((cs_ref[r:r + 1, :] <= cio) & (cio < ce_ref[r:r + 1, :]), 1.0, 0.0) for r in range(rows)]
        a = jnp.concatenate(parts, axis=1).T.astype(BF16)
        blocks = CONTRIB_TN // 2 // LANES
        cols = []
        for n in range(D_MODEL // CONTRIB_TN):
            los, his = [], []
            for c in range(blocks):
                u = c_ref[pl.ds(n * blocks + c, cg, stride=PACKED_ROWS), :]
                los.append(pltpu.bitcast(lax.shift_left(u, 16), F32).astype(BF16))
                his.append(pltpu.bitcast(u & jnp.int32(-65536), F32).astype(BF16))
            cols += los + his
        chunk = jnp.concatenate(cols, axis=1)
        acc_ref[...] += jnp.dot(a, chunk, preferred_element_type=F32)

    @pl.when((flag & 4) > 0)
    def _():
        out_ref[...] = _rms(acc_ref[...], nw_ref[...])


def _combine(ch, tt, flag, x1, cs3, ce3, contrib, final_w, tb, cg):
    T = x1.shape[0]
    rows = tb // LANES
    n_pairs = ch.shape[0]
    grid_spec = pltpu.PrefetchScalarGridSpec(
        num_scalar_prefetch=3,
        grid=(n_pairs,),
        in_specs=[pl.BlockSpec((tb, D_MODEL), lambda k, ch, tt, fl: (tt[k], 0)),
                  pl.BlockSpec((None, rows, LANES), lambda k, ch, tt, fl: (tt[k], 0, 0)),
                  pl.BlockSpec((None, rows, LANES), lambda k, ch, tt, fl: (tt[k], 0, 0)),
                  pl.BlockSpec((cg * PACKED_ROWS, LANES), lambda k, ch, tt, fl: (ch[k], 0)),
                  pl.BlockSpec((1, D_MODEL), lambda k, ch, tt, fl: (0, 0))],
        out_specs=pl.BlockSpec((tb, D_MODEL), lambda k, ch, tt, fl: (tt[k], 0)),
        scratch_shapes=[pltpu.VMEM((tb, D_MODEL), F32)],
    )
    return pl.pallas_call(
        functools.partial(_combine_kernel, cg=cg, rows=rows),
        grid_spec=grid_spec,
        out_shape=jax.ShapeDtypeStruct((T, D_MODEL), F32),
        compiler_params=_cparams(1, VMEM_LIMIT),
        name="combine_final",
    )(ch, tt, flag, x1, cs3, ce3, contrib, final_w)


def _combine_schedule(cs, ce, tb, cg):
    T = cs.size
    ntt = T // tb
    total = CAPACITY_FACTOR * T
    nch = total // cg
    lo = cs.reshape(ntt, tb)[:, 0].astype(I32)
    hi = ce.reshape(ntt, tb)[:, -1].astype(I32)
    first = jnp.minimum(lo, total - 1) // cg
    last = jnp.where(hi > lo, (hi - 1) // cg, first)
    n = last - first + 1
    ends = jnp.cumsum(n)
    offs = ends - n
    n_pairs = ntt + nch
    ks = jnp.arange(n_pairs, dtype=I32)
    tile = jnp.minimum(jnp.searchsorted(ends, ks, side="right").astype(I32), ntt - 1)
    valid = ks < ends[-1]
    chunk = jnp.where(valid, first[tile] + ks - offs[tile], last[-1])
    flag = jnp.where(valid, 1 + 2 * (ks == offs[tile]) + 4 * (ks == ends[tile] - 1), 0).astype(I32)
    return chunk.astype(I32), tile, flag


def _t5_buckets(rel):
    nb = NUM_BUCKETS // 2
    ret = (rel > 0).astype(np.int32) * nb
    n = np.abs(rel)
    max_exact = nb // 2
    large = max_exact + (np.log(np.maximum(n, 1) / max_exact) / np.log(MAX_DISTANCE / max_exact)
                         * (nb - max_exact)).astype(np.int32)
    large = np.minimum(large, nb - 1)
    return ret + np.where(n < max_exact, n, large)


def _tile(n, pref):
    t = min(n, pref)
    assert n % t == 0
    return t


def _trunk(x, w):
    B, S, _ = x.shape
    T = B * S
    cap = CAPACITY_FACTOR * T // N_EXPERTS
    x2 = x.reshape(T, D_MODEL)
    tm = _tile(T, 512)

    qkv, main = _inproj(x2, w["norm1_w"], w["w_qkv"], w["w_main"], tm)
    main3 = main.reshape(B, S, MAIN_COLS)
    attn = _attention(qkv.reshape(B, S, QKV_COLS), w["bias_tab"], w["attn_sink"], w["attn_norm_w"])
    yf = _ssd_pass(main3, w["conv_w"], w["conv_b"], w["dtb"], w["alog"], reverse=False)
    ssm = _ssd_pass(main3, w["conv_w"], w["conv_b"], w["dtb"], w["alog"], reverse=True,
                    yf=yf, dskip=w["dskip"], norm_w=w["ssm_norm_w"])
    x1, h_rows, aff = _outproj(x2, attn.reshape(T, D_ATTN), ssm.reshape(T, D_SSM), w["w_out"], w["norm2_w"],
                          w["router_w2"], tm)

    tb = _tile(T, 512)
    cg = 512
    idx, dst, g, cs, ce = _route(aff.T.reshape(N_EXPERTS, T // LANES, LANES), cap)
    tmf = _tile(cap, 1024)
    g_t = jnp.swapaxes(g.reshape(N_EXPERTS, cap // tmf, tmf // LANES, LANES), 2, 3)
    contrib = _ffn(idx.reshape(-1), dst.reshape(-1), h_rows, g_t, w["w_gate"], w["w_up"], w["w_down"], tmf, 512)
    ch, tt, flag = _combine_schedule(cs, ce, tb, cg)
    tile3 = lambda a: a.reshape(T // tb, tb // LANES, LANES)
    y = _combine(ch, tt, flag, x1, tile3(cs), tile3(ce), contrib, w["final_norm_w"], tb, cg)
    return y.reshape(B, S, D_MODEL)


def _prep_weights(rel_bias, norm1_w, w_in, conv_w, conv_b, dt_bias_fwd, dt_bias_bwd, a_log_fwd, a_log_bwd,
                  d_skip, ssm_norm_w, attn_sink, attn_norm_w, w_out, norm2_w, router_w, w_gate, w_up, w_down,
                  final_norm_w):
    o1 = D_ATTN; o2 = o1 + D_KV; o3 = o2 + D_KV; o4 = o3 + D_SSM; o5 = o4 + D_SSM + 2 * D_BC
    wi = w_in[0]
    scale = 1.0 / math.sqrt(HEAD_DIM)
    perm = np.concatenate([np.arange(h * HEAD_DIM, (h + 1) * HEAD_DIM) for h in ATTN_HEAD_ORDER])
    w_qkv = jnp.concatenate([(wi[:, :o1] * scale)[:, perm], wi[:, o1:o3]], axis=1).astype(BF16)
    w_out_p = jnp.concatenate([w_out[0][:D_ATTN][perm], w_out[0][D_ATTN:]], axis=0).astype(BF16)
    w_main = jnp.concatenate([wi[:, o3:], jnp.zeros((D_MODEL, LANES - 2 * SSM_HEADS), F32)], axis=1).astype(BF16)
    rel = np.arange(3 * BLOCK)[None, :] - BLOCK - np.arange(BLOCK)[:, None]
    onehot = (jnp.asarray(_t5_buckets(rel), I32)[..., None] == jnp.arange(NUM_BUCKETS, dtype=I32)).astype(F32)
    bias_tab = jnp.einsum("qsn,nh->hqs", onehot, rel_bias.astype(F32), precision=lax.Precision.HIGHEST)
    bias_tab = jnp.where(jnp.asarray(np.abs(rel) <= WINDOW)[None], bias_tab, NEG_INF)
    rw = router_w[0].astype(F32)
    rw_hi = rw.astype(BF16)
    router_w2 = jnp.concatenate([rw_hi, (rw - rw_hi.astype(F32)).astype(BF16)], axis=1)
    pad = jnp.zeros((LANES - 2 * SSM_HEADS,), F32)
    row = lambda a: a.reshape(1, -1).astype(F32)
    return dict(
        norm1_w=row(norm1_w[0]), w_qkv=w_qkv, w_main=w_main, bias_tab=bias_tab, attn_sink=attn_sink[0].astype(F32),
        attn_norm_w=row(attn_norm_w[0][perm]), conv_w=conv_w[0].T.astype(F32), conv_b=row(conv_b[0]),
        dtb=row(jnp.concatenate([dt_bias_fwd[0], dt_bias_bwd[0], pad])),
        alog=row(jnp.concatenate([a_log_fwd[0], a_log_bwd[0], pad])),
        dskip=row(jnp.repeat(d_skip[0], SSM_HEAD_DIM)), ssm_norm_w=row(ssm_norm_w[0]),
        w_out=w_out_p, norm2_w=row(norm2_w[0]), router_w2=router_w2,
        w_gate=w_gate[0], w_up=w_up[0], w_down=w_down[0],
        final_norm_w=row(final_norm_w))


def kernel(x_prompt, x_sample, rel_bias, norm1_w, w_in, conv_w, conv_b, dt_bias_fwd, dt_bias_bwd, a_log_fwd, a_log_bwd, d_skip, ssm_norm_w, attn_sink, attn_norm_w, w_out, norm2_w, router_w, w_gate, w_up, w_down, final_norm_w):
    assert norm1_w.shape[0] == 1
    w = _prep_weights(rel_bias, norm1_w, w_in, conv_w, conv_b, dt_bias_fwd, dt_bias_bwd, a_log_fwd, a_log_bwd,
                      d_skip, ssm_norm_w, attn_sink, attn_norm_w, w_out, norm2_w, router_w, w_gate, w_up,
                      w_down, final_norm_w)
    return (_trunk(x_prompt, w), _trunk(x_sample, w))
```

```python
import functools
import math

import numpy as np
import jax
import jax.numpy as jnp
from jax import lax
from jax.experimental import pallas as pl
from jax.experimental.pallas import tpu as pltpu

F32 = jnp.float32
BF16 = jnp.bfloat16
I32 = jnp.int32

D_MODEL = 2048
HEAD_DIM = 64
N_HEADS = 16
N_KV_HEADS = 4
D_ATTN = 1024
D_KV = 256
WINDOW = 128
BLOCK = 128
NUM_BUCKETS = 32
MAX_DISTANCE = 128
SSM_HEAD_DIM = 64
SSM_HEADS = 16
D_SSM = 1024
SSM_STATE = 128
SSM_GROUPS = 2
HEADS_PER_GROUP = SSM_HEADS // SSM_GROUPS
D_BC = 256
CHUNK = 128
N_EXPERTS = 16
CAPACITY_FACTOR = 2
D_FF = 2048
EPS = 1e-6
NEG_INF = -1e30

LANES = 128
HALO_ROWS = 8
TOKEN_ROWS = D_MODEL // LANES
PACKED_ROWS = TOKEN_ROWS // 2
CONTRIB_TN = 512

QKV_COLS = D_ATTN + 2 * D_KV
COL_QK, COL_QV = D_ATTN, D_ATTN + D_KV
COL_Z, COL_X, COL_B, COL_DT = 0, 1024, 2048, 2560
MAIN_COLS = COL_DT + LANES

ATTN_HEAD_ORDER = tuple(8 * t + 4 * par + i for t in range(2) for i in range(4) for par in range(2))

SSD_BATCH_BLOCK = 4
ATTN_BATCH_BLOCK = 4

VMEM_LIMIT = 56 * 1024 * 1024
FFN_VMEM_LIMIT = 60 * 1024 * 1024


def _cparams(n_axes, vmem=None):
    return pltpu.CompilerParams(dimension_semantics=("arbitrary",) * n_axes,
                                vmem_limit_bytes=vmem)


def _rms(x, w):
    return x * lax.rsqrt(jnp.mean(x * x, axis=-1, keepdims=True) + EPS) * w


def _silu(x):
    return x * jax.nn.sigmoid(x)


def _inproj_kernel(x_ref, nw_ref, wq_ref, wr_ref, oq_ref, om_ref):
    hn = _rms(x_ref[...], nw_ref[...]).astype(BF16)
    oq_ref[...] = jnp.dot(hn, wq_ref[...], preferred_element_type=F32).astype(BF16)
    om_ref[...] = jnp.dot(hn, wr_ref[...], preferred_element_type=F32)


def _inproj(x2, norm_w, w_qkv, w_main, tm):
    T = x2.shape[0]
    resident = lambda a: pl.BlockSpec(a.shape, lambda i: (0, 0), pipeline_mode=pl.Buffered(1))
    return pl.pallas_call(
        _inproj_kernel,
        grid=(T // tm,),
        in_specs=[pl.BlockSpec((tm, D_MODEL), lambda i: (i, 0)), resident(norm_w), resident(w_qkv),
                  resident(w_main)],
        out_specs=[pl.BlockSpec((tm, QKV_COLS), lambda i: (i, 0)),
                   pl.BlockSpec((tm, MAIN_COLS), lambda i: (i, 0))],
        out_shape=[jax.ShapeDtypeStruct((T, QKV_COLS), BF16), jax.ShapeDtypeStruct((T, MAIN_COLS), F32)],
        compiler_params=_cparams(1, VMEM_LIMIT),
        name="inproj",
    )(x2, norm_w, w_qkv, w_main)


def _attn_kernel(q_ref, kp_ref, ko_ref, kn_ref, vp_ref, vo_ref, vn_ref, bias_ref, sink_ref, nw_ref,
                 o_ref, *, nb):
    i = pl.program_id(1)
    lane = lax.broadcasted_iota(I32, (3 * BLOCK, LANES), 1)
    m_lo = jnp.where(lane < HEAD_DIM, 1.0, 0.0).astype(BF16)
    m_up = jnp.where(lane < HEAD_DIM, 0.0, 1.0).astype(BF16)
    low_q = lax.broadcasted_iota(I32, (BLOCK, LANES), 1) < HEAD_DIM
    top = lax.broadcasted_iota(I32, (2 * BLOCK, 1), 0) < BLOCK
    tiles_per_pair = D_ATTN // LANES // (D_KV // LANES)

    def body(bb, edge):
        tiles = []
        for t in range(D_KV // LANES):
            sl = slice(t * LANES, (t + 1) * LANES)
            kt = jnp.concatenate([kp_ref[bb, :, sl], ko_ref[bb, :, sl], kn_ref[bb, :, sl]], axis=0)
            vt = jnp.concatenate([vp_ref[bb, :, sl], vo_ref[bb, :, sl], vn_ref[bb, :, sl]], axis=0)
            rhs = [(kt * m_lo, vt * m_lo + m_up), (kt * m_up, vt * m_up + m_lo)]
            for j in range(0, tiles_per_pair, 2):
                qa, qb = t * tiles_per_pair + j, t * tiles_per_pair + j + 1
                q2 = jnp.concatenate([q_ref[bb, :, qa * LANES:(qa + 1) * LANES],
                                      q_ref[bb, :, qb * LANES:(qb + 1) * LANES]], axis=0)
                res = []
                for par in range(2):
                    kz, vz = rhs[par]
                    ha, hb = ATTN_HEAD_ORDER[2 * qa + par], ATTN_HEAD_ORDER[2 * qb + par]
                    s = lax.dot_general(q2, kz, (((1,), (1,)), ((), ())), preferred_element_type=F32)
                    s = s + jnp.concatenate([bias_ref[ha], bias_ref[hb]], axis=0) + edge
                    sk = jnp.where(top, sink_ref[ha], sink_ref[hb])
                    m = jnp.maximum(jnp.max(s, axis=-1, keepdims=True), sk)
                    p = jnp.exp(s - m).astype(BF16)
                    pv = jnp.dot(p, vz, preferred_element_type=F32)
                    den = (pv[:, LANES - 1:LANES] if par == 0 else pv[:, 0:1]) + jnp.exp(sk - m)
                    res.append(pv * (1.0 / den))
                tiles.append(jnp.where(low_q, res[0][:BLOCK], res[1][:BLOCK]))
                tiles.append(jnp.where(low_q, res[0][BLOCK:], res[1][BLOCK:]))
        o = jnp.concatenate(tiles, axis=-1)
        o_ref[bb] = _rms(o, nw_ref[...]).astype(BF16)

    kcol = lax.broadcasted_iota(I32, (1, 3 * BLOCK), 1)
    edge = (jnp.where((kcol < BLOCK) & (i == 0), NEG_INF, 0.0)
            + jnp.where((kcol >= 2 * BLOCK) & (i == nb - 1), NEG_INF, 0.0))
    for bb in range(q_ref.shape[0]):
        body(bb, edge)


def _attention(qkv3, bias_tab, sink, norm_w):
    B, S, _ = qkv3.shape
    nb = S // BLOCK
    kcol, vcol = COL_QK // D_KV, COL_QV // D_KV
    prev = lambda i: jnp.maximum(i - 1, 0)
    nxt = lambda i: jnp.minimum(i + 1, nb - 1)
    nbt = ATTN_BATCH_BLOCK if B % ATTN_BATCH_BLOCK == 0 else 1
    kv = lambda col, f: pl.BlockSpec((nbt, BLOCK, D_KV), lambda b, i: (b, f(i), col))
    same = lambda i: i
    return pl.pallas_call(
        functools.partial(_attn_kernel, nb=nb),
        grid=(B // nbt, nb),
        in_specs=[pl.BlockSpec((nbt, BLOCK, D_ATTN), lambda b, i: (b, i, 0)),
                  kv(kcol, prev), kv(kcol, same), kv(kcol, nxt),
                  kv(vcol, prev), kv(vcol, same), kv(vcol, nxt),
                  pl.BlockSpec((N_HEADS, BLOCK, 3 * BLOCK), lambda b, i: (0, 0, 0)),
                  pl.BlockSpec(memory_space=pltpu.SMEM),
                  pl.BlockSpec((1, D_ATTN), lambda b, i: (0, 0))],
        out_specs=pl.BlockSpec((nbt, BLOCK, D_ATTN), lambda b, i: (b, i, 0)),
        out_shape=jax.ShapeDtypeStruct((B, S, D_ATTN), BF16),
        compiler_params=_cparams(2),
        name="attention",
    )(qkv3, qkv3, qkv3, qkv3, qkv3, qkv3, qkv3, bias_tab, sink, norm_w)


def _conv_silu(u, prev8, next8, w, b, has_prev, has_next):
    prev_row = jnp.where(has_prev, prev8[HALO_ROWS - 1:HALO_ROWS, :], 0.0)
    next_row = jnp.where(has_next, next8[0:1, :], 0.0)
    rid = lax.broadcasted_iota(I32, u.shape, 0)
    um = jnp.where(rid == 0, prev_row, pltpu.roll(u, 1, 0))
    up = jnp.where(rid == CHUNK - 1, next_row, pltpu.roll(u, CHUNK - 1, 0))
    y = b + um * w[0:1, :] + u * w[1:2, :] + up * w[2:3, :]
    return _silu(y)


def _ssd_kernel(*refs, reverse, nc):
    h_ref = refs[-1]

    @pl.when(pl.program_id(1) == 0)
    def _():
        h_ref[...] = jnp.zeros_like(h_ref)

    for bb in range(h_ref.shape[0]):
        _ssd_chunk(bb, refs, reverse, nc)


def _ssd_chunk(bb, refs, reverse, nc):
    if reverse:
        (z_ref, yf_ref, x_ref, xp_ref, xn_ref, bc_ref, bcp_ref, bcn_ref, dt_ref, cw_ref, cb_ref, dtb_ref,
         alog_ref, dskip_ref, nw_ref, o_ref, h_ref) = refs
    else:
        (x_ref, xp_ref, xn_ref, bc_ref, bcp_ref, bcn_ref, dt_ref, cw_ref, cb_ref, dtb_ref,
         alog_ref, o_ref, h_ref) = refs
    step = pl.program_id(1)
    c = (nc - 1 - step) if reverse else step
    has_prev, has_next = c > 0, c < nc - 1
    cw, cb = cw_ref[...], cb_ref[...]
    xc = _conv_silu(x_ref[bb], xp_ref[bb], xn_ref[bb], cw[:, :D_SSM], cb[:, :D_SSM], has_prev, has_next)
    bcv = _conv_silu(bc_ref[bb], bcp_ref[bb], bcn_ref[bb], cw[:, D_SSM:], cb[:, D_SSM:], has_prev, has_next)

    raw = dt_ref[bb] + dtb_ref[...]
    dt = jnp.maximum(raw, 0.0) + jnp.log1p(jnp.exp(-jnp.abs(raw)))
    a = dt * (-jnp.exp(alog_ref[...]))
    li = lax.broadcasted_iota(I32, (CHUNK, CHUNK), 0)
    si = lax.broadcasted_iota(I32, (CHUNK, CHUNK), 1)
    incl = jnp.dot(jnp.where(li >= si, 1.0, 0.0).astype(F32), a, precision=lax.Precision.HIGHEST,
                   preferred_element_type=F32)
    tot = incl[CHUNK - 1:CHUNK, :]
    if reverse:
        pcs = incl - a
        dstate = jnp.exp(pcs)
        yscale = jnp.exp(tot - pcs)
        mask = si >= li
    else:
        pcs = incl
        dstate = jnp.exp(tot - pcs)
        yscale = jnp.exp(pcs)
        mask = li >= si
    pcs_t = pcs.T
    lane0 = SSM_HEADS if reverse else 0
    P = SSM_HEAD_DIM
    gw = HEADS_PER_GROUP * P

    def spread(x, width, pieces):
        k = lax.broadcasted_iota(I32, (LANES, SSM_HEADS * width), 0)
        c = lax.broadcasted_iota(I32, (LANES, SSM_HEADS * width), 1)
        sel = jnp.where(k == lane0 + c // width, 1.0, 0.0).astype(BF16)
        out, rem = None, x
        for _ in range(pieces):
            piece = rem.astype(BF16)
            rem = rem - piece.astype(F32)
            d = jnp.dot(piece, sel, preferred_element_type=F32)
            out = d if out is None else out + d
        return out

    scales = spread(jnp.concatenate([dt, dstate, yscale, jnp.broadcast_to(jnp.exp(tot), (HALO_ROWS, LANES))], axis=0),
                    P, 2)
    dt_e, ds_e, ys_e = scales[:CHUNK], scales[CHUNK:2 * CHUNK], scales[2 * CHUNK:3 * CHUNK]
    cdec_e = scales[3 * CHUNK:3 * CHUNK + 1]
    col_e = spread(pcs, CHUNK, 3)
    xdt = xc * dt_e
    xdt_b = xdt.astype(BF16)
    xs_b = (xdt * ds_e).astype(BF16)
    low = lax.broadcasted_iota(I32, (CHUNK, LANES), 1) < P

    y_groups = []
    for g in range(SSM_GROUPS):
        bg = bcv[:, g * SSM_STATE:(g + 1) * SSM_STATE]
        cg = bcv[:, D_BC + g * SSM_STATE:D_BC + (g + 1) * SSM_STATE].astype(BF16)
        cbm = lax.dot_general(cg, bg.astype(BF16), (((1,), (1,)), ((), ())), preferred_element_type=F32)
        hg = h_ref[bb, g]
        yoff = jnp.dot(cg, hg.astype(BF16), preferred_element_type=F32)
        yd = []
        for jp in range(HEADS_PER_GROUP // 2):
            tile = (g * HEADS_PER_GROUP) // 2 + jp
            xpair = xdt_b[:, tile * LANES:(tile + 1) * LANES]
            halves = []
            for par in range(2):
                j = 2 * tile + par
                ln = lane0 + j
                col = col_e[:, j * CHUNK:(j + 1) * CHUNK]
                row = pcs_t[ln:ln + 1, :]
                seg = (row - col) if reverse else (col - row)
                lm = jnp.exp(jnp.where(mask, seg, -jnp.inf))
                halves.append(jnp.dot((cbm * lm).astype(BF16), xpair, preferred_element_type=F32))
            yd.append(jnp.where(low, halves[0], halves[1]))
        sl = slice(g * gw, (g + 1) * gw)
        y_groups.append(jnp.concatenate(yd, axis=1) + yoff * ys_e[:, sl])
        snew = jnp.dot(bg.T.astype(BF16), xs_b[:, sl], preferred_element_type=F32)
        h_ref[bb, g] = hg * cdec_e[:, sl] + snew
    y = jnp.concatenate(y_groups, axis=1)

    if not reverse:
        o_ref[bb] = y
    else:
        ytot = yf_ref[bb] + y + dskip_ref[...] * xc
        yz = ytot * _silu(z_ref[bb])
        halves = []
        for g in range(SSM_GROUPS):
            seg = yz[:, g * gw:(g + 1) * gw]
            halves.append(seg * lax.rsqrt(jnp.mean(seg * seg, axis=-1, keepdims=True) + EPS))
        o_ref[bb] = (jnp.concatenate(halves, axis=1) * nw_ref[...]).astype(BF16)


def _ssd_pass(proj3, conv_w3, conv_b, dtb, alog, reverse, yf=None, dskip=None, norm_w=None):
    B, S, _ = proj3.shape
    nc = S // CHUNK
    hp = CHUNK // HALO_ROWS
    nh = S // HALO_ROWS
    ch = (lambda s: nc - 1 - s) if reverse else (lambda s: s)
    nbt = SSD_BATCH_BLOCK if B % SSD_BATCH_BLOCK == 0 else 1
    main = lambda w, col: pl.BlockSpec((nbt, CHUNK, w), lambda b, s: (b, ch(s), col // w))
    halo_p = lambda w, col: pl.BlockSpec((nbt, HALO_ROWS, w),
                                         lambda b, s: (b, jnp.maximum(ch(s) * hp - 1, 0), col // w))
    halo_n = lambda w, col: pl.BlockSpec((nbt, HALO_ROWS, w),
                                         lambda b, s: (b, jnp.minimum((ch(s) + 1) * hp, nh - 1), col // w))
    full = lambda a: pl.BlockSpec(a.shape, lambda b, s: (0,) * a.ndim)
    d_bc2 = 2 * D_BC
    in_specs = [main(D_SSM, COL_X), halo_p(D_SSM, COL_X), halo_n(D_SSM, COL_X),
                main(d_bc2, COL_B), halo_p(d_bc2, COL_B), halo_n(d_bc2, COL_B),
                main(LANES, COL_DT), full(conv_w3), full(conv_b), full(dtb), full(alog)]
    args = [proj3, proj3, proj3, proj3, proj3, proj3, proj3, conv_w3, conv_b, dtb, alog]
    if reverse:
        in_specs = [main(D_SSM, COL_Z), pl.BlockSpec((nbt, CHUNK, D_SSM), lambda b, s: (b, ch(s), 0))] + in_specs
        in_specs += [full(dskip), full(norm_w)]
        args = [proj3, yf] + args + [dskip, norm_w]
    return pl.pallas_call(
        functools.partial(_ssd_kernel, reverse=reverse, nc=nc),
        grid=(B // nbt, nc),
        in_specs=in_specs,
        out_specs=pl.BlockSpec((nbt, CHUNK, D_SSM), lambda b, s: (b, ch(s), 0)),
        out_shape=jax.ShapeDtypeStruct((B, S, D_SSM), BF16 if reverse else F32),
        scratch_shapes=[pltpu.VMEM((nbt, SSM_GROUPS, SSM_STATE, HEADS_PER_GROUP * SSM_HEAD_DIM), F32)],
        compiler_params=_cparams(2),
        name="ssd_bwd" if reverse else "ssd_fwd",
    )(*args)


def _outproj_kernel(x_ref, a_ref, s_ref, w_ref, n2_ref, rw2_ref, x1_ref, h_ref, aff_ref):
    tm = x_ref.shape[0]
    x1 = (x_ref[...]
          + jnp.dot(a_ref[...], w_ref[:D_ATTN, :], preferred_element_type=F32)
          + jnp.dot(s_ref[...], w_ref[D_ATTN:, :], preferred_element_type=F32))
    x1_ref[...] = x1
    hn = _rms(x1, n2_ref[...])
    hn_hi = hn.astype(BF16)
    bits = pltpu.bitcast(hn_hi.astype(F32), I32)
    packed = lax.shift_right_logical(bits[:, :D_MODEL // 2], 16) | bits[:, D_MODEL // 2:]
    for s in range(PACKED_ROWS):
        h_ref[pl.ds(s, tm, stride=PACKED_ROWS), :] = packed[:, s * LANES:(s + 1) * LANES]
    hn_lo = (hn - hn_hi.astype(F32)).astype(BF16)
    l_hi = jnp.dot(hn_hi, rw2_ref[...], preferred_element_type=F32)
    l_lo = jnp.dot(hn_lo, rw2_ref[:, :N_EXPERTS], preferred_element_type=F32)
    logits = l_hi[:, :N_EXPERTS] + l_hi[:, N_EXPERTS:] + l_lo
    e = jnp.exp(logits - jnp.max(logits, axis=-1, keepdims=True))
    aff_ref[...] = e / jnp.sum(e, axis=-1, keepdims=True)


def _outproj(x2, attn2, ssm2, w_out, norm2_w, router_w, tm):
    T = x2.shape[0]
    row = lambda w: pl.BlockSpec((tm, w), lambda i: (i, 0))
    full = lambda a: pl.BlockSpec(a.shape, lambda i: (0,) * a.ndim)
    return pl.pallas_call(
        _outproj_kernel,
        grid=(T // tm,),
        in_specs=[row(D_MODEL), row(D_ATTN), row(D_SSM), full(w_out), full(norm2_w), full(router_w)],
        out_specs=[row(D_MODEL), pl.BlockSpec((tm * PACKED_ROWS, LANES), lambda i: (i, 0)), row(N_EXPERTS)],
        out_shape=[jax.ShapeDtypeStruct((T, D_MODEL), F32),
                   jax.ShapeDtypeStruct((T * PACKED_ROWS, LANES), I32),
                   jax.ShapeDtypeStruct((T, N_EXPERTS), F32)],
        compiler_params=_cparams(1, VMEM_LIMIT),
        name="outproj_router",
    )(x2, attn2, ssm2, w_out, norm2_w, router_w)


def _count(m):
    c = jnp.sum(jnp.where(m, 1.0, 0.0), axis=0, keepdims=True)
    return jnp.sum(c, axis=1, keepdims=True)


def _tri(n, m, fn):
    return jnp.where(fn(lax.broadcasted_iota(I32, (n, m), 0), lax.broadcasted_iota(I32, (n, m), 1)),
                     1.0, 0.0).astype(BF16)


def _dot_u16(lhs01, x):
    hi = jnp.floor(x * (1.0 / 256.0))
    lo = x - hi * 256.0
    return (jnp.dot(lhs01, hi.astype(BF16), preferred_element_type=F32) * 256.0
            + jnp.dot(lhs01, lo.astype(BF16), preferred_element_type=F32))


def _cumsum_rowmajor(x):
    R = x.shape[0]
    within = jnp.dot(x.astype(BF16), _tri(LANES, LANES, lambda k, l: k <= l), preferred_element_type=F32)
    rowtot = jnp.broadcast_to(within[:, LANES - 1:LANES], (R, LANES))
    before = _dot_u16(_tri(R, R, lambda i, k: k < i), rowtot)
    return within + before


def _route_kernel(a_ref, idx_ref, dst_ref, g_ref, cs_ref, ce_ref, sel_s, cnt_s, cs_s, rank_s, *, cap):
    s = pl.program_id(0)
    R = a_ref.shape[0]

    @pl.when(s < N_EXPERTS)
    def _select():
        bits = pltpu.bitcast(a_ref[...], I32)
        capf = jnp.float32(cap)

        def body(k, tau):
            cand = tau | lax.shift_left(jnp.int32(1), 30 - k)
            return jnp.where(_count(bits >= cand) >= capf, cand, tau)

        tau = lax.fori_loop(0, 31, body, jnp.zeros((1, 1), I32))
        gt = bits > tau
        eq = bits == tau
        need = capf - _count(gt)
        ties = _cumsum_rowmajor(jnp.where(eq, 1.0, 0.0))
        sel = jnp.where(gt | (eq & (ties <= need)), 1.0, 0.0)
        sel_s[s] = sel

        @pl.when(s == 0)
        def _():
            cnt_s[...] = sel

        @pl.when(s > 0)
        def _():
            cnt_s[...] += sel

    @pl.when(s == N_EXPERTS)
    def _prefix():
        cnt = cnt_s[...]
        ce = _cumsum_rowmajor(cnt)
        cs_s[...] = ce - cnt
        rank_s[...] = jnp.zeros_like(rank_s)
        cs_ref[...] = ce - cnt
        ce_ref[...] = ce

    @pl.when(s >= N_EXPERTS)
    def _invert():
        e = s - N_EXPERTS
        sel = sel_s[e]
        rank = rank_s[...]
        q = cs_s[...] + rank
        rank_s[...] = rank + sel
        a = a_ref[...]
        within = jnp.dot(sel.astype(BF16), _tri(LANES, LANES, lambda k, l: k <= l), preferred_element_type=F32)
        n_b = jnp.broadcast_to(within[:, LANES - 1:LANES], (R, LANES))
        rowend = jnp.dot(_tri(R, R, lambda i, k: k <= i), n_b.astype(BF16), preferred_element_type=F32)
        slot = lax.broadcasted_iota(I32, (R, cap), 1).astype(F32)
        done = jnp.where(rowend[:, 0:1] <= slot, 1.0, 0.0)
        ones = jnp.ones((8, R), BF16)
        row_p = jnp.dot(ones, done.astype(BF16), preferred_element_type=F32)[0:1]
        start_p = jnp.dot(ones, (done * n_b[:, 0:1]).astype(BF16), preferred_element_type=F32)[0:1]
        onehot = jnp.where(lax.broadcasted_iota(I32, (R, cap), 0).astype(F32) == row_p, 1.0, 0.0).astype(BF16)
        w_t = jnp.dot(within.T.astype(BF16), onehot, preferred_element_type=F32)
        k_in_row = lax.broadcasted_iota(I32, (1, cap), 1).astype(F32) - start_p
        lane_p = jnp.sum(jnp.where(w_t <= k_in_row, 1.0, 0.0), axis=0, keepdims=True)
        pick = lax.broadcasted_iota(I32, (LANES, cap), 0).astype(F32) == lane_p

        def take(x):
            v = jnp.dot(x.T.astype(BF16), onehot, preferred_element_type=F32)
            return jnp.sum(jnp.where(pick, v, 0.0), axis=0, keepdims=True)

        q_hi = jnp.floor(q * (1.0 / 256.0))
        a_hi = a.astype(BF16).astype(F32)
        a_mid = (a - a_hi).astype(BF16).astype(F32)
        a_lo = a - a_hi - a_mid
        idx_ref[...] = (row_p * LANES + lane_p).astype(I32)
        dst_ref[...] = (take(q_hi) * 256.0 + take(q - q_hi * 256.0)).astype(I32)
        g_ref[...] = take(a_hi) + take(a_mid) + take(a_lo)


def _route(aff_t3, cap):
    E, R, _ = aff_t3.shape
    assert R <= 256 and E == N_EXPERTS
    slot_spec = pl.BlockSpec((None, 1, cap), lambda s: (jnp.maximum(s - N_EXPERTS, 0), 0, 0))
    tok_spec = pl.BlockSpec((R, LANES), lambda s: (0, 0))
    return pl.pallas_call(
        functools.partial(_route_kernel, cap=cap),
        grid=(2 * E,),
        in_specs=[pl.BlockSpec((None, R, LANES), lambda s: (s % N_EXPERTS, 0, 0))],
        out_specs=[slot_spec, slot_spec, slot_spec, tok_spec, tok_spec],
        out_shape=[jax.ShapeDtypeStruct((E, 1, cap), I32), jax.ShapeDtypeStruct((E, 1, cap), I32),
                   jax.ShapeDtypeStruct((E, 1, cap), F32),
                   jax.ShapeDtypeStruct((R, LANES), F32), jax.ShapeDtypeStruct((R, LANES), F32)],
        scratch_shapes=[pltpu.VMEM((E, R, LANES), F32), pltpu.VMEM((R, LANES), F32),
                        pltpu.VMEM((R, LANES), F32), pltpu.VMEM((R, LANES), F32)],
        compiler_params=_cparams(1, VMEM_LIMIT),
        name="route",
    )(aff_t3)


def _ffn_kernel(idx_ref, dst_ref, h_hbm, gt_ref, wg_ref, wu_ref, wd_ref, c_hbm,
                xs_stage, xsb, hid, o_stage, gsem, ssem, *, tm, cap, nf):
    e, m, f = pl.program_id(0), pl.program_id(1), pl.program_id(2)
    base = e * cap + m * tm
    part = tm // nf
    tf = wg_ref.shape[1]
    tn = wd_ref.shape[1]
    is_first = (e == 0) & (m == 0)
    is_last = (e == pl.num_programs(0) - 1) & (m == pl.num_programs(1) - 1)

    def packed_rows(t):
        return pl.ds(pl.multiple_of(t * PACKED_ROWS, PACKED_ROWS), PACKED_ROWS)

    def gather_rows(tile_base, r0, n):
        def issue(r, carry):
            pltpu.make_async_copy(h_hbm.at[packed_rows(idx_ref[tile_base + r0 + r]), :],
                                  xs_stage.at[packed_rows(r0 + r), :], gsem).start()
            return carry

        lax.fori_loop(0, n, issue, 0, unroll=8)

    def scatter_rows(tile_base, r0, n):
        def issue(r, carry):
            pltpu.make_async_copy(o_stage.at[packed_rows(r0 + r), :],
                                  c_hbm.at[packed_rows(dst_ref[tile_base + r0 + r]), :], ssem).start()
            return carry

        lax.fori_loop(0, n, issue, 0, unroll=8)

    def scatter_done():
        pltpu.make_async_copy(o_stage, c_hbm.at[pl.ds(0, tm * PACKED_ROWS), :], ssem).wait()

    def gather_done():
        pltpu.make_async_copy(h_hbm.at[pl.ds(0, tm * PACKED_ROWS), :], xs_stage, gsem).wait()

    @pl.when(f == 0)
    def _stage_in():
        @pl.when(is_first)
        def _():
            gather_rows(base, 0, tm)
            o_stage[...] = jnp.zeros_like(o_stage)

        gather_done()
        half = D_MODEL // 2
        for s in range(PACKED_ROWS):
            u = xs_stage[pl.ds(s, tm, stride=PACKED_ROWS), :]
            lo = pltpu.bitcast(lax.shift_left(u, 16), F32)
            hi = pltpu.bitcast(u & jnp.int32(-65536), F32)
            xsb[:, s * LANES:(s + 1) * LANES] = lo.astype(BF16)
            xsb[:, half + s * LANES:half + (s + 1) * LANES] = hi.astype(BF16)

    nxt = jnp.where(is_last, base, base + tm)
    prv = jnp.maximum(base - tm, 0)

    for k in range(nf):
        @pl.when(f == k)
        def _gate_up(k=k):
            xs = xsb[...]
            g = jnp.dot(xs, wg_ref[...].astype(BF16), preferred_element_type=F32)
            u = jnp.dot(xs, wu_ref[...].astype(BF16), preferred_element_type=F32)
            hid[:, k * tf:(k + 1) * tf] = (_silu(g) * u).astype(BF16)
            for r in range(k * part, (k + 1) * part):
                pltpu.make_async_copy(h_hbm.at[packed_rows(idx_ref[nxt + r]), :],
                                      xs_stage.at[packed_rows(r), :], gsem).start()
                pltpu.make_async_copy(o_stage.at[packed_rows(r), :],
                                      c_hbm.at[packed_rows(dst_ref[prv + r]), :], ssem).start(priority=r % 2)

    for n in range(nf):
        @pl.when(f == nf + n)
        def _down(n=n):
            if n == 0:
                scatter_done()

            out = jnp.dot(hid[...], wd_ref[...].astype(BF16), preferred_element_type=F32)
            for j in range(tm // LANES):
                rows = out[j * LANES:(j + 1) * LANES, :] * gt_ref[:, j:j + 1]
                bits = pltpu.bitcast(rows.astype(BF16).astype(F32), I32)
                packed = lax.shift_right_logical(bits[:, :tn // 2], 16) | bits[:, tn // 2:]
                for c in range(tn // 2 // LANES):
                    o_stage[pl.ds(j * LANES * PACKED_ROWS + n * (tn // 2 // LANES) + c, LANES,
                                  stride=PACKED_ROWS), :] = packed[:, c * LANES:(c + 1) * LANES]

            if n == nf - 1:
                @pl.when(is_last)
                def _():
                    scatter_rows(base, 0, tm)
                    scatter_done()
                    gather_done()


def _ffn(idx, dst, h_rows, g_t, w_gate, w_up, w_down, tm, tf):
    E, n_m = g_t.shape[0], g_t.shape[1]
    cap = n_m * tm
    n_contrib = E * cap
    nf = D_FF // tf
    tn = D_MODEL // nf
    assert tn == CONTRIB_TN
    up = lambda f: jnp.minimum(f, nf - 1)
    down = lambda f: jnp.maximum(f - nf, 0)
    grid_spec = pltpu.PrefetchScalarGridSpec(
        num_scalar_prefetch=2,
        grid=(E, n_m, 2 * nf),
        in_specs=[pl.BlockSpec(memory_space=pl.ANY),
                  pl.BlockSpec((None, None, LANES, tm // LANES), lambda e, m, f, i, d: (e, m, 0, 0)),
                  pl.BlockSpec((None, D_MODEL, tf), lambda e, m, f, i, d: (e, 0, up(f))),
                  pl.BlockSpec((None, D_MODEL, tf), lambda e, m, f, i, d: (e, 0, up(f))),
                  pl.BlockSpec((None, D_FF, tn), lambda e, m, f, i, d: (e, 0, down(f)))],
        out_specs=pl.BlockSpec(memory_space=pl.ANY),
        scratch_shapes=[pltpu.VMEM((tm * PACKED_ROWS, LANES), I32), pltpu.VMEM((tm, D_MODEL), BF16),
                        pltpu.VMEM((tm, D_FF), BF16), pltpu.VMEM((tm * PACKED_ROWS, LANES), I32),
                        pltpu.SemaphoreType.DMA, pltpu.SemaphoreType.DMA],
    )
    return pl.pallas_call(
        functools.partial(_ffn_kernel, tm=tm, cap=cap, nf=nf),
        grid_spec=grid_spec,
        out_shape=jax.ShapeDtypeStruct((n_contrib * PACKED_ROWS, LANES), I32),
        compiler_params=_cparams(3, FFN_VMEM_LIMIT),
        name="expert_ffn",
    )(idx, dst, h_rows, g_t, w_gate, w_up, w_down)


def _combine_kernel(ch_ref, tt_ref, flag_ref, x1_ref, cs_ref, ce_ref, c_ref, nw_ref, out_ref, acc_ref, *, cg, rows):
    k = pl.program_id(0)
    flag = flag_ref[k]

    @pl.when((flag & 2) > 0)
    def _():
        acc_ref[...] = x1_ref[...]

    @pl.when((flag & 1) > 0)
    def _():
        cio = (lax.broadcasted_iota(I32, (cg, LANES), 0) + ch_ref[k] * cg).astype(F32)
        parts = [jnp.where((cs_ref[r:r + 1, :] <= cio) & (cio < ce_ref[r:r + 1, :]), 1.0, 0.0) for r in range(rows)]
        a = jnp.concatenate(parts, axis=1).T.astype(BF16)
        blocks = CONTRIB_TN // 2 // LANES
        cols = []
        for n in range(D_MODEL // CONTRIB_TN):
            los, his = [], []
            for c in range(blocks):
                u = c_ref[pl.ds(n * blocks + c, cg, stride=PACKED_ROWS), :]
                los.append(pltpu.bitcast(lax.shift_left(u, 16), F32).astype(BF16))
                his.append(pltpu.bitcast(u & jnp.int32(-65536), F32).astype(BF16))
            cols += los + his
        chunk = jnp.concatenate(cols, axis=1)
        acc_ref[...] += jnp.dot(a, chunk, preferred_element_type=F32)

    @pl.when((flag & 4) > 0)
    def _():
        out_ref[...] = _rms(acc_ref[...], nw_ref[...])


def _combine(ch, tt, flag, x1, cs3, ce3, contrib, final_w, tb, cg):
    T = x1.shape[0]
    rows = tb // LANES
    n_pairs = ch.shape[0]
    grid_spec = pltpu.PrefetchScalarGridSpec(
        num_scalar_prefetch=3,
        grid=(n_pairs,),
        in_specs=[pl.BlockSpec((tb, D_MODEL), lambda k, ch, tt, fl: (tt[k], 0)),
                  pl.BlockSpec((None, rows, LANES), lambda k, ch, tt, fl: (tt[k], 0, 0)),
                  pl.BlockSpec((None, rows, LANES), lambda k, ch, tt, fl: (tt[k], 0, 0)),
                  pl.BlockSpec((cg * PACKED_ROWS, LANES), lambda k, ch, tt, fl: (ch[k], 0)),
                  pl.BlockSpec((1, D_MODEL), lambda k, ch, tt, fl: (0, 0))],
        out_specs=pl.BlockSpec((tb, D_MODEL), lambda k, ch, tt, fl: (tt[k], 0)),
        scratch_shapes=[pltpu.VMEM((tb, D_MODEL), F32)],
    )
    return pl.pallas_call(
        functools.partial(_combine_kernel, cg=cg, rows=rows),
        grid_spec=grid_spec,
        out_shape=jax.ShapeDtypeStruct((T, D_MODEL), F32),
        compiler_params=_cparams(1, VMEM_LIMIT),
        name="combine_final",
    )(ch, tt, flag, x1, cs3, ce3, contrib, final_w)


def _combine_schedule(cs, ce, tb, cg):
    T = cs.size
    ntt = T // tb
    total = CAPACITY_FACTOR * T
    nch = total // cg
    lo = cs.reshape(ntt, tb)[:, 0].astype(I32)
    hi = ce.reshape(ntt, tb)[:, -1].astype(I32)
    first = jnp.minimum(lo, total - 1) // cg
    last = jnp.where(hi > lo, (hi - 1) // cg, first)
    n = last - first + 1
    ends = jnp.cumsum(n)
    offs = ends - n
    n_pairs = ntt + nch
    ks = jnp.arange(n_pairs, dtype=I32)
    tile = jnp.minimum(jnp.searchsorted(ends, ks, side="right").astype(I32), ntt - 1)
    valid = ks < ends[-1]
    chunk = jnp.where(valid, first[tile] + ks - offs[tile], last[-1])
    flag = jnp.where(valid, 1 + 2 * (ks == offs[tile]) + 4 * (ks == ends[tile] - 1), 0).astype(I32)
    return chunk.astype(I32), tile, flag


def _t5_buckets(rel):
    nb = NUM_BUCKETS // 2
    ret = (rel > 0).astype(np.int32) * nb
    n = np.abs(rel)
    max_exact = nb // 2
    large = max_exact + (np.log(np.maximum(n, 1) / max_exact) / np.log(MAX_DISTANCE / max_exact)
                         * (nb - max_exact)).astype(np.int32)
    large = np.minimum(large, nb - 1)
    return ret + np.where(n < max_exact, n, large)


def _tile(n, pref):
    t = min(n, pref)
    assert n % t == 0
    return t


def _trunk(x, w):
    B, S, _ = x.shape
    T = B * S
    cap = CAPACITY_FACTOR * T // N_EXPERTS
    x2 = x.reshape(T, D_MODEL)
    tm = _tile(T, 512)

    qkv, main = _inproj(x2, w["norm1_w"], w["w_qkv"], w["w_main"], tm)
    main3 = main.reshape(B, S, MAIN_COLS)
    attn = _attention(qkv.reshape(B, S, QKV_COLS), w["bias_tab"], w["attn_sink"], w["attn_norm_w"])
    yf = _ssd_pass(main3, w["conv_w"], w["conv_b"], w["dtb"], w["alog"], reverse=False)
    ssm = _ssd_pass(main3, w["conv_w"], w["conv_b"], w["dtb"], w["alog"], reverse=True,
                    yf=yf, dskip=w["dskip"], norm_w=w["ssm_norm_w"])
    x1, h_rows, aff = _outproj(x2, attn.reshape(T, D_ATTN), ssm.reshape(T, D_SSM), w["w_out"], w["norm2_w"],
                          w["router_w2"], tm)

    tb = _tile(T, 512)
    cg = 512
    idx, dst, g, cs, ce = _route(aff.T.reshape(N_EXPERTS, T // LANES, LANES), cap)
    tmf = _tile(cap, 1024)
    g_t = jnp.swapaxes(g.reshape(N_EXPERTS, cap // tmf, tmf // LANES, LANES), 2, 3)
    contrib = _ffn(idx.reshape(-1), dst.reshape(-1), h_rows, g_t, w["w_gate"], w["w_up"], w["w_down"], tmf, 512)
    ch, tt, flag = _combine_schedule(cs, ce, tb, cg)
    tile3 = lambda a: a.reshape(T // tb, tb // LANES, LANES)
    y = _combine(ch, tt, flag, x1, tile3(cs), tile3(ce), contrib, w["final_norm_w"], tb, cg)
    return y.reshape(B, S, D_MODEL)


def _prep_weights(rel_bias, norm1_w, w_in, conv_w, conv_b, dt_bias_fwd, dt_bias_bwd, a_log_fwd, a_log_bwd,
                  d_skip, ssm_norm_w, attn_sink, attn_norm_w, w_out, norm2_w, router_w, w_gate, w_up, w_down,
                  final_norm_w):
    o1 = D_ATTN; o2 = o1 + D_KV; o3 = o2 + D_KV; o4 = o3 + D_SSM; o5 = o4 + D_SSM + 2 * D_BC
    wi = w_in[0]
    scale = 1.0 / math.sqrt(HEAD_DIM)
    perm = np.concatenate([np.arange(h * HEAD_DIM, (h + 1) * HEAD_DIM) for h in ATTN_HEAD_ORDER])
    w_qkv = jnp.concatenate([(wi[:, :o1] * scale)[:, perm], wi[:, o1:o3]], axis=1).astype(BF16)
    w_out_p = jnp.concatenate([w_out[0][:D_ATTN][perm], w_out[0][D_ATTN:]], axis=0).astype(BF16)
    w_main = jnp.concatenate([wi[:, o3:], jnp.zeros((D_MODEL, LANES - 2 * SSM_HEADS), F32)], axis=1).astype(BF16)
    rel = np.arange(3 * BLOCK)[None, :] - BLOCK - np.arange(BLOCK)[:, None]
    onehot = (jnp.asarray(_t5_buckets(rel), I32)[..., None] == jnp.arange(NUM_BUCKETS, dtype=I32)).astype(F32)
    bias_tab = jnp.einsum("qsn,nh->hqs", onehot, rel_bias.astype(F32), precision=lax.Precision.HIGHEST)
    bias_tab = jnp.where(jnp.asarray(np.abs(rel) <= WINDOW)[None], bias_tab, NEG_INF)
    rw = router_w[0].astype(F32)
    rw_hi = rw.astype(BF16)
    router_w2 = jnp.concatenate([rw_hi, (rw - rw_hi.astype(F32)).astype(BF16)], axis=1)
    pad = jnp.zeros((LANES - 2 * SSM_HEADS,), F32)
    row = lambda a: a.reshape(1, -1).astype(F32)
    return dict(
        norm1_w=row(norm1_w[0]), w_qkv=w_qkv, w_main=w_main, bias_tab=bias_tab, attn_sink=attn_sink[0].astype(F32),
        attn_norm_w=row(attn_norm_w[0][perm]), conv_w=conv_w[0].T.astype(F32), conv_b=row(conv_b[0]),
        dtb=row(jnp.concatenate([dt_bias_fwd[0], dt_bias_bwd[0], pad])),
        alog=row(jnp.concatenate([a_log_fwd[0], a_log_bwd[0], pad])),
        dskip=row(jnp.repeat(d_skip[0], SSM_HEAD_DIM)), ssm_norm_w=row(ssm_norm_w[0]),
        w_out=w_out_p, norm2_w=row(norm2_w[0]), router_w2=router_w2,
        w_gate=w_gate[0], w_up=w_up[0], w_down=w_down[0],
        final_norm_w=row(final_norm_w))


def kernel(x_prompt, x_sample, rel_bias, norm1_w, w_in, conv_w, conv_b, dt_bias_fwd, dt_bias_bwd, a_log_fwd, a_log_bwd, d_skip, ssm_norm_w, attn_sink, attn_norm_w, w_out, norm2_w, router_w, w_gate, w_up, w_down, final_norm_w):
    assert norm1_w.shape[0] == 1
    w = _prep_weights(rel_bias, norm1_w, w_in, conv_w, conv_b, dt_bias_fwd, dt_bias_bwd, a_log_fwd, a_log_bwd,
                      d_skip, ssm_norm_w, attn_sink, attn_norm_w, w_out, norm2_w, router_w, w_gate, w_up,
                      w_down, final_norm_w)
    return (_trunk(x_prompt, w), _trunk(x_sample, w))
```
